```python
import math
import jax, jax.numpy as jnp
from jax import lax
import numpy as np

D_MODEL = 2048
BATCH = 2
SEQ = 8192
DEPTH = 1

N_MEM = 256
DA_HEAD_DIM = 64
DA_WIDTH = D_MODEL // 2
DA_HEADS = DA_WIDTH // (2 * DA_HEAD_DIM)
DA_ROT_DIM = DA_HEAD_DIM // 4
SB_HEAD_DIM = 128
SB_WIDTH = D_MODEL // 2
SB_HEADS = SB_WIDTH // SB_HEAD_DIM
XA_HEADS = 4
XA_HEAD_DIM = 128
XA_WIDTH = XA_HEADS * XA_HEAD_DIM
D_FF = 256 * ((8 * D_MODEL // 3 + 255) // 256)
CONV_WIDTH = 3
ROPE_THETA = 500000.0
Q_BLOCK = 128
EPS = 1e-6
IN_WIDTHS = (DA_WIDTH, DA_WIDTH, DA_WIDTH, SB_WIDTH, SB_WIDTH, SB_WIDTH, D_MODEL, D_MODEL)
N_IN = 3 * DA_WIDTH + 3 * SB_WIDTH + 2 * D_MODEL

kernel_name = "hybrid_diffattn_stickbreaking_gated_block"


def rms_norm(x, g):
    xf = x.astype(jnp.float32)
    y = xf * lax.rsqrt(jnp.mean(xf * xf, axis=-1, keepdims=True) + EPS)
    return (y * g.astype(jnp.float32)).astype(x.dtype)


def rope_tables(seq, rot_dim):
    inv = ROPE_THETA ** (-jnp.arange(0, rot_dim, 2, dtype=jnp.float32) / rot_dim)
    ang = jnp.arange(seq, dtype=jnp.float32)[:, None] * inv[None, :]
    return jnp.cos(ang), jnp.sin(ang)


def partial_rope(x, cos, sin):
    half = cos.shape[-1]
    shape = (1, x.shape[1]) + (1,) * (x.ndim - 3) + (half,)
    c = cos.reshape(shape).astype(x.dtype)
    s = sin.reshape(shape).astype(x.dtype)
    x1, x2, xp = x[..., :half], x[..., half:2 * half], x[..., 2 * half:]
    return jnp.concatenate([x1 * c - x2 * s, x1 * s + x2 * c, xp], axis=-1)


def split_columns(z, widths):
    outs, start = [], 0
    for w in widths:
        outs.append(z[..., start:start + w])
        start += w
    return outs


def differential_attention(q, k, v, lam):
    b, s, h, _, d = q.shape
    nb = s // Q_BLOCK
    scale = d ** -0.5
    qh = q.transpose(0, 2, 3, 1, 4)
    qblk = qh.reshape(b, h, 2, nb, Q_BLOCK, d).transpose(3, 0, 1, 2, 4, 5)
    kh = k.transpose(0, 2, 3, 1, 4)
    vh = v.transpose(0, 2, 1, 3)
    key_pos = jnp.arange(s)

    def block(args):
        qi, i = args
        q_pos = i * Q_BLOCK + jnp.arange(Q_BLOCK)
        sc = jnp.einsum('bhmqd,bhmkd->bhmqk', qi, kh).astype(jnp.float32) * scale
        causal = key_pos[None, :] <= q_pos[:, None]
        p = jax.nn.softmax(jnp.where(causal, sc, -jnp.inf), axis=-1)
        a = p[:, :, 0] - lam * p[:, :, 1]
        return jnp.einsum('bhqk,bhkv->bhqv', a.astype(vh.dtype), vh)

    o = lax.map(block, (qblk, jnp.arange(nb)))
    return o.transpose(1, 0, 3, 2, 4).reshape(b, s, h, 2 * d)


def stick_breaking_attention(q, k, v):
    b, s, h, d = q.shape
    nb = s // Q_BLOCK
    scale = d ** -0.5
    qh = q.transpose(0, 2, 1, 3)
    qblk = qh.reshape(b, h, nb, Q_BLOCK, d).transpose(2, 0, 1, 3, 4)
    kh = k.transpose(0, 2, 1, 3)
    vh = v.transpose(0, 2, 1, 3)
    key_pos = jnp.arange(s)

    def block(args):
        qi, i = args
        q_pos = i * Q_BLOCK + jnp.arange(Q_BLOCK)
        z = jnp.einsum('bhqd,bhkd->bhqk', qi, kh).astype(jnp.float32) * scale
        strict = key_pos[None, :] < q_pos[:, None]
        log_beta = jax.nn.log_sigmoid(z)
        log_1m_beta = jnp.where(strict, jax.nn.log_sigmoid(-z), 0.0)
        tail = lax.cumsum(log_1m_beta, axis=log_1m_beta.ndim - 1, reverse=True) - log_1m_beta
        a = jnp.where(strict, jnp.exp(log_beta + tail), 0.0)
        return jnp.einsum('bhqk,bhkd->bhqd', a.astype(vh.dtype), vh)

    o = lax.map(block, (qblk, jnp.arange(nb)))
    return o.transpose(1, 0, 3, 2, 4).reshape(b, s, h, d)


def parallel_mixer(h, w_in, lambda_q1, lambda_k1, lambda_q2, lambda_k2, da_subln_g,
                   w_proj_a, w_proj_b, w_out, cos, sin, lam_init):
    b, s, _ = h.shape
    z = h @ w_in
    qa, ka, va, qb, kb, vb, ga, gb = split_columns(z, IN_WIDTHS)
    qa = partial_rope(qa.reshape(b, s, DA_HEADS, 2, DA_HEAD_DIM), cos, sin)
    ka = partial_rope(ka.reshape(b, s, DA_HEADS, 2, DA_HEAD_DIM), cos, sin)
    va = va.reshape(b, s, DA_HEADS, 2 * DA_HEAD_DIM)
    lq1, lk1 = lambda_q1.astype(jnp.float32), lambda_k1.astype(jnp.float32)
    lq2, lk2 = lambda_q2.astype(jnp.float32), lambda_k2.astype(jnp.float32)
    lam = jnp.exp(jnp.sum(lq1 * lk1)) - jnp.exp(jnp.sum(lq2 * lk2)) + lam_init
    oa = differential_attention(qa, ka, va, lam)
    oa = (rms_norm(oa, da_subln_g) * (1.0 - lam_init)).reshape(b, s, DA_WIDTH)
    ob = stick_breaking_attention(qb.reshape(b, s, SB_HEADS, SB_HEAD_DIM),
                                  kb.reshape(b, s, SB_HEADS, SB_HEAD_DIM),
                                  vb.reshape(b, s, SB_HEADS, SB_HEAD_DIM)).reshape(b, s, SB_WIDTH)
    merged = jax.nn.sigmoid(ga) * (oa @ w_proj_a) + jax.nn.sigmoid(gb) * (ob @ w_proj_b)
    return merged @ w_out


def memory_cross_attention(h, mem_n, w_xq, w_xkv, w_xo):
    b, s, _ = h.shape
    m = mem_n.shape[1]
    q = (h @ w_xq).reshape(b, s, XA_HEADS, XA_HEAD_DIM)
    kv = mem_n @ w_xkv
    k = kv[..., :XA_WIDTH].reshape(b, m, XA_HEADS, XA_HEAD_DIM)
    v = kv[..., XA_WIDTH:].reshape(b, m, XA_HEADS, XA_HEAD_DIM)
    sc = jnp.einsum('bshd,bmhd->bhsm', q, k).astype(jnp.float32) * (XA_HEAD_DIM ** -0.5)
    p = jax.nn.softmax(sc, axis=-1)
    o = jnp.einsum('bhsm,bmhd->bshd', p.astype(v.dtype), v).reshape(b, s, XA_WIDTH)
    return o @ w_xo


def causal_depthwise_conv(u, w, bias):
    c = u.shape[-1]
    y = lax.conv_general_dilated(u, w[:, None, :].astype(u.dtype), window_strides=(1,),
                                 padding=((CONV_WIDTH - 1, 0),),
                                 dimension_numbers=('NWC', 'WIO', 'NWC'),
                                 feature_group_count=c)
    return y + bias.astype(u.dtype)


def conv_ffn(h, w_up, conv_w, conv_b, w_down):
    u = causal_depthwise_conv(h @ w_up, conv_w, conv_b)
    gate, val = u[..., :D_FF], u[..., D_FF:]
    return (jax.nn.silu(gate) * val) @ w_down


def setup_inputs(seed: int = 0) -> dict:
    key = jax.random.key(seed)
    ks = jax.random.split(key, 24)

    def nrm(k, shape, scale):
        return jax.random.normal(k, shape, dtype=jnp.float32) * scale

    def gain(k, shape):
        return 1.0 + nrm(k, shape, 0.01)

    L, D = DEPTH, D_MODEL
    return {
        "x": nrm(ks[0], (BATCH, SEQ, D), 1.0),
        "mem": nrm(ks[1], (BATCH, N_MEM, D), 1.0),
        "norm_mix_g": gain(ks[2], (L, D)),
        "w_in": nrm(ks[3], (L, D, N_IN), D ** -0.5),
        "lambda_q1": nrm(ks[4], (L, DA_HEAD_DIM), 0.1),
        "lambda_k1": nrm(ks[5], (L, DA_HEAD_DIM), 0.1),
        "lambda_q2": nrm(ks[6], (L, DA_HEAD_DIM), 0.1),
        "lambda_k2": nrm(ks[7], (L, DA_HEAD_DIM), 0.1),
        "da_subln_g": gain(ks[8], (L, 2 * DA_HEAD_DIM)),
        "w_proj_a": nrm(ks[9], (L, DA_WIDTH, D), DA_WIDTH ** -0.5),
        "w_proj_b": nrm(ks[10], (L, SB_WIDTH, D), SB_WIDTH ** -0.5),
        "w_out": nrm(ks[11], (L, D, D), D ** -0.5),
        "norm_x_g": gain(ks[12], (L, D)),
        "norm_mem_g": gain(ks[13], (L, D)),
        "w_xq": nrm(ks[14], (L, D, XA_WIDTH), D ** -0.5),
        "w_xkv": nrm(ks[15], (L, D, 2 * XA_WIDTH), D ** -0.5),
        "w_xo": nrm(ks[16], (L, XA_WIDTH, D), XA_WIDTH ** -0.5),
        "norm_ffn_g": gain(ks[17], (L, D)),
        "w_up": nrm(ks[18], (L, D, 2 * D_FF), D ** -0.5),
        "conv_w": nrm(ks[19], (L, CONV_WIDTH, 2 * D_FF), CONV_WIDTH ** -0.5),
        "conv_b": nrm(ks[20], (L, 2 * D_FF), 0.01),
        "w_down": nrm(ks[21], (L, D_FF, D), D_FF ** -0.5),
        "final_norm_g": gain(ks[22], (D,)),
    }


def reference(x, mem, norm_mix_g, w_in, lambda_q1, lambda_k1, lambda_q2, lambda_k2,
              da_subln_g, w_proj_a, w_proj_b, w_out, norm_x_g, norm_mem_g, w_xq, w_xkv,
              w_xo, norm_ffn_g, w_up, conv_w, conv_b, w_down, final_norm_g):
    cos, sin = rope_tables(x.shape[1], DA_ROT_DIM)
    for l in range(DEPTH):
        lam_init = 0.8 - 0.6 * math.exp(-0.3 * l)
        x = x + parallel_mixer(rms_norm(x, norm_mix_g[l]), w_in[l],
                               lambda_q1[l], lambda_k1[l], lambda_q2[l], lambda_k2[l],
                               da_subln_g[l], w_proj_a[l], w_proj_b[l], w_out[l],
                               cos, sin, lam_init)
        x = x + memory_cross_attention(rms_norm(x, norm_x_g[l]), rms_norm(mem, norm_mem_g[l]),
                                       w_xq[l], w_xkv[l], w_xo[l])
        x = x + conv_ffn(rms_norm(x, norm_ffn_g[l]), w_up[l], conv_w[l], conv_b[l], w_down[l])
    return rms_norm(x, final_norm_g)
```

```python
import functools
import math

import jax
import jax.numpy as jnp
from jax import lax
from jax.experimental import pallas as pl
from jax.experimental.pallas import tpu as pltpu

F32 = jnp.float32
BF16 = jnp.bfloat16

D_MODEL = 2048
N_MEM = 256
DA_HEAD_DIM = 64
DA_WIDTH = D_MODEL // 2
DA_HEADS = DA_WIDTH // (2 * DA_HEAD_DIM)
DA_ROT_DIM = DA_HEAD_DIM // 4
SB_HEAD_DIM = 128
SB_WIDTH = D_MODEL // 2
SB_HEADS = SB_WIDTH // SB_HEAD_DIM
XA_HEADS = 4
XA_HEAD_DIM = 128
XA_WIDTH = XA_HEADS * XA_HEAD_DIM
D_FF = 256 * ((8 * D_MODEL // 3 + 255) // 256)
CONV_WIDTH = 3
ROPE_THETA = 500000.0
EPS = 1e-6
N_IN = 3 * DA_WIDTH + 3 * SB_WIDTH + 2 * D_MODEL

LANES = 128
SUBLANES = 8
NEG_BIG = -1e30
VMEM_LIMIT = 48 * 1024 * 1024

COL_QA, COL_KA, COL_VA = 0, DA_WIDTH, 2 * DA_WIDTH
COL_QB, COL_KB, COL_VB = 3 * DA_WIDTH, 3 * DA_WIDTH + SB_WIDTH, 3 * DA_WIDTH + 2 * SB_WIDTH
COL_GA = 3 * DA_WIDTH + 3 * SB_WIDTH
COL_GB = COL_GA + D_MODEL


def _params(sem):
    return pltpu.CompilerParams(dimension_semantics=sem, vmem_limit_bytes=VMEM_LIMIT)


def _rms(x, g):
    return x * lax.rsqrt(jnp.mean(x * x, axis=-1, keepdims=True) + EPS) * g


def _dot(a, b):
    return jnp.dot(a, b, preferred_element_type=F32)


def _dot_nt(a, b):
    return lax.dot_general(a, b, (((1,), (1,)), ((), ())), preferred_element_type=F32)


def _in_proj_kernel(x_ref, g_ref, w_ref, cos_ref, sp_ref, sm_ref, o_ref, h_scr, *, bn):
    j = pl.program_id(1)

    @pl.when(j == 0)
    def _():
        h_scr[...] = _rms(x_ref[...], g_ref[...]).astype(BF16)

    acc = _dot(h_scr[...], w_ref[...])
    col = j * bn
    is_qa = col < COL_KA
    is_qb = jnp.logical_and(col >= COL_QB, col < COL_KB)
    scale = jnp.where(is_qa, DA_HEAD_DIM ** -0.5, jnp.where(is_qb, SB_HEAD_DIM ** -0.5, 1.0))
    acc = acc * scale.astype(F32)

    @pl.when(col < COL_VA)
    def _():
        cos, sp, sm = cos_ref[...], sp_ref[...], sm_ref[...]
        for c in range(bn // LANES):
            a = acc[:, c * LANES:(c + 1) * LANES]
            r = a * cos + pltpu.roll(a, DA_ROT_DIM // 2, 1) * sp \
                + pltpu.roll(a, LANES - DA_ROT_DIM // 2, 1) * sm
            o_ref[:, c * LANES:(c + 1) * LANES] = r.astype(o_ref.dtype)

    @pl.when(col >= COL_VA)
    def _():
        o_ref[...] = acc.astype(o_ref.dtype)


def _rope_lane_tables(seq):
    half = DA_ROT_DIM // 2
    inv = ROPE_THETA ** (-jnp.arange(0, DA_ROT_DIM, 2, dtype=F32) / DA_ROT_DIM)
    ang = jnp.arange(seq, dtype=F32)[:, None] * inv[None, :]
    cos, sin = jnp.cos(ang), jnp.sin(ang)
    lane = jnp.arange(LANES) % DA_HEAD_DIM
    idx = lane % half
    lo = (lane < half)[None, :]
    hi = jnp.logical_and(lane >= half, lane < 2 * half)[None, :]
    cos_t = jnp.where(jnp.logical_or(lo, hi), cos[:, idx], 1.0)
    sp_t = jnp.where(hi, sin[:, idx], 0.0)
    sm_t = jnp.where(lo, -sin[:, idx], 0.0)
    return cos_t.astype(F32), sp_t.astype(F32), sm_t.astype(F32)


def _in_proj(x2d, g, w, seq, *, bm=512, bn=512):
    m, d = x2d.shape
    n = w.shape[1]
    cos_t, sp_t, sm_t = _rope_lane_tables(seq)
    nseq = seq // bm
    tab_spec = pl.BlockSpec((bm, LANES), lambda i, j: (i % nseq, 0))
    return pl.pallas_call(
        functools.partial(_in_proj_kernel, bn=bn),
        out_shape=jax.ShapeDtypeStruct((m, n), BF16),
        grid=(m // bm, n // bn),
        in_specs=[
            pl.BlockSpec((bm, d), lambda i, j: (i, 0)),
            pl.BlockSpec((1, d), lambda i, j: (0, 0)),
            pl.BlockSpec((d, bn), lambda i, j: (0, j)),
            tab_spec, tab_spec, tab_spec,
        ],
        out_specs=pl.BlockSpec((bm, bn), lambda i, j: (i, j)),
        scratch_shapes=[pltpu.VMEM((bm, d), BF16)],
        compiler_params=_params(("parallel", "arbitrary")),
        name="in_proj",
    )(x2d, g.reshape(1, d), w, cos_t, sp_t, sm_t)


def _diff_attn_kernel(q_ref, k_ref, v_ref, lq1_ref, lk1_ref, lq2_ref, lk2_ref, g_ref, o_ref,
                      m1_scr, l1_scr, a1_scr, m2_scr, l2_scr, a2_scr, *, bq, bk, lam_init):
    qi = pl.program_id(2)
    q = q_ref[...]
    lane = lax.broadcasted_iota(jnp.int32, q.shape, 1)
    zero = jnp.zeros_like(q)
    q1 = jnp.where(lane < DA_HEAD_DIM, q, zero)
    q2 = jnp.where(lane >= DA_HEAD_DIM, q, zero)

    m1_scr[...] = jnp.full(m1_scr.shape, NEG_BIG, F32)
    m2_scr[...] = jnp.full(m2_scr.shape, NEG_BIG, F32)
    l1_scr[...] = jnp.zeros(l1_scr.shape, F32)
    l2_scr[...] = jnp.zeros(l2_scr.shape, F32)
    a1_scr[...] = jnp.zeros(a1_scr.shape, F32)
    a2_scr[...] = jnp.zeros(a2_scr.shape, F32)

    def update(s, v, m_scr, l_scr, a_scr):
        m_prev = m_scr[...]
        m_new = jnp.maximum(m_prev, jnp.max(s, axis=-1, keepdims=True))
        alpha = jnp.exp(m_prev - m_new)
        p = jnp.exp(s - m_new)
        l_scr[...] = alpha * l_scr[...] + jnp.sum(p, axis=-1, keepdims=True)
        a_scr[...] = alpha * a_scr[...] + _dot(p.astype(BF16), v)
        m_scr[...] = m_new

    def step(kb, masked):
        start = pl.multiple_of(kb * bk, bk)
        k = k_ref[pl.ds(start, bk), :]
        v = v_ref[pl.ds(start, bk), :]
        s1 = _dot_nt(q1, k)
        s2 = _dot_nt(q2, k)
        if masked:
            row = lax.broadcasted_iota(jnp.int32, (bq, bk), 0)
            colk = lax.broadcasted_iota(jnp.int32, (bq, bk), 1)
            keep = colk <= row
            s1 = jnp.where(keep, s1, NEG_BIG)
            s2 = jnp.where(keep, s2, NEG_BIG)
        update(s1, v, m1_scr, l1_scr, a1_scr)
        update(s2, v, m2_scr, l2_scr, a2_scr)

    def body(kb, carry):
        step(kb, False)
        return carry

    lax.fori_loop(0, qi, body, 0)
    step(qi, True)

    lam = (jnp.exp(jnp.sum(lq1_ref[...] * lk1_ref[...], axis=-1, keepdims=True))
           - jnp.exp(jnp.sum(lq2_ref[...] * lk2_ref[...], axis=-1, keepdims=True)) + lam_init)
    o = a1_scr[...] / l1_scr[...] - lam * (a2_scr[...] / l2_scr[...])
    o = _rms(o, g_ref[...]) * (1.0 - lam_init)
    o_ref[...] = o.astype(o_ref.dtype)


def _diff_attn(z, lq1, lk1, lq2, lk2, subln_g, batch, seq, lam_init, *, bq=256):
    bk = bq
    nq = seq // bq
    hd = 2 * DA_HEAD_DIM
    vec = lambda a: a.reshape(1, -1).astype(F32)
    small = lambda n: pl.BlockSpec((1, n), lambda b, h, i: (0, 0))
    return pl.pallas_call(
        functools.partial(_diff_attn_kernel, bq=bq, bk=bk, lam_init=lam_init),
        out_shape=jax.ShapeDtypeStruct((batch * seq, DA_WIDTH), BF16),
        grid=(batch, DA_HEADS, nq),
        in_specs=[
            pl.BlockSpec((bq, hd), lambda b, h, i: (b * nq + i, COL_QA // hd + h)),
            pl.BlockSpec((seq, hd), lambda b, h, i: (b, COL_KA // hd + h)),
            pl.BlockSpec((seq, hd), lambda b, h, i: (b, COL_VA // hd + h)),
            small(DA_HEAD_DIM), small(DA_HEAD_DIM), small(DA_HEAD_DIM), small(DA_HEAD_DIM),
            small(hd),
        ],
        out_specs=pl.BlockSpec((bq, hd), lambda b, h, i: (b * nq + i, h)),
        scratch_shapes=[
            pltpu.VMEM((bq, 1), F32), pltpu.VMEM((bq, 1), F32), pltpu.VMEM((bq, hd), F32),
            pltpu.VMEM((bq, 1), F32), pltpu.VMEM((bq, 1), F32), pltpu.VMEM((bq, hd), F32),
        ],
        compiler_params=_params(("parallel", "parallel", "arbitrary")),
        name="diff_attn",
    )(z, z, z, vec(lq1), vec(lk1), vec(lq2), vec(lk2), vec(subln_g))


def _sb_attn_kernel(q_ref, k_ref, v_ref, u_ref, o_ref, c_scr, a_scr, *, bq, bk):
    qi = pl.program_id(2)
    q = q_ref[...]
    c_scr[...] = jnp.zeros(c_scr.shape, F32)
    a_scr[...] = jnp.zeros(a_scr.shape, F32)

    def step(kb, masked):
        start = pl.multiple_of(kb * bk, bk)
        k = k_ref[pl.ds(start, bk), :]
        v = v_ref[pl.ds(start, bk), :]
        z = _dot_nt(q, k)
        sp = jnp.log(1.0 + jnp.exp(-jnp.abs(z)))
        log_beta = jnp.minimum(z, 0.0) - sp
        log_1m = log_beta - z
        if masked:
            row = lax.broadcasted_iota(jnp.int32, (bq, bk), 0)
            colk = lax.broadcasted_iota(jnp.int32, (bq, bk), 1)
            strict = colk < row
            log_1m = jnp.where(strict, log_1m, 0.0)
        tail = _dot(log_1m.astype(BF16), u_ref[...]) + c_scr[...]
        a = jnp.exp(log_beta + tail)
        if masked:
            a = jnp.where(strict, a, 0.0)
        a_scr[...] += _dot(a.astype(BF16), v)
        c_scr[...] += jnp.sum(log_1m, axis=-1, keepdims=True)

    step(qi, True)

    def body(t, carry):
        step(qi - 1 - t, False)
        return carry

    lax.fori_loop(0, qi, body, 0)
    o_ref[...] = a_scr[...].astype(o_ref.dtype)


def _sb_attn(z, batch, seq, *, bq=256):
    bk = bq
    nq = seq // bq
    hd = SB_HEAD_DIM
    r = lax.broadcasted_iota(jnp.int32, (bk, bk), 0)
    c = lax.broadcasted_iota(jnp.int32, (bk, bk), 1)
    u = (r > c).astype(BF16)
    return pl.pallas_call(
        functools.partial(_sb_attn_kernel, bq=bq, bk=bk),
        out_shape=jax.ShapeDtypeStruct((batch * seq, SB_WIDTH), BF16),
        grid=(batch, SB_HEADS, nq),
        in_specs=[
            pl.BlockSpec((bq, hd), lambda b, h, i: (b * nq + i, COL_QB // hd + h)),
            pl.BlockSpec((seq, hd), lambda b, h, i: (b, COL_KB // hd + h)),
            pl.BlockSpec((seq, hd), lambda b, h, i: (b, COL_VB // hd + h)),
            pl.BlockSpec((bk, bk), lambda b, h, i: (0, 0)),
        ],
        out_specs=pl.BlockSpec((bq, hd), lambda b, h, i: (b * nq + i, h)),
        scratch_shapes=[pltpu.VMEM((bq, 1), F32), pltpu.VMEM((bq, hd), F32)],
        compiler_params=_params(("parallel", "parallel", "arbitrary")),
        name="sb_attn",
    )(z, z, z, u)


def _sigmoid(x):
    return 1.0 / (1.0 + jnp.exp(-x))


def _merge_kernel(oa_ref, ob_ref, ga_ref, gb_ref, wa_ref, wb_ref, o_ref):
    ya = _dot(oa_ref[...], wa_ref[...])
    yb = _dot(ob_ref[...], wb_ref[...])
    ga = _sigmoid(ga_ref[...].astype(F32))
    gb = _sigmoid(gb_ref[...].astype(F32))
    o_ref[...] = (ga * ya + gb * yb).astype(o_ref.dtype)


def _merge(oa, ob, z, wa, wb, *, bm=512, bn=512):
    m = oa.shape[0]
    return pl.pallas_call(
        _merge_kernel,
        out_shape=jax.ShapeDtypeStruct((m, D_MODEL), BF16),
        grid=(m // bm, D_MODEL // bn),
        in_specs=[
            pl.BlockSpec((bm, DA_WIDTH), lambda i, j: (i, 0)),
            pl.BlockSpec((bm, SB_WIDTH), lambda i, j: (i, 0)),
            pl.BlockSpec((bm, bn), lambda i, j: (i, COL_GA // bn + j)),
            pl.BlockSpec((bm, bn), lambda i, j: (i, COL_GB // bn + j)),
            pl.BlockSpec((DA_WIDTH, bn), lambda i, j: (0, j)),
            pl.BlockSpec((SB_WIDTH, bn), lambda i, j: (0, j)),
        ],
        out_specs=pl.BlockSpec((bm, bn), lambda i, j: (i, j)),
        compiler_params=_params(("parallel", "parallel")),
        name="merge",
    )(oa, ob, z, z, wa, wb)


def _out_proj_kernel(a_ref, w_ref, x_ref, o_ref):
    o_ref[...] = x_ref[...] + _dot(a_ref[...], w_ref[...])


def _out_proj(a, w, x2d, *, bm=512, bn=512):
    m, k = a.shape
    n = w.shape[1]
    return pl.pallas_call(
        _out_proj_kernel,
        out_shape=jax.ShapeDtypeStruct((m, n), F32),
        grid=(m // bm, n // bn),
        in_specs=[
            pl.BlockSpec((bm, k), lambda i, j: (i, 0)),
            pl.BlockSpec((k, bn), lambda i, j: (0, j)),
            pl.BlockSpec((bm, bn), lambda i, j: (i, j)),
        ],
        out_specs=pl.BlockSpec((bm, bn), lambda i, j: (i, j)),
        compiler_params=_params(("parallel", "parallel")),
        name="out_proj",
    )(a, w, x2d)


def _mem_kv_kernel(x_ref, g_ref, w_ref, o_ref):
    h = _rms(x_ref[...], g_ref[...]).astype(BF16)
    o_ref[...] = _dot(h, w_ref[...]).astype(o_ref.dtype)


def _mem_kv(mem2d, g, w, *, bm=256):
    m, d = mem2d.shape
    n = w.shape[1]
    return pl.pallas_call(
        _mem_kv_kernel,
        out_shape=jax.ShapeDtypeStruct((m, n), BF16),
        grid=(m // bm,),
        in_specs=[
            pl.BlockSpec((bm, d), lambda i: (i, 0)),
            pl.BlockSpec((1, d), lambda i: (0, 0)),
            pl.BlockSpec((d, n), lambda i: (0, 0)),
        ],
        out_specs=pl.BlockSpec((bm, n), lambda i: (i, 0)),
        compiler_params=_params(("parallel",)),
        name="mem_kv",
    )(mem2d, g.reshape(1, d), w)


def _xattn_kernel(x_ref, g_ref, wq_ref, kv_ref, wo_ref, o_ref, oh_scr):
    x = x_ref[...]
    h = _rms(x, g_ref[...]).astype(BF16)
    q = (_dot(h, wq_ref[...]) * (XA_HEAD_DIM ** -0.5)).astype(BF16)
    for hh in range(XA_HEADS):
        lo = hh * XA_HEAD_DIM
        k = kv_ref[:, lo:lo + XA_HEAD_DIM]
        v = kv_ref[:, XA_WIDTH + lo:XA_WIDTH + lo + XA_HEAD_DIM]
        s = _dot_nt(q[:, lo:lo + XA_HEAD_DIM], k)
        p = jnp.exp(s - jnp.max(s, axis=-1, keepdims=True))
        p = p / jnp.sum(p, axis=-1, keepdims=True)
        oh_scr[:, lo:lo + XA_HEAD_DIM] = _dot(p.astype(BF16), v).astype(BF16)
    o_ref[...] = x + _dot(oh_scr[...], wo_ref[...])


def _xattn(x2d, g, wq, kv, wo, seq, *, bm=512):
    m, d = x2d.shape
    nseq = seq // bm
    return pl.pallas_call(
        _xattn_kernel,
        out_shape=jax.ShapeDtypeStruct((m, d), F32),
        grid=(m // bm,),
        in_specs=[
            pl.BlockSpec((bm, d), lambda i: (i, 0)),
            pl.BlockSpec((1, d), lambda i: (0, 0)),
            pl.BlockSpec((d, XA_WIDTH), lambda i: (0, 0)),
            pl.BlockSpec((N_MEM, 2 * XA_WIDTH), lambda i: (i // nseq, 0)),
            pl.BlockSpec((XA_WIDTH, d), lambda i: (0, 0)),
        ],
        out_specs=pl.BlockSpec((bm, d), lambda i: (i, 0)),
        scratch_shapes=[pltpu.VMEM((bm, XA_WIDTH), BF16)],
        compiler_params=_params(("parallel",)),
        name="xattn",
    )(x2d, g.reshape(1, d), wq, kv, wo)


def _ffn_kernel(x_ref, halo_ref, g_ref, wg_ref, wv_ref, cwg_ref, cwv_ref, cbg_ref, cbv_ref,
                wd_ref, fg_ref, o_ref, h_scr, acc_scr, *, nseq, final):
    i = pl.program_id(0)
    f = pl.program_id(1)
    pad = SUBLANES

    @pl.when(f == 0)
    def _():
        g = g_ref[...]
        h_scr[pad:, :] = _rms(x_ref[...], g).astype(BF16)
        keep = (i % nseq != 0).astype(F32)
        h_scr[:pad, :] = (_rms(halo_ref[...], g) * keep).astype(BF16)
        acc_scr[...] = jnp.zeros(acc_scr.shape, F32)

    h = h_scr[...]

    def conv(w_ref, cw_ref, cb_ref):
        u = _dot(h, w_ref[...])
        cw = cw_ref[...]
        y = (cw[2:3, :] * u + cw[1:2, :] * pltpu.roll(u, 1, 0) + cw[0:1, :] * pltpu.roll(u, 2, 0))
        return y[pad:, :] + cb_ref[...]

    gate = conv(wg_ref, cwg_ref, cbg_ref)
    val = conv(wv_ref, cwv_ref, cbv_ref)
    act = (gate * _sigmoid(gate) * val).astype(BF16)
    acc_scr[...] += _dot(act, wd_ref[...])

    @pl.when(f == pl.num_programs(1) - 1)
    def _():
        y = x_ref[...] + acc_scr[...]
        o_ref[...] = _rms(y, fg_ref[...]) if final else y


def _ffn(x2d, g, w_up, conv_w, conv_b, w_down, final_g, seq, *, final, bm=512, bf=512):
    m, d = x2d.shape
    nf = D_FF // bf
    nseq = seq // bm
    hb = bm // SUBLANES
    return pl.pallas_call(
        functools.partial(_ffn_kernel, nseq=nseq, final=final),
        out_shape=jax.ShapeDtypeStruct((m, d), F32),
        grid=(m // bm, nf),
        in_specs=[
            pl.BlockSpec((bm, d), lambda i, f: (i, 0)),
            pl.BlockSpec((SUBLANES, d), lambda i, f: (jnp.maximum(i * hb - 1, 0), 0)),
            pl.BlockSpec((1, d), lambda i, f: (0, 0)),
            pl.BlockSpec((d, bf), lambda i, f: (0, f)),
            pl.BlockSpec((d, bf), lambda i, f: (0, nf + f)),
            pl.BlockSpec((CONV_WIDTH, bf), lambda i, f: (0, f)),
            pl.BlockSpec((CONV_WIDTH, bf), lambda i, f: (0, nf + f)),
            pl.BlockSpec((1, bf), lambda i, f: (0, f)),
            pl.BlockSpec((1, bf), lambda i, f: (0, nf + f)),
            pl.BlockSpec((bf, d), lambda i, f: (f, 0)),
            pl.BlockSpec((1, d), lambda i, f: (0, 0)),
        ],
        out_specs=pl.BlockSpec((bm, d), lambda i, f: (i, 0)),
        scratch_shapes=[pltpu.VMEM((bm + SUBLANES, d), BF16), pltpu.VMEM((bm, d), F32)],
        compiler_params=_params(("parallel", "arbitrary")),
        name="ffn",
    )(x2d, x2d, g.reshape(1, d), w_up, w_up, conv_w, conv_w,
      conv_b.reshape(1, -1), conv_b.reshape(1, -1), w_down, final_g.reshape(1, d))


def kernel(x, mem, norm_mix_g, w_in, lambda_q1, lambda_k1, lambda_q2, lambda_k2, da_subln_g,
           w_proj_a, w_proj_b, w_out, norm_x_g, norm_mem_g, w_xq, w_xkv, w_xo, norm_ffn_g,
           w_up, conv_w, conv_b, w_down, final_norm_g):
    batch, seq, d = x.shape
    depth = w_in.shape[0]
    bf = lambda a: a.astype(BF16)
    x2d = x.reshape(batch * seq, d)
    mem2d = mem.reshape(batch * mem.shape[1], d)
    for l in range(depth):
        lam_init = 0.8 - 0.6 * math.exp(-0.3 * l)
        z = _in_proj(x2d, norm_mix_g[l], bf(w_in[l]), seq)
        oa = _diff_attn(z, lambda_q1[l], lambda_k1[l], lambda_q2[l], lambda_k2[l],
                        da_subln_g[l], batch, seq, lam_init)
        ob = _sb_attn(z, batch, seq)
        merged = _merge(oa, ob, z, bf(w_proj_a[l]), bf(w_proj_b[l]))
        x2d = _out_proj(merged, bf(w_out[l]), x2d)
        kv = _mem_kv(mem2d, norm_mem_g[l], bf(w_xkv[l]))
        x2d = _xattn(x2d, norm_x_g[l], bf(w_xq[l]), kv, bf(w_xo[l]), seq)
        x2d = _ffn(x2d, norm_ffn_g[l], bf(w_up[l]), conv_w[l], conv_b[l], bf(w_down[l]),
                   final_norm_g, seq, final=(l == depth - 1))
    return x2d.reshape(batch, seq, d)
```

```python
import functools
import math

import jax
import jax.numpy as jnp
from jax import lax
from jax.experimental import pallas as pl
from jax.experimental.pallas import tpu as pltpu

F32 = jnp.float32
BF16 = jnp.bfloat16

D_MODEL = 2048
N_MEM = 256
DA_HEAD_DIM = 64
DA_WIDTH = D_MODEL // 2
DA_HEADS = DA_WIDTH // (2 * DA_HEAD_DIM)
DA_ROT_DIM = DA_HEAD_DIM // 4
SB_HEAD_DIM = 128
SB_WIDTH = D_MODEL // 2
SB_HEADS = SB_WIDTH // SB_HEAD_DIM
XA_HEADS = 4
XA_HEAD_DIM = 128
XA_WIDTH = XA_HEADS * XA_HEAD_DIM
D_FF = 256 * ((8 * D_MODEL // 3 + 255) // 256)
CONV_WIDTH = 3
ROPE_THETA = 500000.0
EPS = 1e-6
N_IN = 3 * DA_WIDTH + 3 * SB_WIDTH + 2 * D_MODEL

LANES = 128
SUBLANES = 8
BF16_ROWS = 16
NEG_BIG = -1e30
LOG2E = math.log2(math.e)
VMEM_LIMIT = 48 * 1024 * 1024

COL_QA, COL_KA, COL_VA = 0, DA_WIDTH, 2 * DA_WIDTH
COL_QB, COL_KB, COL_VB = 3 * DA_WIDTH, 3 * DA_WIDTH + SB_WIDTH, 3 * DA_WIDTH + 2 * SB_WIDTH
COL_GA = 3 * DA_WIDTH + 3 * SB_WIDTH
COL_GB = COL_GA + D_MODEL


def _params(sem):
    return pltpu.CompilerParams(dimension_semantics=sem, vmem_limit_bytes=VMEM_LIMIT)


def _rms(x, g):
    return x * lax.rsqrt(jnp.mean(x * x, axis=-1, keepdims=True) + EPS) * g


def _dot(a, b):
    return jnp.dot(a, b, preferred_element_type=F32)


def _dot_nt(a, b):
    return lax.dot_general(a, b, (((1,), (1,)), ((), ())), preferred_element_type=F32)


def _in_proj_kernel(x_ref, g_ref, w_ref, cos_ref, sp_ref, sm_ref, o_ref, h_scr, *, bn):
    j = pl.program_id(1)

    @pl.when(j == 0)
    def _():
        h_scr[...] = _rms(x_ref[...], g_ref[...]).astype(BF16)

    acc = _dot(h_scr[...], w_ref[...])
    col = j * bn
    is_qa = col < COL_KA
    is_qb = jnp.logical_and(col >= COL_QB, col < COL_KB)
    scale = jnp.where(is_qa, LOG2E * DA_HEAD_DIM ** -0.5,
                      jnp.where(is_qb, LOG2E * SB_HEAD_DIM ** -0.5, 1.0))
    acc = acc * scale.astype(F32)

    @pl.when(col < COL_VA)
    def _():
        cos, sp, sm = cos_ref[...], sp_ref[...], sm_ref[...]
        for c in range(bn // LANES):
            a = acc[:, c * LANES:(c + 1) * LANES]
            r = a * cos + pltpu.roll(a, DA_ROT_DIM // 2, 1) * sp \
                + pltpu.roll(a, LANES - DA_ROT_DIM // 2, 1) * sm
            o_ref[:, c * LANES:(c + 1) * LANES] = r.astype(o_ref.dtype)

    @pl.when(col >= COL_VA)
    def _():
        o_ref[...] = acc.astype(o_ref.dtype)


def _rope_lane_tables(seq):
    half = DA_ROT_DIM // 2
    inv = ROPE_THETA ** (-jnp.arange(0, DA_ROT_DIM, 2, dtype=F32) / DA_ROT_DIM)
    ang = jnp.arange(seq, dtype=F32)[:, None] * inv[None, :]
    cos, sin = jnp.cos(ang), jnp.sin(ang)
    lane = jnp.arange(LANES) % DA_HEAD_DIM
    idx = lane % half
    lo = (lane < half)[None, :]
    hi = jnp.logical_and(lane >= half, lane < 2 * half)[None, :]
    cos_t = jnp.where(jnp.logical_or(lo, hi), cos[:, idx], 1.0)
    sp_t = jnp.where(hi, sin[:, idx], 0.0)
    sm_t = jnp.where(lo, -sin[:, idx], 0.0)
    return cos_t.astype(F32), sp_t.astype(F32), sm_t.astype(F32)


def _in_proj(x2d, g, w, seq, *, bm=512, bn=512):
    m, d = x2d.shape
    n = w.shape[1]
    cos_t, sp_t, sm_t = _rope_lane_tables(seq)
    nseq = seq // bm
    tab_spec = pl.BlockSpec((bm, LANES), lambda i, j: (i % nseq, 0))
    return pl.pallas_call(
        functools.partial(_in_proj_kernel, bn=bn),
        out_shape=jax.ShapeDtypeStruct((m, n), BF16),
        grid=(m // bm, n // bn),
        in_specs=[
            pl.BlockSpec((bm, d), lambda i, j: (i, 0)),
            pl.BlockSpec((1, d), lambda i, j: (0, 0)),
            pl.BlockSpec((d, bn), lambda i, j: (0, j)),
            tab_spec, tab_spec, tab_spec,
        ],
        out_specs=pl.BlockSpec((bm, bn), lambda i, j: (i, j)),
        scratch_shapes=[pltpu.VMEM((bm, d), BF16)],
        compiler_params=_params(("parallel", "arbitrary")),
        name="in_proj",
    )(x2d, g.reshape(1, d), w, cos_t, sp_t, sm_t)


def _transpose_values(v_ref, vt_scr, bk):
    hd = v_ref.shape[1]
    ones = jnp.ones((vt_scr.shape[1] - hd, bk), BF16)

    def body(kb, carry):
        start = pl.multiple_of(kb * bk, bk)
        vt_scr[kb, :hd, :] = v_ref[pl.ds(start, bk), :].astype(F32).T.astype(BF16)
        vt_scr[kb, hd:, :] = ones
        return carry
    lax.fori_loop(0, vt_scr.shape[0], body, 0)


def _diff_attn_kernel(q_ref, k_ref, v_ref, lq1_ref, lk1_ref, lq2_ref, lk2_ref, g_ref, o_ref,
                      vt_scr, a1_scr, a2_scr, *, bq, bk, lam_init):
    qi = pl.program_id(2)
    hd = q_ref.shape[1]

    @pl.when(qi == 0)
    def _():
        _transpose_values(v_ref, vt_scr, bk)

    qt = q_ref[...].astype(F32).T
    feat = lax.broadcasted_iota(jnp.int32, qt.shape, 0)
    q1t = jnp.where(feat < DA_HEAD_DIM, qt, 0.0).astype(BF16)
    q2t = jnp.where(feat >= DA_HEAD_DIM, qt, 0.0).astype(BF16)

    a1_scr[...] = jnp.zeros(a1_scr.shape, F32)
    a2_scr[...] = jnp.zeros(a2_scr.shape, F32)

    def update(s, vt, m_prev, a_scr):
        m_new = jnp.maximum(m_prev, jnp.max(s, axis=0, keepdims=True))
        alpha = jnp.exp2(m_prev - m_new)
        p = jnp.exp2(s - m_new).astype(BF16)
        a_scr[...] = alpha * a_scr[...] + _dot(vt, p)
        return m_new

    def step(kb, carry, masked):
        m1, m2 = carry
        start = pl.multiple_of(kb * bk, bk)
        k = k_ref[pl.ds(start, bk), :]
        vt = vt_scr[kb]
        s1 = _dot(k, q1t)
        s2 = _dot(k, q2t)
        if masked:
            key = start + lax.broadcasted_iota(jnp.int32, (bk, bq), 0)
            qry = qi * bq + lax.broadcasted_iota(jnp.int32, (bk, bq), 1)
            keep = key <= qry
            s1 = jnp.where(keep, s1, NEG_BIG)
            s2 = jnp.where(keep, s2, NEG_BIG)
        return update(s1, vt, m1, a1_scr), update(s2, vt, m2, a2_scr)

    neg = jnp.full((1, bq), NEG_BIG, F32)
    n_full = (qi * bq) // bk
    carry = lax.fori_loop(0, n_full, lambda kb, c: step(kb, c, False), (neg, neg))
    step(n_full, carry, True)

    lam = (jnp.exp(jnp.sum(lq1_ref[...] * lk1_ref[...], axis=-1, keepdims=True))
           - jnp.exp(jnp.sum(lq2_ref[...] * lk2_ref[...], axis=-1, keepdims=True)) + lam_init)
    ot = (a1_scr[:hd, :] / a1_scr[hd:hd + 1, :]
          - lam * (a2_scr[:hd, :] / a2_scr[hd:hd + 1, :]))
    o = _rms(ot.T, g_ref[...]) * (1.0 - lam_init)
    o_ref[...] = o.astype(o_ref.dtype)


def _diff_attn(z, lq1, lk1, lq2, lk2, subln_g, batch, seq, lam_init, *, bq=256, bk=1024):
    nq = seq // bq
    hd = 2 * DA_HEAD_DIM
    vec = lambda a: a.reshape(1, -1).astype(F32)
    small = lambda n: pl.BlockSpec((1, n), lambda b, h, i: (0, 0))
    return pl.pallas_call(
        functools.partial(_diff_attn_kernel, bq=bq, bk=bk, lam_init=lam_init),
        out_shape=jax.ShapeDtypeStruct((batch * seq, DA_WIDTH), BF16),
        grid=(batch, DA_HEADS, nq),
        in_specs=[
            pl.BlockSpec((bq, hd), lambda b, h, i: (b * nq + i, COL_QA // hd + h)),
            pl.BlockSpec((seq, hd), lambda b, h, i: (b, COL_KA // hd + h)),
            pl.BlockSpec((seq, hd), lambda b, h, i: (b, COL_VA // hd + h)),
            small(DA_HEAD_DIM), small(DA_HEAD_DIM), small(DA_HEAD_DIM), small(DA_HEAD_DIM),
            small(hd),
        ],
        out_specs=pl.BlockSpec((bq, hd), lambda b, h, i: (b * nq + i, h)),
        scratch_shapes=[
            pltpu.VMEM((seq // bk, hd + BF16_ROWS, bk), BF16),
            pltpu.VMEM((hd + BF16_ROWS, bq), F32), pltpu.VMEM((hd + BF16_ROWS, bq), F32),
        ],
        compiler_params=_params(("arbitrary", "arbitrary", "arbitrary")),
        name="diff_attn",
    )(z, z, z, vec(lq1), vec(lk1), vec(lq2), vec(lk2), vec(subln_g))


def _sb_attn_kernel(q_ref, k_ref, v_ref, u_ref, o_ref, vt_scr, a_scr, *, bq, bk, bc):
    qi = pl.program_id(2)
    hd = q_ref.shape[1]
    nsub = bk // bc

    @pl.when(qi == 0)
    def _():
        _transpose_values(v_ref, vt_scr, bk)

    qt = q_ref[...].astype(F32).T.astype(BF16)
    a_scr[...] = jnp.zeros(a_scr.shape, F32)
    u = u_ref[...]

    def step(kb, c, masked):
        start = pl.multiple_of(kb * bk, bk)
        k = k_ref[pl.ds(start, bk), :]
        z = _dot(k, qt)
        sp = jnp.log2(1.0 + jnp.exp2(-jnp.abs(z)))
        log_beta = jnp.minimum(z, 0.0) - sp
        log_1m = log_beta - z
        if masked:
            key = start + lax.broadcasted_iota(jnp.int32, (bk, bq), 0)
            qry = qi * bq + lax.broadcasted_iota(jnp.int32, (bk, bq), 1)
            strict = key < qry
            log_1m = jnp.where(strict, log_1m, 0.0)
        l16 = log_1m.astype(BF16)
        tails = [None] * nsub
        for sb in reversed(range(nsub)):
            lo = sb * bc
            blk = l16[lo:lo + bc, :]
            t = _dot(u, blk) + c
            tails[sb] = t
            c = t[0:1, :] + blk[0:1, :].astype(F32)
        a = jnp.exp2(log_beta + jnp.concatenate(tails, axis=0))
        if masked:
            a = jnp.where(strict, a, 0.0)
        a_scr[...] += _dot(vt_scr[kb, :hd, :], a.astype(BF16))
        return c

    n_full = (qi * bq) // bk
    c = step(n_full, jnp.zeros((1, bq), F32), True)
    lax.fori_loop(0, n_full, lambda t, cc: step(n_full - 1 - t, cc, False), c)
    o_ref[...] = a_scr[...].T.astype(o_ref.dtype)


def _sb_attn(z, batch, seq, *, bq=256, bk=1024, bc=256):
    nq = seq // bq
    hd = SB_HEAD_DIM
    r = lax.broadcasted_iota(jnp.int32, (bc, bc), 0)
    c = lax.broadcasted_iota(jnp.int32, (bc, bc), 1)
    u = (c > r).astype(BF16)
    return pl.pallas_call(
        functools.partial(_sb_attn_kernel, bq=bq, bk=bk, bc=bc),
        out_shape=jax.ShapeDtypeStruct((batch * seq, SB_WIDTH), BF16),
        grid=(batch, SB_HEADS, nq),
        in_specs=[
            pl.BlockSpec((bq, hd), lambda b, h, i: (b * nq + i, COL_QB // hd + h)),
            pl.BlockSpec((seq, hd), lambda b, h, i: (b, COL_KB // hd + h)),
            pl.BlockSpec((seq, hd), lambda b, h, i: (b, COL_VB // hd + h)),
            pl.BlockSpec((bc, bc), lambda b, h, i: (0, 0)),
        ],
        out_specs=pl.BlockSpec((bq, hd), lambda b, h, i: (b * nq + i, h)),
        scratch_shapes=[pltpu.VMEM((seq // bk, hd + BF16_ROWS, bk), BF16), pltpu.VMEM((hd, bq), F32)],
        compiler_params=_params(("arbitrary", "arbitrary", "arbitrary")),
        name="sb_attn",
    )(z, z, z, u)


def _sigmoid(x):
    return 1.0 / (1.0 + jnp.exp(-x))


def _merge_kernel(oa_ref, ob_ref, ga_ref, gb_ref, wa_ref, wb_ref, o_ref):
    ya = _dot(oa_ref[...], wa_ref[...])
    yb = _dot(ob_ref[...], wb_ref[...])
    ga = _sigmoid(ga_ref[...].astype(F32))
    gb = _sigmoid(gb_ref[...].astype(F32))
    o_ref[...] = (ga * ya + gb * yb).astype(o_ref.dtype)


def _merge(oa, ob, z, wa, wb, *, bm=512, bn=512):
    m = oa.shape[0]
    return pl.pallas_call(
        _merge_kernel,
        out_shape=jax.ShapeDtypeStruct((m, D_MODEL), BF16),
        grid=(m // bm, D_MODEL // bn),
        in_specs=[
            pl.BlockSpec((bm, DA_WIDTH), lambda i, j: (i, 0)),
            pl.BlockSpec((bm, SB_WIDTH), lambda i, j: (i, 0)),
            pl.BlockSpec((bm, bn), lambda i, j: (i, COL_GA // bn + j)),
            pl.BlockSpec((bm, bn), lambda i, j: (i, COL_GB // bn + j)),
            pl.BlockSpec((DA_WIDTH, bn), lambda i, j: (0, j)),
            pl.BlockSpec((SB_WIDTH, bn), lambda i, j: (0, j)),
        ],
        out_specs=pl.BlockSpec((bm, bn), lambda i, j: (i, j)),
        compiler_params=_params(("parallel", "parallel")),
        name="merge",
    )(oa, ob, z, z, wa, wb)


def _out_proj_kernel(a_ref, w_ref, x_ref, o_ref):
    o_ref[...] = x_ref[...] + _dot(a_ref[...], w_ref[...])


def _out_proj(a, w, x2d, *, bm=512, bn=512):
    m, k = a.shape
    n = w.shape[1]
    return pl.pallas_call(
        _out_proj_kernel,
        out_shape=jax.ShapeDtypeStruct((m, n), F32),
        grid=(m // bm, n // bn),
        in_specs=[
            pl.BlockSpec((bm, k), lambda i, j: (i, 0)),
            pl.BlockSpec((k, bn), lambda i, j: (0, j)),
            pl.BlockSpec((bm, bn), lambda i, j: (i, j)),
        ],
        out_specs=pl.BlockSpec((bm, bn), lambda i, j: (i, j)),
        compiler_params=_params(("parallel", "parallel")),
        name="out_proj",
    )(a, w, x2d)


def _mem_kv_kernel(x_ref, g_ref, w_ref, o_ref):
    h = _rms(x_ref[...], g_ref[...]).astype(BF16)
    o_ref[...] = _dot(h, w_ref[...]).astype(o_ref.dtype)


def _mem_kv(mem2d, g, w, *, bm=256):
    m, d = mem2d.shape
    n = w.shape[1]
    return pl.pallas_call(
        _mem_kv_kernel,
        out_shape=jax.ShapeDtypeStruct((m, n), BF16),
        grid=(m // bm,),
        in_specs=[
            pl.BlockSpec((bm, d), lambda i: (i, 0)),
            pl.BlockSpec((1, d), lambda i: (0, 0)),
            pl.BlockSpec((d, n), lambda i: (0, 0)),
        ],
        out_specs=pl.BlockSpec((bm, n), lambda i: (i, 0)),
        compiler_params=_params(("parallel",)),
        name="mem_kv",
    )(mem2d, g.reshape(1, d), w)


def _xattn_kernel(x_ref, g_ref, wq_ref, kv_ref, wo_ref, o_ref, oh_scr):
    x = x_ref[...]
    h = _rms(x, g_ref[...]).astype(BF16)
    q = (_dot(h, wq_ref[...]) * (XA_HEAD_DIM ** -0.5)).astype(BF16)
    for hh in range(XA_HEADS):
        lo = hh * XA_HEAD_DIM
        k = kv_ref[:, lo:lo + XA_HEAD_DIM]
        v = kv_ref[:, XA_WIDTH + lo:XA_WIDTH + lo + XA_HEAD_DIM]
        s = _dot_nt(q[:, lo:lo + XA_HEAD_DIM], k)
        p = jnp.exp(s - jnp.max(s, axis=-1, keepdims=True))
        p = p / jnp.sum(p, axis=-1, keepdims=True)
        oh_scr[:, lo:lo + XA_HEAD_DIM] = _dot(p.astype(BF16), v).astype(BF16)
    o_ref[...] = x + _dot(oh_scr[...], wo_ref[...])


def _xattn(x2d, g, wq, kv, wo, seq, *, bm=512):
    m, d = x2d.shape
    nseq = seq // bm
    return pl.pallas_call(
        _xattn_kernel,
        out_shape=jax.ShapeDtypeStruct((m, d), F32),
        grid=(m // bm,),
        in_specs=[
            pl.BlockSpec((bm, d), lambda i: (i, 0)),
            pl.BlockSpec((1, d), lambda i: (0, 0)),
            pl.BlockSpec((d, XA_WIDTH), lambda i: (0, 0)),
            pl.BlockSpec((N_MEM, 2 * XA_WIDTH), lambda i: (i // nseq, 0)),
            pl.BlockSpec((XA_WIDTH, d), lambda i: (0, 0)),
        ],
        out_specs=pl.BlockSpec((bm, d), lambda i: (i, 0)),
        scratch_shapes=[pltpu.VMEM((bm, XA_WIDTH), BF16)],
        compiler_params=_params(("parallel",)),
        name="xattn",
    )(x2d, g.reshape(1, d), wq, kv, wo)


def _ffn_kernel(x_ref, halo_ref, g_ref, wg_ref, wv_ref, cwg_ref, cwv_ref, cbg_ref, cbv_ref,
                wd_ref, fg_ref, o_ref, h_scr, acc_scr, *, nseq, final):
    i = pl.program_id(0)
    f = pl.program_id(1)
    pad = SUBLANES

    @pl.when(f == 0)
    def _():
        g = g_ref[...]
        h_scr[pad:, :] = _rms(x_ref[...], g).astype(BF16)
        keep = (i % nseq != 0).astype(F32)
        h_scr[:pad, :] = (_rms(halo_ref[...], g) * keep).astype(BF16)
        acc_scr[...] = jnp.zeros(acc_scr.shape, F32)

    h = h_scr[...]

    def conv(w_ref, cw_ref, cb_ref):
        u = _dot(h, w_ref[...])
        cw = cw_ref[...]
        y = (cw[2:3, :] * u + cw[1:2, :] * pltpu.roll(u, 1, 0) + cw[0:1, :] * pltpu.roll(u, 2, 0))
        return y[pad:, :] + cb_ref[...]

    gate = conv(wg_ref, cwg_ref, cbg_ref)
    val = conv(wv_ref, cwv_ref, cbv_ref)
    act = (gate * _sigmoid(gate) * val).astype(BF16)
    acc_scr[...] += _dot(act, wd_ref[...])

    @pl.when(f == pl.num_programs(1) - 1)
    def _():
        y = x_ref[...] + acc_scr[...]
        o_ref[...] = _rms(y, fg_ref[...]) if final else y


def _ffn(x2d, g, w_up, conv_w, conv_b, w_down, final_g, seq, *, final, bm=512, bf=512):
    m, d = x2d.shape
    nf = D_FF // bf
    nseq = seq // bm
    hb = bm // SUBLANES
    return pl.pallas_call(
        functools.partial(_ffn_kernel, nseq=nseq, final=final),
        out_shape=jax.ShapeDtypeStruct((m, d), F32),
        grid=(m // bm, nf),
        in_specs=[
            pl.BlockSpec((bm, d), lambda i, f: (i, 0)),
            pl.BlockSpec((SUBLANES, d), lambda i, f: (jnp.maximum(i * hb - 1, 0), 0)),
            pl.BlockSpec((1, d), lambda i, f: (0, 0)),
            pl.BlockSpec((d, bf), lambda i, f: (0, f)),
            pl.BlockSpec((d, bf), lambda i, f: (0, nf + f)),
            pl.BlockSpec((CONV_WIDTH, bf), lambda i, f: (0, f)),
            pl.BlockSpec((CONV_WIDTH, bf), lambda i, f: (0, nf + f)),
            pl.BlockSpec((1, bf), lambda i, f: (0, f)),
            pl.BlockSpec((1, bf), lambda i, f: (0, nf + f)),
            pl.BlockSpec((bf, d), lambda i, f: (f, 0)),
            pl.BlockSpec((1, d), lambda i, f: (0, 0)),
        ],
        out_specs=pl.BlockSpec((bm, d), lambda i, f: (i, 0)),
        scratch_shapes=[pltpu.VMEM((bm + SUBLANES, d), BF16), pltpu.VMEM((bm, d), F32)],
        compiler_params=_params(("parallel", "arbitrary")),
        name="ffn",
    )(x2d, x2d, g.reshape(1, d), w_up, w_up, conv_w, conv_w,
      conv_b.reshape(1, -1), conv_b.reshape(1, -1), w_down, final_g.reshape(1, d))


def kernel(x, mem, norm_mix_g, w_in, lambda_q1, lambda_k1, lambda_q2, lambda_k2, da_subln_g,
           w_proj_a, w_proj_b, w_out, norm_x_g, norm_mem_g, w_xq, w_xkv, w_xo, norm_ffn_g,
           w_up, conv_w, conv_b, w_down, final_norm_g):
    batch, seq, d = x.shape
    depth = w_in.shape[0]
    bf = lambda a: a.astype(BF16)
    x2d = x.reshape(batch * seq, d)
    mem2d = mem.reshape(batch * mem.shape[1], d)
    for l in range(depth):
        lam_init = 0.8 - 0.6 * math.exp(-0.3 * l)
        z = _in_proj(x2d, norm_mix_g[l], bf(w_in[l]), seq)
        oa = _diff_attn(z, lambda_q1[l], lambda_k1[l], lambda_q2[l], lambda_k2[l],
                        da_subln_g[l], batch, seq, lam_init)
        ob = _sb_attn(z, batch, seq)
        merged = _merge(oa, ob, z, bf(w_proj_a[l]), bf(w_proj_b[l]))
        x2d = _out_proj(merged, bf(w_out[l]), x2d)
        kv = _mem_kv(mem2d, norm_mem_g[l], bf(w_xkv[l]))
        x2d = _xattn(x2d, norm_x_g[l], bf(w_xq[l]), kv, bf(w_xo[l]), seq)
        x2d = _ffn(x2d, norm_ffn_g[l], bf(w_up[l]), conv_w[l], conv_b[l], bf(w_down[l]),
                   final_norm_g, seq, final=(l == depth - 1))
    return x2d.reshape(batch, seq, d)
```

```python
import functools
import math

import jax
import jax.numpy as jnp
from jax import lax
from jax.experimental import pallas as pl
from jax.experimental.pallas import tpu as pltpu

F32 = jnp.float32
BF16 = jnp.bfloat16

D_MODEL = 2048
N_MEM = 256
DA_HEAD_DIM = 64
DA_WIDTH = D_MODEL // 2
DA_HEADS = DA_WIDTH // (2 * DA_HEAD_DIM)
DA_ROT_DIM = DA_HEAD_DIM // 4
SB_HEAD_DIM = 128
SB_WIDTH = D_MODEL // 2
SB_HEADS = SB_WIDTH // SB_HEAD_DIM
XA_HEADS = 4
XA_HEAD_DIM = 128
XA_WIDTH = XA_HEADS * XA_HEAD_DIM
D_FF = 256 * ((8 * D_MODEL // 3 + 255) // 256)
CONV_WIDTH = 3
ROPE_THETA = 500000.0
EPS = 1e-6
N_IN = 3 * DA_WIDTH + 3 * SB_WIDTH + 2 * D_MODEL

LANES = 128
SUBLANES = 8
BF16_ROWS = 16
NEG_BIG = -1e30
LOG2E = math.log2(math.e)
SB_ZERO_LOG2 = -160.0
VMEM_LIMIT = 48 * 1024 * 1024

COL_QA, COL_KA, COL_VA = 0, DA_WIDTH, 2 * DA_WIDTH
COL_QB, COL_KB, COL_VB = 3 * DA_WIDTH, 3 * DA_WIDTH + SB_WIDTH, 3 * DA_WIDTH + 2 * SB_WIDTH
COL_GA = 3 * DA_WIDTH + 3 * SB_WIDTH
COL_GB = COL_GA + D_MODEL


def _params(sem):
    return pltpu.CompilerParams(dimension_semantics=sem, vmem_limit_bytes=VMEM_LIMIT)


def _rms(x, g):
    return x * lax.rsqrt(jnp.mean(x * x, axis=-1, keepdims=True) + EPS) * g


def _dot(a, b):
    return jnp.dot(a, b, preferred_element_type=F32)


def _dot_nt(a, b):
    return lax.dot_general(a, b, (((1,), (1,)), ((), ())), preferred_element_type=F32)


def _in_proj_kernel(x_ref, g_ref, w_ref, cos_ref, sp_ref, sm_ref, o_ref, h_scr, *, bn):
    j = pl.program_id(1)

    @pl.when(j == 0)
    def _():
        h_scr[...] = _rms(x_ref[...], g_ref[...]).astype(BF16)

    acc = _dot(h_scr[...], w_ref[...])
    col = j * bn
    is_qa = col < COL_KA
    is_qb = jnp.logical_and(col >= COL_QB, col < COL_KB)
    scale = jnp.where(is_qa, LOG2E * DA_HEAD_DIM ** -0.5,
                      jnp.where(is_qb, LOG2E * SB_HEAD_DIM ** -0.5, 1.0))
    acc = acc * scale.astype(F32)

    @pl.when(col < COL_VA)
    def _():
        cos, sp, sm = cos_ref[...], sp_ref[...], sm_ref[...]
        for c in range(bn // LANES):
            a = acc[:, c * LANES:(c + 1) * LANES]
            r = a * cos + pltpu.roll(a, DA_ROT_DIM // 2, 1) * sp \
                + pltpu.roll(a, LANES - DA_ROT_DIM // 2, 1) * sm
            o_ref[:, c * LANES:(c + 1) * LANES] = r.astype(o_ref.dtype)

    @pl.when(col >= COL_VA)
    def _():
        o_ref[...] = acc.astype(o_ref.dtype)


def _rope_lane_tables(seq):
    half = DA_ROT_DIM // 2
    inv = ROPE_THETA ** (-jnp.arange(0, DA_ROT_DIM, 2, dtype=F32) / DA_ROT_DIM)
    ang = jnp.arange(seq, dtype=F32)[:, None] * inv[None, :]
    cos, sin = jnp.cos(ang), jnp.sin(ang)
    lane = jnp.arange(LANES) % DA_HEAD_DIM
    idx = lane % half
    lo = (lane < half)[None, :]
    hi = jnp.logical_and(lane >= half, lane < 2 * half)[None, :]
    cos_t = jnp.where(jnp.logical_or(lo, hi), cos[:, idx], 1.0)
    sp_t = jnp.where(hi, sin[:, idx], 0.0)
    sm_t = jnp.where(lo, -sin[:, idx], 0.0)
    return cos_t.astype(F32), sp_t.astype(F32), sm_t.astype(F32)


def _in_proj(x2d, g, w, seq, *, bm=512, bn=512):
    m, d = x2d.shape
    n = w.shape[1]
    cos_t, sp_t, sm_t = _rope_lane_tables(seq)
    nseq = seq // bm
    tab_spec = pl.BlockSpec((bm, LANES), lambda i, j: (i % nseq, 0))
    return pl.pallas_call(
        functools.partial(_in_proj_kernel, bn=bn),
        out_shape=jax.ShapeDtypeStruct((m, n), BF16),
        grid=(m // bm, n // bn),
        in_specs=[
            pl.BlockSpec((bm, d), lambda i, j: (i, 0)),
            pl.BlockSpec((1, d), lambda i, j: (0, 0)),
            pl.BlockSpec((d, bn), lambda i, j: (0, j)),
            tab_spec, tab_spec, tab_spec,
        ],
        out_specs=pl.BlockSpec((bm, bn), lambda i, j: (i, j)),
        scratch_shapes=[pltpu.VMEM((bm, d), BF16)],
        compiler_params=_params(("parallel", "arbitrary")),
        name="in_proj",
    )(x2d, g.reshape(1, d), w, cos_t, sp_t, sm_t)


def _transpose_values(v_ref, vt_scr, bk):
    hd = v_ref.shape[1]
    ones = jnp.ones((vt_scr.shape[1] - hd, bk), BF16)

    def body(kb, carry):
        start = pl.multiple_of(kb * bk, bk)
        vt_scr[kb, :hd, :] = v_ref[pl.ds(start, bk), :].astype(F32).T.astype(BF16)
        vt_scr[kb, hd:, :] = ones
        return carry
    lax.fori_loop(0, vt_scr.shape[0], body, 0)


def _diff_attn_kernel(q_ref, k_ref, v_ref, lq1_ref, lk1_ref, lq2_ref, lk2_ref, g_ref, o_ref,
                      vt_scr, a1_scr, a2_scr, *, bq, bk, lam_init):
    qi = pl.program_id(2)
    hd = q_ref.shape[1]

    @pl.when(qi == 0)
    def _():
        _transpose_values(v_ref, vt_scr, bk)

    qt = q_ref[...].astype(F32).T
    feat = lax.broadcasted_iota(jnp.int32, qt.shape, 0)
    q1t = jnp.where(feat < DA_HEAD_DIM, qt, 0.0).astype(BF16)
    q2t = jnp.where(feat >= DA_HEAD_DIM, qt, 0.0).astype(BF16)

    a1_scr[...] = jnp.zeros(a1_scr.shape, F32)
    a2_scr[...] = jnp.zeros(a2_scr.shape, F32)

    def update(s, vt, m_prev, a_scr):
        m_new = jnp.maximum(m_prev, jnp.max(s, axis=0, keepdims=True))
        alpha = jnp.exp2(m_prev - m_new)
        p = jnp.exp2(s - m_new).astype(BF16)
        a_scr[...] = alpha * a_scr[...] + _dot(vt, p)
        return m_new

    def step(kb, carry, masked):
        m1, m2 = carry
        start = pl.multiple_of(kb * bk, bk)
        k = k_ref[pl.ds(start, bk), :]
        vt = vt_scr[kb]
        s1 = _dot(k, q1t)
        s2 = _dot(k, q2t)
        if masked:
            key = start + lax.broadcasted_iota(jnp.int32, (bk, bq), 0)
            qry = qi * bq + lax.broadcasted_iota(jnp.int32, (bk, bq), 1)
            keep = key <= qry
            s1 = jnp.where(keep, s1, NEG_BIG)
            s2 = jnp.where(keep, s2, NEG_BIG)
        return update(s1, vt, m1, a1_scr), update(s2, vt, m2, a2_scr)

    neg = jnp.full((1, bq), NEG_BIG, F32)
    n_full = (qi * bq) // bk
    carry = lax.fori_loop(0, n_full, lambda kb, c: step(kb, c, False), (neg, neg))
    step(n_full, carry, True)

    lam = (jnp.exp(jnp.sum(lq1_ref[...] * lk1_ref[...], axis=-1, keepdims=True))
           - jnp.exp(jnp.sum(lq2_ref[...] * lk2_ref[...], axis=-1, keepdims=True)) + lam_init)
    ot = (a1_scr[:hd, :] / a1_scr[hd:hd + 1, :]
          - lam * (a2_scr[:hd, :] / a2_scr[hd:hd + 1, :]))
    o = _rms(ot.T, g_ref[...]) * (1.0 - lam_init)
    o_ref[...] = o.astype(o_ref.dtype)


def _diff_attn(z, lq1, lk1, lq2, lk2, subln_g, batch, seq, lam_init, *, bq=256, bk=1024):
    nq = seq // bq
    hd = 2 * DA_HEAD_DIM
    vec = lambda a: a.reshape(1, -1).astype(F32)
    small = lambda n: pl.BlockSpec((1, n), lambda b, h, i: (0, 0))
    return pl.pallas_call(
        functools.partial(_diff_attn_kernel, bq=bq, bk=bk, lam_init=lam_init),
        out_shape=jax.ShapeDtypeStruct((batch * seq, DA_WIDTH), BF16),
        grid=(batch, DA_HEADS, nq),
        in_specs=[
            pl.BlockSpec((bq, hd), lambda b, h, i: (b * nq + i, COL_QA // hd + h)),
            pl.BlockSpec((seq, hd), lambda b, h, i: (b, COL_KA // hd + h)),
            pl.BlockSpec((seq, hd), lambda b, h, i: (b, COL_VA // hd + h)),
            small(DA_HEAD_DIM), small(DA_HEAD_DIM), small(DA_HEAD_DIM), small(DA_HEAD_DIM),
            small(hd),
        ],
        out_specs=pl.BlockSpec((bq, hd), lambda b, h, i: (b * nq + i, h)),
        scratch_shapes=[
            pltpu.VMEM((seq // bk, hd + BF16_ROWS, bk), BF16),
            pltpu.VMEM((hd + BF16_ROWS, bq), F32), pltpu.VMEM((hd + BF16_ROWS, bq), F32),
        ],
        compiler_params=_params(("arbitrary", "arbitrary", "arbitrary")),
        name="diff_attn",
    )(z, z, z, vec(lq1), vec(lk1), vec(lq2), vec(lk2), vec(subln_g))


def _sb_attn_kernel(q_ref, k_ref, v_ref, u_ref, o_ref, vt_scr, a_scr, *, bq, bk, bc):
    qi = pl.program_id(2)
    hd = q_ref.shape[1]
    nsub = bk // bc

    @pl.when(qi == 0)
    def _():
        _transpose_values(v_ref, vt_scr, bk)

    qt = q_ref[...].astype(F32).T.astype(BF16)
    a_scr[...] = jnp.zeros(a_scr.shape, F32)
    u = u_ref[...]

    def step(kb, c, masked):
        start = pl.multiple_of(kb * bk, bk)
        k = k_ref[pl.ds(start, bk), :]
        z = _dot(k, qt)
        sp = jnp.log2(1.0 + jnp.exp2(-jnp.abs(z)))
        log_beta = jnp.minimum(z, 0.0) - sp
        log_1m = log_beta - z
        if masked:
            key = start + lax.broadcasted_iota(jnp.int32, (bk, bq), 0)
            qry = qi * bq + lax.broadcasted_iota(jnp.int32, (bk, bq), 1)
            strict = key < qry
            log_1m = jnp.where(strict, log_1m, 0.0)
        l16 = log_1m.astype(BF16)
        tails = [None] * nsub
        for sb in reversed(range(nsub)):
            lo = sb * bc
            blk = l16[lo:lo + bc, :]
            t = _dot(u, blk) + c
            tails[sb] = t
            c = t[0:1, :] + blk[0:1, :].astype(F32)
        a = jnp.exp2(log_beta + jnp.concatenate(tails, axis=0))
        if masked:
            a = jnp.where(strict, a, 0.0)
        a_scr[...] += _dot(vt_scr[kb, :hd, :], a.astype(BF16))
        return c

    n_full = (qi * bq) // bk
    c = step(n_full, jnp.zeros((1, bq), F32), True)

    def more(state):
        t, cc = state
        return jnp.logical_and(t < n_full, jnp.max(cc) > SB_ZERO_LOG2)

    def walk(state):
        t, cc = state
        return t + 1, step(n_full - 1 - t, cc, False)

    lax.while_loop(more, walk, (jnp.int32(0), c))
    o_ref[...] = a_scr[...].T.astype(o_ref.dtype)


def _sb_attn(z, batch, seq, *, bq=256, bk=512, bc=256):
    nq = seq // bq
    hd = SB_HEAD_DIM
    r = lax.broadcasted_iota(jnp.int32, (bc, bc), 0)
    c = lax.broadcasted_iota(jnp.int32, (bc, bc), 1)
    u = (c > r).astype(BF16)
    return pl.pallas_call(
        functools.partial(_sb_attn_kernel, bq=bq, bk=bk, bc=bc),
        out_shape=jax.ShapeDtypeStruct((batch * seq, SB_WIDTH), BF16),
        grid=(batch, SB_HEADS, nq),
        in_specs=[
            pl.BlockSpec((bq, hd), lambda b, h, i: (b * nq + i, COL_QB // hd + h)),
            pl.BlockSpec((seq, hd), lambda b, h, i: (b, COL_KB // hd + h)),
            pl.BlockSpec((seq, hd), lambda b, h, i: (b, COL_VB // hd + h)),
            pl.BlockSpec((bc, bc), lambda b, h, i: (0, 0)),
        ],
        out_specs=pl.BlockSpec((bq, hd), lambda b, h, i: (b * nq + i, h)),
        scratch_shapes=[pltpu.VMEM((seq // bk, hd + BF16_ROWS, bk), BF16), pltpu.VMEM((hd, bq), F32)],
        compiler_params=_params(("arbitrary", "arbitrary", "arbitrary")),
        name="sb_attn",
    )(z, z, z, u)


def _sigmoid(x):
    return 1.0 / (1.0 + jnp.exp(-x))


def _merge_kernel(oa_ref, ob_ref, ga_ref, gb_ref, wa_ref, wb_ref, o_ref):
    ya = _dot(oa_ref[...], wa_ref[...])
    yb = _dot(ob_ref[...], wb_ref[...])
    ga = _sigmoid(ga_ref[...].astype(F32))
    gb = _sigmoid(gb_ref[...].astype(F32))
    o_ref[...] = (ga * ya + gb * yb).astype(o_ref.dtype)


def _merge(oa, ob, z, wa, wb, *, bm=512, bn=512):
    m = oa.shape[0]
    return pl.pallas_call(
        _merge_kernel,
        out_shape=jax.ShapeDtypeStruct((m, D_MODEL), BF16),
        grid=(m // bm, D_MODEL // bn),
        in_specs=[
            pl.BlockSpec((bm, DA_WIDTH), lambda i, j: (i, 0)),
            pl.BlockSpec((bm, SB_WIDTH), lambda i, j: (i, 0)),
            pl.BlockSpec((bm, bn), lambda i, j: (i, COL_GA // bn + j)),
            pl.BlockSpec((bm, bn), lambda i, j: (i, COL_GB // bn + j)),
            pl.BlockSpec((DA_WIDTH, bn), lambda i, j: (0, j)),
            pl.BlockSpec((SB_WIDTH, bn), lambda i, j: (0, j)),
        ],
        out_specs=pl.BlockSpec((bm, bn), lambda i, j: (i, j)),
        compiler_params=_params(("parallel", "parallel")),
        name="merge",
    )(oa, ob, z, z, wa, wb)


def _out_proj_kernel(a_ref, w_ref, x_ref, o_ref):
    o_ref[...] = x_ref[...] + _dot(a_ref[...], w_ref[...])


def _out_proj(a, w, x2d, *, bm=512, bn=512):
    m, k = a.shape
    n = w.shape[1]
    return pl.pallas_call(
        _out_proj_kernel,
        out_shape=jax.ShapeDtypeStruct((m, n), F32),
        grid=(m // bm, n // bn),
        in_specs=[
            pl.BlockSpec((bm, k), lambda i, j: (i, 0)),
            pl.BlockSpec((k, bn), lambda i, j: (0, j)),
            pl.BlockSpec((bm, bn), lambda i, j: (i, j)),
        ],
        out_specs=pl.BlockSpec((bm, bn), lambda i, j: (i, j)),
        compiler_params=_params(("parallel", "parallel")),
        name="out_proj",
    )(a, w, x2d)


def _mem_kv_kernel(x_ref, g_ref, w_ref, o_ref):
    h = _rms(x_ref[...], g_ref[...]).astype(BF16)
    o_ref[...] = _dot(h, w_ref[...]).astype(o_ref.dtype)


def _mem_kv(mem2d, g, w, *, bm=256):
    m, d = mem2d.shape
    n = w.shape[1]
    return pl.pallas_call(
        _mem_kv_kernel,
        out_shape=jax.ShapeDtypeStruct((m, n), BF16),
        grid=(m // bm,),
        in_specs=[
            pl.BlockSpec((bm, d), lambda i: (i, 0)),
            pl.BlockSpec((1, d), lambda i: (0, 0)),
            pl.BlockSpec((d, n), lambda i: (0, 0)),
        ],
        out_specs=pl.BlockSpec((bm, n), lambda i: (i, 0)),
        compiler_params=_params(("parallel",)),
        name="mem_kv",
    )(mem2d, g.reshape(1, d), w)


def _xattn_kernel(x_ref, g_ref, wq_ref, kv_ref, wo_ref, o_ref, oh_scr):
    x = x_ref[...]
    h = _rms(x, g_ref[...]).astype(BF16)
    q = (_dot(h, wq_ref[...]) * (XA_HEAD_DIM ** -0.5)).astype(BF16)
    for hh in range(XA_HEADS):
        lo = hh * XA_HEAD_DIM
        k = kv_ref[:, lo:lo + XA_HEAD_DIM]
        v = kv_ref[:, XA_WIDTH + lo:XA_WIDTH + lo + XA_HEAD_DIM]
        s = _dot_nt(q[:, lo:lo + XA_HEAD_DIM], k)
        p = jnp.exp(s - jnp.max(s, axis=-1, keepdims=True))
        p = p / jnp.sum(p, axis=-1, keepdims=True)
        oh_scr[:, lo:lo + XA_HEAD_DIM] = _dot(p.astype(BF16), v).astype(BF16)
    o_ref[...] = x + _dot(oh_scr[...], wo_ref[...])


def _xattn(x2d, g, wq, kv, wo, seq, *, bm=512):
    m, d = x2d.shape
    nseq = seq // bm
    return pl.pallas_call(
        _xattn_kernel,
        out_shape=jax.ShapeDtypeStruct((m, d), F32),
        grid=(m // bm,),
        in_specs=[
            pl.BlockSpec((bm, d), lambda i: (i, 0)),
            pl.BlockSpec((1, d), lambda i: (0, 0)),
            pl.BlockSpec((d, XA_WIDTH), lambda i: (0, 0)),
            pl.BlockSpec((N_MEM, 2 * XA_WIDTH), lambda i: (i // nseq, 0)),
            pl.BlockSpec((XA_WIDTH, d), lambda i: (0, 0)),
        ],
        out_specs=pl.BlockSpec((bm, d), lambda i: (i, 0)),
        scratch_shapes=[pltpu.VMEM((bm, XA_WIDTH), BF16)],
        compiler_params=_params(("parallel",)),
        name="xattn",
    )(x2d, g.reshape(1, d), wq, kv, wo)


def _ffn_kernel(x_ref, halo_ref, g_ref, wg_ref, wv_ref, cwg_ref, cwv_ref, cbg_ref, cbv_ref,
                wd_ref, fg_ref, o_ref, h_scr, acc_scr, *, nseq, final):
    i = pl.program_id(0)
    f = pl.program_id(1)
    pad = SUBLANES

    @pl.when(f == 0)
    def _():
        g = g_ref[...]
        h_scr[pad:, :] = _rms(x_ref[...], g).astype(BF16)
        keep = (i % nseq != 0).astype(F32)
        h_scr[:pad, :] = (_rms(halo_ref[...], g) * keep).astype(BF16)
        acc_scr[...] = jnp.zeros(acc_scr.shape, F32)

    h = h_scr[...]

    def conv(w_ref, cw_ref, cb_ref):
        u = _dot(h, w_ref[...])
        cw = cw_ref[...]
        y = (cw[2:3, :] * u + cw[1:2, :] * pltpu.roll(u, 1, 0) + cw[0:1, :] * pltpu.roll(u, 2, 0))
        return y[pad:, :] + cb_ref[...]

    gate = conv(wg_ref, cwg_ref, cbg_ref)
    val = conv(wv_ref, cwv_ref, cbv_ref)
    act = (gate * _sigmoid(gate) * val).astype(BF16)
    acc_scr[...] += _dot(act, wd_ref[...])

    @pl.when(f == pl.num_programs(1) - 1)
    def _():
        y = x_ref[...] + acc_scr[...]
        o_ref[...] = _rms(y, fg_ref[...]) if final else y


def _ffn(x2d, g, w_up, conv_w, conv_b, w_down, final_g, seq, *, final, bm=512, bf=512):
    m, d = x2d.shape
    nf = D_FF // bf
    nseq = seq // bm
    hb = bm // SUBLANES
    return pl.pallas_call(
        functools.partial(_ffn_kernel, nseq=nseq, final=final),
        out_shape=jax.ShapeDtypeStruct((m, d), F32),
        grid=(m // bm, nf),
        in_specs=[
            pl.BlockSpec((bm, d), lambda i, f: (i, 0)),
            pl.BlockSpec((SUBLANES, d), lambda i, f: (jnp.maximum(i * hb - 1, 0), 0)),
            pl.BlockSpec((1, d), lambda i, f: (0, 0)),
            pl.BlockSpec((d, bf), lambda i, f: (0, f)),
            pl.BlockSpec((d, bf), lambda i, f: (0, nf + f)),
            pl.BlockSpec((CONV_WIDTH, bf), lambda i, f: (0, f)),
            pl.BlockSpec((CONV_WIDTH, bf), lambda i, f: (0, nf + f)),
            pl.BlockSpec((1, bf), lambda i, f: (0, f)),
            pl.BlockSpec((1, bf), lambda i, f: (0, nf + f)),
            pl.BlockSpec((bf, d), lambda i, f: (f, 0)),
            pl.BlockSpec((1, d), lambda i, f: (0, 0)),
        ],
        out_specs=pl.BlockSpec((bm, d), lambda i, f: (i, 0)),
        scratch_shapes=[pltpu.VMEM((bm + SUBLANES, d), BF16), pltpu.VMEM((bm, d), F32)],
        compiler_params=_params(("parallel", "arbitrary")),
        name="ffn",
    )(x2d, x2d, g.reshape(1, d), w_up, w_up, conv_w, conv_w,
      conv_b.reshape(1, -1), conv_b.reshape(1, -1), w_down, final_g.reshape(1, d))


def kernel(x, mem, norm_mix_g, w_in, lambda_q1, lambda_k1, lambda_q2, lambda_k2, da_subln_g,
           w_proj_a, w_proj_b, w_out, norm_x_g, norm_mem_g, w_xq, w_xkv, w_xo, norm_ffn_g,
           w_up, conv_w, conv_b, w_down, final_norm_g):
    batch, seq, d = x.shape
    depth = w_in.shape[0]
    bf = lambda a: a.astype(BF16)
    x2d = x.reshape(batch * seq, d)
    mem2d = mem.reshape(batch * mem.shape[1], d)
    for l in range(depth):
        lam_init = 0.8 - 0.6 * math.exp(-0.3 * l)
        z = _in_proj(x2d, norm_mix_g[l], bf(w_in[l]), seq)
        oa = _diff_attn(z, lambda_q1[l], lambda_k1[l], lambda_q2[l], lambda_k2[l],
                        da_subln_g[l], batch, seq, lam_init)
        ob = _sb_attn(z, batch, seq)
        merged = _merge(oa, ob, z, bf(w_proj_a[l]), bf(w_proj_b[l]))
        x2d = _out_proj(merged, bf(w_out[l]), x2d)
        kv = _mem_kv(mem2d, norm_mem_g[l], bf(w_xkv[l]))
        x2d = _xattn(x2d, norm_x_g[l], bf(w_xq[l]), kv, bf(w_xo[l]), seq)
        x2d = _ffn(x2d, norm_ffn_g[l], bf(w_up[l]), conv_w[l], conv_b[l], bf(w_down[l]),
                   final_norm_g, seq, final=(l == depth - 1))
    return x2d.reshape(batch, seq, d)
```

```python
import functools
import math

import jax
import jax.numpy as jnp
from jax import lax
from jax.experimental import pallas as pl
from jax.experimental.pallas import tpu as pltpu

F32 = jnp.float32
BF16 = jnp.bfloat16

D_MODEL = 2048
N_MEM = 256
DA_HEAD_DIM = 64
DA_WIDTH = D_MODEL // 2
DA_HEADS = DA_WIDTH // (2 * DA_HEAD_DIM)
DA_ROT_DIM = DA_HEAD_DIM // 4
SB_HEAD_DIM = 128
SB_WIDTH = D_MODEL // 2
SB_HEADS = SB_WIDTH // SB_HEAD_DIM
XA_HEADS = 4
XA_HEAD_DIM = 128
XA_WIDTH = XA_HEADS * XA_HEAD_DIM
D_FF = 256 * ((8 * D_MODEL // 3 + 255) // 256)
CONV_WIDTH = 3
ROPE_THETA = 500000.0
EPS = 1e-6
N_IN = 3 * DA_WIDTH + 3 * SB_WIDTH + 2 * D_MODEL

LANES = 128
SUBLANES = 8
BF16_ROWS = 16
NEG_BIG = -1e30
LOG2E = math.log2(math.e)
SB_ZERO_LOG2 = -160.0
VMEM_LIMIT = 48 * 1024 * 1024

COL_QA, COL_KA, COL_VA = 0, DA_WIDTH, 2 * DA_WIDTH
COL_QB, COL_KB, COL_VB = 3 * DA_WIDTH, 3 * DA_WIDTH + SB_WIDTH, 3 * DA_WIDTH + 2 * SB_WIDTH
COL_GA = 3 * DA_WIDTH + 3 * SB_WIDTH
COL_GB = COL_GA + D_MODEL


def _params(sem):
    return pltpu.CompilerParams(dimension_semantics=sem, vmem_limit_bytes=VMEM_LIMIT)


def _rms(x, g):
    return x * lax.rsqrt(jnp.mean(x * x, axis=-1, keepdims=True) + EPS) * g


def _dot(a, b):
    return jnp.dot(a, b, preferred_element_type=F32)


def _dot_nt(a, b):
    return lax.dot_general(a, b, (((1,), (1,)), ((), ())), preferred_element_type=F32)


def _in_proj_kernel(x_ref, g_ref, w_ref, cos_ref, sp_ref, sm_ref, o_ref, h_scr, *, bn):
    j = pl.program_id(1)

    @pl.when(j == 0)
    def _():
        h_scr[...] = _rms(x_ref[...], g_ref[...]).astype(BF16)

    acc = _dot(h_scr[...], w_ref[...])
    col = j * bn
    is_qa = col < COL_KA
    is_qb = jnp.logical_and(col >= COL_QB, col < COL_KB)
    scale = jnp.where(is_qa, LOG2E * DA_HEAD_DIM ** -0.5,
                      jnp.where(is_qb, LOG2E * SB_HEAD_DIM ** -0.5, 1.0))
    acc = acc * scale.astype(F32)

    @pl.when(col < COL_VA)
    def _():
        cos, sp, sm = cos_ref[...], sp_ref[...], sm_ref[...]
        for c in range(bn // LANES):
            a = acc[:, c * LANES:(c + 1) * LANES]
            r = a * cos + pltpu.roll(a, DA_ROT_DIM // 2, 1) * sp \
                + pltpu.roll(a, LANES - DA_ROT_DIM // 2, 1) * sm
            o_ref[:, c * LANES:(c + 1) * LANES] = r.astype(o_ref.dtype)

    @pl.when(col >= COL_VA)
    def _():
        o_ref[...] = acc.astype(o_ref.dtype)


def _rope_lane_tables(seq):
    half = DA_ROT_DIM // 2
    inv = ROPE_THETA ** (-jnp.arange(0, DA_ROT_DIM, 2, dtype=F32) / DA_ROT_DIM)
    ang = jnp.arange(seq, dtype=F32)[:, None] * inv[None, :]
    cos, sin = jnp.cos(ang), jnp.sin(ang)
    lane = jnp.arange(LANES) % DA_HEAD_DIM
    idx = lane % half
    lo = (lane < half)[None, :]
    hi = jnp.logical_and(lane >= half, lane < 2 * half)[None, :]
    cos_t = jnp.where(jnp.logical_or(lo, hi), cos[:, idx], 1.0)
    sp_t = jnp.where(hi, sin[:, idx], 0.0)
    sm_t = jnp.where(lo, -sin[:, idx], 0.0)
    return cos_t.astype(F32), sp_t.astype(F32), sm_t.astype(F32)


def _in_proj(x2d, g, w, seq, *, bm=512, bn=512):
    m, d = x2d.shape
    n = w.shape[1]
    cos_t, sp_t, sm_t = _rope_lane_tables(seq)
    nseq = seq // bm
    tab_spec = pl.BlockSpec((bm, LANES), lambda i, j: (i % nseq, 0))
    return pl.pallas_call(
        functools.partial(_in_proj_kernel, bn=bn),
        out_shape=jax.ShapeDtypeStruct((m, n), BF16),
        grid=(m // bm, n // bn),
        in_specs=[
            pl.BlockSpec((bm, d), lambda i, j: (i, 0)),
            pl.BlockSpec((1, d), lambda i, j: (0, 0)),
            pl.BlockSpec((d, bn), lambda i, j: (0, j)),
            tab_spec, tab_spec, tab_spec,
        ],
        out_specs=pl.BlockSpec((bm, bn), lambda i, j: (i, j)),
        scratch_shapes=[pltpu.VMEM((bm, d), BF16)],
        compiler_params=_params(("parallel", "arbitrary")),
        name="in_proj",
    )(x2d, g.reshape(1, d), w, cos_t, sp_t, sm_t)


def _transpose_values(v_ref, vt_scr, bk):
    hd = v_ref.shape[1]
    ones = jnp.ones((vt_scr.shape[1] - hd, bk), BF16)

    def body(kb, carry):
        start = pl.multiple_of(kb * bk, bk)
        vt_scr[kb, :hd, :] = v_ref[pl.ds(start, bk), :].astype(F32).T.astype(BF16)
        vt_scr[kb, hd:, :] = ones
        return carry
    lax.fori_loop(0, vt_scr.shape[0], body, 0)


def _diff_head_pair_kernel(q_ref, k_ref, v_ref, lq1_ref, lk1_ref, lq2_ref, lk2_ref, g_ref, o_ref,
                           vt_scr, a_scr, *, bq, bk, nh, lam_init):
    qi = pl.program_id(2)
    hd = 2 * DA_HEAD_DIM
    nmap = 2 * nh

    @pl.when(qi == 0)
    def _():
        for h in range(nh):
            _transpose_values(v_ref.at[:, h * hd:(h + 1) * hd], vt_scr.at[h], bk)

    qts = []
    for h in range(nh):
        qt = q_ref[:, h * hd:(h + 1) * hd].astype(F32).T
        feat = lax.broadcasted_iota(jnp.int32, qt.shape, 0)
        qts.append(jnp.where(feat < DA_HEAD_DIM, qt, 0.0).astype(BF16))
        qts.append(jnp.where(feat >= DA_HEAD_DIM, qt, 0.0).astype(BF16))
    a_scr[...] = jnp.zeros(a_scr.shape, F32)

    def step(kb, ms, masked):
        start = pl.multiple_of(kb * bk, bk)
        k = k_ref[pl.ds(start, bk), :]
        ss = [_dot(k[:, (i // 2) * hd:(i // 2 + 1) * hd], qts[i]) for i in range(nmap)]
        if masked:
            key = start + lax.broadcasted_iota(jnp.int32, (bk, bq), 0)
            qry = qi * bq + lax.broadcasted_iota(jnp.int32, (bk, bq), 1)
            keep = key <= qry
            ss = [jnp.where(keep, s, NEG_BIG) for s in ss]
        out = []
        for i in range(nmap):
            m_new = jnp.maximum(ms[i], jnp.max(ss[i], axis=0, keepdims=True))
            alpha = jnp.exp2(ms[i] - m_new)
            p = jnp.exp2(ss[i] - m_new).astype(BF16)
            a_scr[i] = alpha * a_scr[i] + _dot(vt_scr[i // 2, kb], p)
            out.append(m_new)
        return tuple(out)

    neg = jnp.full((1, bq), NEG_BIG, F32)
    n_full = (qi * bq) // bk
    ms = lax.fori_loop(0, n_full, lambda kb, c: step(kb, c, False), (neg,) * nmap)
    step(n_full, ms, True)

    lam = (jnp.exp(jnp.sum(lq1_ref[...] * lk1_ref[...], axis=-1, keepdims=True))
           - jnp.exp(jnp.sum(lq2_ref[...] * lk2_ref[...], axis=-1, keepdims=True)) + lam_init)
    for h in range(nh):
        a1, a2 = a_scr[2 * h], a_scr[2 * h + 1]
        ot = a1[:hd, :] / a1[hd:hd + 1, :] - lam * (a2[:hd, :] / a2[hd:hd + 1, :])
        o = _rms(ot.T, g_ref[...]) * (1.0 - lam_init)
        o_ref[:, h * hd:(h + 1) * hd] = o.astype(o_ref.dtype)


def _diff_attn_pairs(z, lq1, lk1, lq2, lk2, subln_g, batch, seq, lam_init, *, bq=256, bk=1024, nh=2):
    nq = seq // bq
    hd = 2 * DA_HEAD_DIM
    w = nh * hd
    vec = lambda a: a.reshape(1, -1).astype(F32)
    small = lambda n: pl.BlockSpec((1, n), lambda b, h, i: (0, 0))
    return pl.pallas_call(
        functools.partial(_diff_head_pair_kernel, bq=bq, bk=bk, nh=nh, lam_init=lam_init),
        out_shape=jax.ShapeDtypeStruct((batch * seq, DA_WIDTH), BF16),
        grid=(batch, DA_HEADS // nh, nq),
        in_specs=[
            pl.BlockSpec((bq, w), lambda b, h, i: (b * nq + i, COL_QA // w + h)),
            pl.BlockSpec((seq, w), lambda b, h, i: (b, COL_KA // w + h)),
            pl.BlockSpec((seq, w), lambda b, h, i: (b, COL_VA // w + h)),
            small(DA_HEAD_DIM), small(DA_HEAD_DIM), small(DA_HEAD_DIM), small(DA_HEAD_DIM),
            small(hd),
        ],
        out_specs=pl.BlockSpec((bq, w), lambda b, h, i: (b * nq + i, h)),
        scratch_shapes=[
            pltpu.VMEM((nh, seq // bk, hd + BF16_ROWS, bk), BF16),
            pltpu.VMEM((2 * nh, hd + BF16_ROWS, bq), F32),
        ],
        compiler_params=_params(("arbitrary", "arbitrary", "arbitrary")),
        name="diff_attn",
    )(z, z, z, vec(lq1), vec(lk1), vec(lq2), vec(lk2), vec(subln_g))


def _sb_attn_kernel(q_ref, k_ref, v_ref, u_ref, o_ref, vt_scr, a_scr, *, bq, bk, bc):
    qi = pl.program_id(2)
    hd = q_ref.shape[1]
    nsub = bk // bc

    @pl.when(qi == 0)
    def _():
        _transpose_values(v_ref, vt_scr, bk)

    qt = q_ref[...].astype(F32).T.astype(BF16)
    a_scr[...] = jnp.zeros(a_scr.shape, F32)
    u = u_ref[...]

    def step(kb, c, masked):
        start = pl.multiple_of(kb * bk, bk)
        k = k_ref[pl.ds(start, bk), :]
        z = _dot(k, qt)
        sp = jnp.log2(1.0 + jnp.exp2(-jnp.abs(z)))
        log_beta = jnp.minimum(z, 0.0) - sp
        log_1m = log_beta - z
        if masked:
            key = start + lax.broadcasted_iota(jnp.int32, (bk, bq), 0)
            qry = qi * bq + lax.broadcasted_iota(jnp.int32, (bk, bq), 1)
            strict = key < qry
            log_1m = jnp.where(strict, log_1m, 0.0)
        l16 = log_1m.astype(BF16)
        tails = [None] * nsub
        for sb in reversed(range(nsub)):
            lo = sb * bc
            blk = l16[lo:lo + bc, :]
            t = _dot(u, blk) + c
            tails[sb] = t
            c = t[0:1, :] + blk[0:1, :].astype(F32)
        a = jnp.exp2(log_beta + jnp.concatenate(tails, axis=0))
        if masked:
            a = jnp.where(strict, a, 0.0)
        a_scr[...] += _dot(vt_scr[kb, :hd, :], a.astype(BF16))
        return c

    n_full = (qi * bq) // bk
    c = step(n_full, jnp.zeros((1, bq), F32), True)

    def more(state):
        t, cc = state
        return jnp.logical_and(t < n_full, jnp.max(cc) > SB_ZERO_LOG2)

    def walk(state):
        t, cc = state
        return t + 1, step(n_full - 1 - t, cc, False)

    lax.while_loop(more, walk, (jnp.int32(0), c))
    o_ref[...] = a_scr[...].T.astype(o_ref.dtype)


def _sb_attn(z, batch, seq, *, bq=256, bk=512, bc=256):
    nq = seq // bq
    hd = SB_HEAD_DIM
    r = lax.broadcasted_iota(jnp.int32, (bc, bc), 0)
    c = lax.broadcasted_iota(jnp.int32, (bc, bc), 1)
    u = (c > r).astype(BF16)
    return pl.pallas_call(
        functools.partial(_sb_attn_kernel, bq=bq, bk=bk, bc=bc),
        out_shape=jax.ShapeDtypeStruct((batch * seq, SB_WIDTH), BF16),
        grid=(batch, SB_HEADS, nq),
        in_specs=[
            pl.BlockSpec((bq, hd), lambda b, h, i: (b * nq + i, COL_QB // hd + h)),
            pl.BlockSpec((seq, hd), lambda b, h, i: (b, COL_KB // hd + h)),
            pl.BlockSpec((seq, hd), lambda b, h, i: (b, COL_VB // hd + h)),
            pl.BlockSpec((bc, bc), lambda b, h, i: (0, 0)),
        ],
        out_specs=pl.BlockSpec((bq, hd), lambda b, h, i: (b * nq + i, h)),
        scratch_shapes=[pltpu.VMEM((seq // bk, hd + BF16_ROWS, bk), BF16), pltpu.VMEM((hd, bq), F32)],
        compiler_params=_params(("arbitrary", "arbitrary", "arbitrary")),
        name="sb_attn",
    )(z, z, z, u)


def _sigmoid(x):
    return 1.0 / (1.0 + jnp.exp(-x))


def _merge_kernel(oa_ref, ob_ref, ga_ref, gb_ref, wa_ref, wb_ref, o_ref):
    ya = _dot(oa_ref[...], wa_ref[...])
    yb = _dot(ob_ref[...], wb_ref[...])
    ga = _sigmoid(ga_ref[...].astype(F32))
    gb = _sigmoid(gb_ref[...].astype(F32))
    o_ref[...] = (ga * ya + gb * yb).astype(o_ref.dtype)


def _merge(oa, ob, z, wa, wb, *, bm=512, bn=512):
    m = oa.shape[0]
    return pl.pallas_call(
        _merge_kernel,
        out_shape=jax.ShapeDtypeStruct((m, D_MODEL), BF16),
        grid=(m // bm, D_MODEL // bn),
        in_specs=[
            pl.BlockSpec((bm, DA_WIDTH), lambda i, j: (i, 0)),
            pl.BlockSpec((bm, SB_WIDTH), lambda i, j: (i, 0)),
            pl.BlockSpec((bm, bn), lambda i, j: (i, COL_GA // bn + j)),
            pl.BlockSpec((bm, bn), lambda i, j: (i, COL_GB // bn + j)),
            pl.BlockSpec((DA_WIDTH, bn), lambda i, j: (0, j)),
            pl.BlockSpec((SB_WIDTH, bn), lambda i, j: (0, j)),
        ],
        out_specs=pl.BlockSpec((bm, bn), lambda i, j: (i, j)),
        compiler_params=_params(("parallel", "parallel")),
        name="merge",
    )(oa, ob, z, z, wa, wb)


def _out_proj_kernel(a_ref, w_ref, x_ref, o_ref):
    o_ref[...] = x_ref[...] + _dot(a_ref[...], w_ref[...])


def _out_proj(a, w, x2d, *, bm=512, bn=512):
    m, k = a.shape
    n = w.shape[1]
    return pl.pallas_call(
        _out_proj_kernel,
        out_shape=jax.ShapeDtypeStruct((m, n), F32),
        grid=(m // bm, n // bn),
        in_specs=[
            pl.BlockSpec((bm, k), lambda i, j: (i, 0)),
            pl.BlockSpec((k, bn), lambda i, j: (0, j)),
            pl.BlockSpec((bm, bn), lambda i, j: (i, j)),
        ],
        out_specs=pl.BlockSpec((bm, bn), lambda i, j: (i, j)),
        compiler_params=_params(("parallel", "parallel")),
        name="out_proj",
    )(a, w, x2d)


def _mem_kv_kernel(x_ref, g_ref, w_ref, o_ref):
    h = _rms(x_ref[...], g_ref[...]).astype(BF16)
    o_ref[...] = _dot(h, w_ref[...]).astype(o_ref.dtype)


def _mem_kv(mem2d, g, w, *, bm=256):
    m, d = mem2d.shape
    n = w.shape[1]
    return pl.pallas_call(
        _mem_kv_kernel,
        out_shape=jax.ShapeDtypeStruct((m, n), BF16),
        grid=(m // bm,),
        in_specs=[
            pl.BlockSpec((bm, d), lambda i: (i, 0)),
            pl.BlockSpec((1, d), lambda i: (0, 0)),
            pl.BlockSpec((d, n), lambda i: (0, 0)),
        ],
        out_specs=pl.BlockSpec((bm, n), lambda i: (i, 0)),
        compiler_params=_params(("parallel",)),
        name="mem_kv",
    )(mem2d, g.reshape(1, d), w)


def _xattn_kernel(x_ref, g_ref, wq_ref, kv_ref, wo_ref, o_ref, oh_scr):
    x = x_ref[...]
    h = _rms(x, g_ref[...]).astype(BF16)
    q = (_dot(h, wq_ref[...]) * (XA_HEAD_DIM ** -0.5)).astype(BF16)
    for hh in range(XA_HEADS):
        lo = hh * XA_HEAD_DIM
        k = kv_ref[:, lo:lo + XA_HEAD_DIM]
        v = kv_ref[:, XA_WIDTH + lo:XA_WIDTH + lo + XA_HEAD_DIM]
        s = _dot_nt(q[:, lo:lo + XA_HEAD_DIM], k)
        p = jnp.exp(s - jnp.max(s, axis=-1, keepdims=True))
        p = p / jnp.sum(p, axis=-1, keepdims=True)
        oh_scr[:, lo:lo + XA_HEAD_DIM] = _dot(p.astype(BF16), v).astype(BF16)
    o_ref[...] = x + _dot(oh_scr[...], wo_ref[...])


def _xattn(x2d, g, wq, kv, wo, seq, *, bm=512):
    m, d = x2d.shape
    nseq = seq // bm
    return pl.pallas_call(
        _xattn_kernel,
        out_shape=jax.ShapeDtypeStruct((m, d), F32),
        grid=(m // bm,),
        in_specs=[
            pl.BlockSpec((bm, d), lambda i: (i, 0)),
            pl.BlockSpec((1, d), lambda i: (0, 0)),
            pl.BlockSpec((d, XA_WIDTH), lambda i: (0, 0)),
            pl.BlockSpec((N_MEM, 2 * XA_WIDTH), lambda i: (i // nseq, 0)),
            pl.BlockSpec((XA_WIDTH, d), lambda i: (0, 0)),
        ],
        out_specs=pl.BlockSpec((bm, d), lambda i: (i, 0)),
        scratch_shapes=[pltpu.VMEM((bm, XA_WIDTH), BF16)],
        compiler_params=_params(("parallel",)),
        name="xattn",
    )(x2d, g.reshape(1, d), wq, kv, wo)


def _ffn_kernel(x_ref, halo_ref, g_ref, wg_ref, wv_ref, cwg_ref, cwv_ref, cbg_ref, cbv_ref,
                wd_ref, fg_ref, o_ref, h_scr, acc_scr, *, nseq, final):
    i = pl.program_id(0)
    f = pl.program_id(1)
    pad = SUBLANES

    @pl.when(f == 0)
    def _():
        g = g_ref[...]
        h_scr[pad:, :] = _rms(x_ref[...], g).astype(BF16)
        keep = (i % nseq != 0).astype(F32)
        h_scr[:pad, :] = (_rms(halo_ref[...], g) * keep).astype(BF16)
        acc_scr[...] = jnp.zeros(acc_scr.shape, F32)

    h = h_scr[...]

    def conv(w_ref, cw_ref, cb_ref):
        u = _dot(h, w_ref[...])
        cw = cw_ref[...]
        y = (cw[2:3, :] * u + cw[1:2, :] * pltpu.roll(u, 1, 0) + cw[0:1, :] * pltpu.roll(u, 2, 0))
        return y[pad:, :] + cb_ref[...]

    gate = conv(wg_ref, cwg_ref, cbg_ref)
    val = conv(wv_ref, cwv_ref, cbv_ref)
    act = (gate * _sigmoid(gate) * val).astype(BF16)
    acc_scr[...] += _dot(act, wd_ref[...])

    @pl.when(f == pl.num_programs(1) - 1)
    def _():
        y = x_ref[...] + acc_scr[...]
        o_ref[...] = _rms(y, fg_ref[...]) if final else y


def _ffn(x2d, g, w_up, conv_w, conv_b, w_down, final_g, seq, *, final, bm=512, bf=512):
    m, d = x2d.shape
    nf = D_FF // bf
    nseq = seq // bm
    hb = bm // SUBLANES
    return pl.pallas_call(
        functools.partial(_ffn_kernel, nseq=nseq, final=final),
        out_shape=jax.ShapeDtypeStruct((m, d), F32),
        grid=(m // bm, nf),
        in_specs=[
            pl.BlockSpec((bm, d), lambda i, f: (i, 0)),
            pl.BlockSpec((SUBLANES, d), lambda i, f: (jnp.maximum(i * hb - 1, 0), 0)),
            pl.BlockSpec((1, d), lambda i, f: (0, 0)),
            pl.BlockSpec((d, bf), lambda i, f: (0, f)),
            pl.BlockSpec((d, bf), lambda i, f: (0, nf + f)),
            pl.BlockSpec((CONV_WIDTH, bf), lambda i, f: (0, f)),
            pl.BlockSpec((CONV_WIDTH, bf), lambda i, f: (0, nf + f)),
            pl.BlockSpec((1, bf), lambda i, f: (0, f)),
            pl.BlockSpec((1, bf), lambda i, f: (0, nf + f)),
            pl.BlockSpec((bf, d), lambda i, f: (f, 0)),
            pl.BlockSpec((1, d), lambda i, f: (0, 0)),
        ],
        out_specs=pl.BlockSpec((bm, d), lambda i, f: (i, 0)),
        scratch_shapes=[pltpu.VMEM((bm + SUBLANES, d), BF16), pltpu.VMEM((bm, d), F32)],
        compiler_params=_params(("parallel", "arbitrary")),
        name="ffn",
    )(x2d, x2d, g.reshape(1, d), w_up, w_up, conv_w, conv_w,
      conv_b.reshape(1, -1), conv_b.reshape(1, -1), w_down, final_g.reshape(1, d))


def kernel(x, mem, norm_mix_g, w_in, lambda_q1, lambda_k1, lambda_q2, lambda_k2, da_subln_g,
           w_proj_a, w_proj_b, w_out, norm_x_g, norm_mem_g, w_xq, w_xkv, w_xo, norm_ffn_g,
           w_up, conv_w, conv_b, w_down, final_norm_g):
    batch, seq, d = x.shape
    depth = w_in.shape[0]
    bf = lambda a: a.astype(BF16)
    x2d = x.reshape(batch * seq, d)
    mem2d = mem.reshape(batch * mem.shape[1], d)
    for l in range(depth):
        lam_init = 0.8 - 0.6 * math.exp(-0.3 * l)
        z = _in_proj(x2d, norm_mix_g[l], bf(w_in[l]), seq)
        oa = _diff_attn_pairs(z, lambda_q1[l], lambda_k1[l], lambda_q2[l], lambda_k2[l],
                        da_subln_g[l], batch, seq, lam_init)
        ob = _sb_attn(z, batch, seq)
        merged = _merge(oa, ob, z, bf(w_proj_a[l]), bf(w_proj_b[l]))
        x2d = _out_proj(merged, bf(w_out[l]), x2d)
        kv = _mem_kv(mem2d, norm_mem_g[l], bf(w_xkv[l]))
        x2d = _xattn(x2d, norm_x_g[l], bf(w_xq[l]), kv, bf(w_xo[l]), seq)
        x2d = _ffn(x2d, norm_ffn_g[l], bf(w_up[l]), conv_w[l], conv_b[l], bf(w_down[l]),
                   final_norm_g, seq, final=(l == depth - 1))
    return x2d.reshape(batch, seq, d)
```

```python
import functools
import math

import jax
import jax.numpy as jnp
from jax import lax
from jax.experimental import pallas as pl
from jax.experimental.pallas import tpu as pltpu

F32 = jnp.float32
BF16 = jnp.bfloat16

D_MODEL = 2048
N_MEM = 256
DA_HEAD_DIM = 64
DA_WIDTH = D_MODEL // 2
DA_HEADS = DA_WIDTH // (2 * DA_HEAD_DIM)
DA_ROT_DIM = DA_HEAD_DIM // 4
SB_HEAD_DIM = 128
SB_WIDTH = D_MODEL // 2
SB_HEADS = SB_WIDTH // SB_HEAD_DIM
XA_HEADS = 4
XA_HEAD_DIM = 128
XA_WIDTH = XA_HEADS * XA_HEAD_DIM
D_FF = 256 * ((8 * D_MODEL // 3 + 255) // 256)
CONV_WIDTH = 3
ROPE_THETA = 500000.0
EPS = 1e-6
N_IN = 3 * DA_WIDTH + 3 * SB_WIDTH + 2 * D_MODEL

LANES = 128
SUBLANES = 8
BF16_ROWS = 16
NEG_BIG = -1e30
LOG2E = math.log2(math.e)
SB_ZERO_LOG2 = -160.0
VMEM_LIMIT = 48 * 1024 * 1024
VMEM_LIMIT_FFN = 60 * 1024 * 1024

COL_QA, COL_KA, COL_VA = 0, DA_WIDTH, 2 * DA_WIDTH
COL_QB, COL_KB, COL_VB = 3 * DA_WIDTH, 3 * DA_WIDTH + SB_WIDTH, 3 * DA_WIDTH + 2 * SB_WIDTH
COL_GA = 3 * DA_WIDTH + 3 * SB_WIDTH
COL_GB = COL_GA + D_MODEL


def _params(sem, vmem_limit=VMEM_LIMIT):
    return pltpu.CompilerParams(dimension_semantics=sem, vmem_limit_bytes=vmem_limit)


def _rms(x, g):
    return x * lax.rsqrt(jnp.mean(x * x, axis=-1, keepdims=True) + EPS) * g


def _dot(a, b):
    return jnp.dot(a, b, preferred_element_type=F32)


def _dot_nt(a, b):
    return lax.dot_general(a, b, (((1,), (1,)), ((), ())), preferred_element_type=F32)


def _in_proj_kernel(x_ref, g_ref, w_ref, cos_ref, sp_ref, sm_ref, o_ref, h_scr, *, bn):
    j = pl.program_id(1)

    @pl.when(j == 0)
    def _():
        h_scr[...] = _rms(x_ref[...], g_ref[...]).astype(BF16)

    acc = _dot(h_scr[...], w_ref[...])
    col = j * bn
    is_qa = col < COL_KA
    is_qb = jnp.logical_and(col >= COL_QB, col < COL_KB)
    scale = jnp.where(is_qa, LOG2E * DA_HEAD_DIM ** -0.5,
                      jnp.where(is_qb, LOG2E * SB_HEAD_DIM ** -0.5, 1.0))
    acc = acc * scale.astype(F32)

    @pl.when(col < COL_VA)
    def _():
        cos, sp, sm = cos_ref[...], sp_ref[...], sm_ref[...]
        for c in range(bn // LANES):
            a = acc[:, c * LANES:(c + 1) * LANES]
            r = a * cos + pltpu.roll(a, DA_ROT_DIM // 2, 1) * sp \
                + pltpu.roll(a, LANES - DA_ROT_DIM // 2, 1) * sm
            o_ref[:, c * LANES:(c + 1) * LANES] = r.astype(o_ref.dtype)

    @pl.when(col >= COL_VA)
    def _():
        o_ref[...] = acc.astype(o_ref.dtype)


def _rope_lane_tables(seq):
    half = DA_ROT_DIM // 2
    inv = ROPE_THETA ** (-jnp.arange(0, DA_ROT_DIM, 2, dtype=F32) / DA_ROT_DIM)
    ang = jnp.arange(seq, dtype=F32)[:, None] * inv[None, :]
    cos, sin = jnp.cos(ang), jnp.sin(ang)
    lane = jnp.arange(LANES) % DA_HEAD_DIM
    idx = lane % half
    lo = (lane < half)[None, :]
    hi = jnp.logical_and(lane >= half, lane < 2 * half)[None, :]
    cos_t = jnp.where(jnp.logical_or(lo, hi), cos[:, idx], 1.0)
    sp_t = jnp.where(hi, sin[:, idx], 0.0)
    sm_t = jnp.where(lo, -sin[:, idx], 0.0)
    return cos_t.astype(F32), sp_t.astype(F32), sm_t.astype(F32)


def _in_proj(x2d, g, w, seq, *, bm=1024, bn=1024):
    m, d = x2d.shape
    n = w.shape[1]
    cos_t, sp_t, sm_t = _rope_lane_tables(seq)
    nseq = seq // bm
    tab_spec = pl.BlockSpec((bm, LANES), lambda i, j: (i % nseq, 0))
    return pl.pallas_call(
        functools.partial(_in_proj_kernel, bn=bn),
        out_shape=jax.ShapeDtypeStruct((m, n), BF16),
        grid=(m // bm, n // bn),
        in_specs=[
            pl.BlockSpec((bm, d), lambda i, j: (i, 0)),
            pl.BlockSpec((1, d), lambda i, j: (0, 0)),
            pl.BlockSpec((d, bn), lambda i, j: (0, j)),
            tab_spec, tab_spec, tab_spec,
        ],
        out_specs=pl.BlockSpec((bm, bn), lambda i, j: (i, j)),
        scratch_shapes=[pltpu.VMEM((bm, d), BF16)],
        compiler_params=_params(("parallel", "arbitrary")),
        name="in_proj",
    )(x2d, g.reshape(1, d), w, cos_t, sp_t, sm_t)


def _transpose_values(v_ref, vt_scr, bk):
    hd = v_ref.shape[1]
    ones = jnp.ones((vt_scr.shape[1] - hd, bk), BF16)

    def body(kb, carry):
        start = pl.multiple_of(kb * bk, bk)
        vt_scr[kb, :hd, :] = v_ref[pl.ds(start, bk), :].astype(F32).T.astype(BF16)
        vt_scr[kb, hd:, :] = ones
        return carry
    lax.fori_loop(0, vt_scr.shape[0], body, 0)


def _diff_head_pair_kernel(q_ref, k_ref, v_ref, lq1_ref, lk1_ref, lq2_ref, lk2_ref, g_ref, o_ref,
                           vt_scr, a_scr, *, bq, bk, nh, lam_init):
    qi = pl.program_id(2)
    hd = 2 * DA_HEAD_DIM
    nmap = 2 * nh

    @pl.when(qi == 0)
    def _():
        for h in range(nh):
            _transpose_values(v_ref.at[:, h * hd:(h + 1) * hd], vt_scr.at[h], bk)

    qts = []
    for h in range(nh):
        qt = q_ref[:, h * hd:(h + 1) * hd].astype(F32).T
        feat = lax.broadcasted_iota(jnp.int32, qt.shape, 0)
        qts.append(jnp.where(feat < DA_HEAD_DIM, qt, 0.0).astype(BF16))
        qts.append(jnp.where(feat >= DA_HEAD_DIM, qt, 0.0).astype(BF16))
    a_scr[...] = jnp.zeros(a_scr.shape, F32)

    def step(kb, ms, masked):
        start = pl.multiple_of(kb * bk, bk)
        k = k_ref[pl.ds(start, bk), :]
        ss = [_dot(k[:, (i // 2) * hd:(i // 2 + 1) * hd], qts[i]) for i in range(nmap)]
        if masked:
            key = start + lax.broadcasted_iota(jnp.int32, (bk, bq), 0)
            qry = qi * bq + lax.broadcasted_iota(jnp.int32, (bk, bq), 1)
            keep = key <= qry
            ss = [jnp.where(keep, s, NEG_BIG) for s in ss]
        out = []
        for i in range(nmap):
            m_new = jnp.maximum(ms[i], jnp.max(ss[i], axis=0, keepdims=True))
            alpha = jnp.exp2(ms[i] - m_new)
            p = jnp.exp2(ss[i] - m_new).astype(BF16)
            a_scr[i] = alpha * a_scr[i] + _dot(vt_scr[i // 2, kb], p)
            out.append(m_new)
        return tuple(out)

    neg = jnp.full((1, bq), NEG_BIG, F32)
    n_full = (qi * bq) // bk
    ms = lax.fori_loop(0, n_full, lambda kb, c: step(kb, c, False), (neg,) * nmap)
    step(n_full, ms, True)

    lam = (jnp.exp(jnp.sum(lq1_ref[...] * lk1_ref[...], axis=-1, keepdims=True))
           - jnp.exp(jnp.sum(lq2_ref[...] * lk2_ref[...], axis=-1, keepdims=True)) + lam_init)
    for h in range(nh):
        a1, a2 = a_scr[2 * h], a_scr[2 * h + 1]
        ot = a1[:hd, :] / a1[hd:hd + 1, :] - lam * (a2[:hd, :] / a2[hd:hd + 1, :])
        o = _rms(ot.T, g_ref[...]) * (1.0 - lam_init)
        o_ref[:, h * hd:(h + 1) * hd] = o.astype(o_ref.dtype)


def _diff_attn_pairs(z, lq1, lk1, lq2, lk2, subln_g, batch, seq, lam_init, *, bq=256, bk=1024, nh=2):
    nq = seq // bq
    hd = 2 * DA_HEAD_DIM
    w = nh * hd
    vec = lambda a: a.reshape(1, -1).astype(F32)
    small = lambda n: pl.BlockSpec((1, n), lambda b, h, i: (0, 0))
    return pl.pallas_call(
        functools.partial(_diff_head_pair_kernel, bq=bq, bk=bk, nh=nh, lam_init=lam_init),
        out_shape=jax.ShapeDtypeStruct((batch * seq, DA_WIDTH), BF16),
        grid=(batch, DA_HEADS // nh, nq),
        in_specs=[
            pl.BlockSpec((bq, w), lambda b, h, i: (b * nq + i, COL_QA // w + h)),
            pl.BlockSpec((seq, w), lambda b, h, i: (b, COL_KA // w + h)),
            pl.BlockSpec((seq, w), lambda b, h, i: (b, COL_VA // w + h)),
            small(DA_HEAD_DIM), small(DA_HEAD_DIM), small(DA_HEAD_DIM), small(DA_HEAD_DIM),
            small(hd),
        ],
        out_specs=pl.BlockSpec((bq, w), lambda b, h, i: (b * nq + i, h)),
        scratch_shapes=[
            pltpu.VMEM((nh, seq // bk, hd + BF16_ROWS, bk), BF16),
            pltpu.VMEM((2 * nh, hd + BF16_ROWS, bq), F32),
        ],
        compiler_params=_params(("arbitrary", "arbitrary", "arbitrary")),
        name="diff_attn",
    )(z, z, z, vec(lq1), vec(lk1), vec(lq2), vec(lk2), vec(subln_g))


def _sb_attn_kernel(q_ref, k_ref, v_ref, u_ref, o_ref, vt_scr, a_scr, *, bq, bk, bc):
    qi = pl.program_id(2)
    hd = q_ref.shape[1]
    nsub = bk // bc

    @pl.when(qi == 0)
    def _():
        _transpose_values(v_ref, vt_scr, bk)

    qt = q_ref[...].astype(F32).T.astype(BF16)
    a_scr[...] = jnp.zeros(a_scr.shape, F32)
    u = u_ref[...]

    def step(kb, c, masked):
        start = pl.multiple_of(kb * bk, bk)
        k = k_ref[pl.ds(start, bk), :]
        z = _dot(k, qt)
        sp = jnp.log2(1.0 + jnp.exp2(-jnp.abs(z)))
        log_beta = jnp.minimum(z, 0.0) - sp
        log_1m = log_beta - z
        if masked:
            key = start + lax.broadcasted_iota(jnp.int32, (bk, bq), 0)
            qry = qi * bq + lax.broadcasted_iota(jnp.int32, (bk, bq), 1)
            strict = key < qry
            log_1m = jnp.where(strict, log_1m, 0.0)
        l16 = log_1m.astype(BF16)
        tails = [None] * nsub
        for sb in reversed(range(nsub)):
            lo = sb * bc
            blk = l16[lo:lo + bc, :]
            t = _dot(u, blk) + c
            tails[sb] = t
            c = t[0:1, :] + blk[0:1, :].astype(F32)
        a = jnp.exp2(log_beta + jnp.concatenate(tails, axis=0))
        if masked:
            a = jnp.where(strict, a, 0.0)
        a_scr[...] += _dot(vt_scr[kb, :hd, :], a.astype(BF16))
        return c

    n_full = (qi * bq) // bk
    c = step(n_full, jnp.zeros((1, bq), F32), True)

    def more(state):
        t, cc = state
        return jnp.logical_and(t < n_full, jnp.max(cc) > SB_ZERO_LOG2)

    def walk(state):
        t, cc = state
        return t + 1, step(n_full - 1 - t, cc, False)

    lax.while_loop(more, walk, (jnp.int32(0), c))
    o_ref[...] = a_scr[...].T.astype(o_ref.dtype)


def _sb_attn(z, batch, seq, *, bq=256, bk=512, bc=256):
    nq = seq // bq
    hd = SB_HEAD_DIM
    r = lax.broadcasted_iota(jnp.int32, (bc, bc), 0)
    c = lax.broadcasted_iota(jnp.int32, (bc, bc), 1)
    u = (c > r).astype(BF16)
    return pl.pallas_call(
        functools.partial(_sb_attn_kernel, bq=bq, bk=bk, bc=bc),
        out_shape=jax.ShapeDtypeStruct((batch * seq, SB_WIDTH), BF16),
        grid=(batch, SB_HEADS, nq),
        in_specs=[
            pl.BlockSpec((bq, hd), lambda b, h, i: (b * nq + i, COL_QB // hd + h)),
            pl.BlockSpec((seq, hd), lambda b, h, i: (b, COL_KB // hd + h)),
            pl.BlockSpec((seq, hd), lambda b, h, i: (b, COL_VB // hd + h)),
            pl.BlockSpec((bc, bc), lambda b, h, i: (0, 0)),
        ],
        out_specs=pl.BlockSpec((bq, hd), lambda b, h, i: (b * nq + i, h)),
        scratch_shapes=[pltpu.VMEM((seq // bk, hd + BF16_ROWS, bk), BF16), pltpu.VMEM((hd, bq), F32)],
        compiler_params=_params(("arbitrary", "arbitrary", "arbitrary")),
        name="sb_attn",
    )(z, z, z, u)


def _sigmoid(x):
    return 1.0 / (1.0 + jnp.exp(-x))


def _merge_kernel(oa_ref, ob_ref, ga_ref, gb_ref, wa_ref, wb_ref, o_ref):
    ya = _dot(oa_ref[...], wa_ref[...])
    yb = _dot(ob_ref[...], wb_ref[...])
    ga = _sigmoid(ga_ref[...].astype(F32))
    gb = _sigmoid(gb_ref[...].astype(F32))
    o_ref[...] = (ga * ya + gb * yb).astype(o_ref.dtype)


def _merge(oa, ob, z, wa, wb, *, bm=1024, bn=1024):
    m = oa.shape[0]
    return pl.pallas_call(
        _merge_kernel,
        out_shape=jax.ShapeDtypeStruct((m, D_MODEL), BF16),
        grid=(m // bm, D_MODEL // bn),
        in_specs=[
            pl.BlockSpec((bm, DA_WIDTH), lambda i, j: (i, 0)),
            pl.BlockSpec((bm, SB_WIDTH), lambda i, j: (i, 0)),
            pl.BlockSpec((bm, bn), lambda i, j: (i, COL_GA // bn + j)),
            pl.BlockSpec((bm, bn), lambda i, j: (i, COL_GB // bn + j)),
            pl.BlockSpec((DA_WIDTH, bn), lambda i, j: (0, j)),
            pl.BlockSpec((SB_WIDTH, bn), lambda i, j: (0, j)),
        ],
        out_specs=pl.BlockSpec((bm, bn), lambda i, j: (i, j)),
        compiler_params=_params(("parallel", "parallel")),
        name="merge",
    )(oa, ob, z, z, wa, wb)


def _out_proj_kernel(a_ref, w_ref, x_ref, o_ref):
    o_ref[...] = x_ref[...] + _dot(a_ref[...], w_ref[...])


def _out_proj(a, w, x2d, *, bm=1024, bn=1024):
    m, k = a.shape
    n = w.shape[1]
    return pl.pallas_call(
        _out_proj_kernel,
        out_shape=jax.ShapeDtypeStruct((m, n), F32),
        grid=(m // bm, n // bn),
        in_specs=[
            pl.BlockSpec((bm, k), lambda i, j: (i, 0)),
            pl.BlockSpec((k, bn), lambda i, j: (0, j)),
            pl.BlockSpec((bm, bn), lambda i, j: (i, j)),
        ],
        out_specs=pl.BlockSpec((bm, bn), lambda i, j: (i, j)),
        compiler_params=_params(("parallel", "parallel")),
        name="out_proj",
    )(a, w, x2d)


def _mem_kv_kernel(x_ref, g_ref, w_ref, o_ref):
    h = _rms(x_ref[...], g_ref[...]).astype(BF16)
    o_ref[...] = _dot(h, w_ref[...]).astype(o_ref.dtype)


def _mem_kv(mem2d, g, w, *, bm=256):
    m, d = mem2d.shape
    n = w.shape[1]
    return pl.pallas_call(
        _mem_kv_kernel,
        out_shape=jax.ShapeDtypeStruct((m, n), BF16),
        grid=(m // bm,),
        in_specs=[
            pl.BlockSpec((bm, d), lambda i: (i, 0)),
            pl.BlockSpec((1, d), lambda i: (0, 0)),
            pl.BlockSpec((d, n), lambda i: (0, 0)),
        ],
        out_specs=pl.BlockSpec((bm, n), lambda i: (i, 0)),
        compiler_params=_params(("parallel",)),
        name="mem_kv",
    )(mem2d, g.reshape(1, d), w)


def _xattn_kernel(x_ref, g_ref, wq_ref, kv_ref, wo_ref, o_ref, oh_scr):
    x = x_ref[...]
    h = _rms(x, g_ref[...]).astype(BF16)
    q = (_dot(h, wq_ref[...]) * (XA_HEAD_DIM ** -0.5)).astype(BF16)
    for hh in range(XA_HEADS):
        lo = hh * XA_HEAD_DIM
        k = kv_ref[:, lo:lo + XA_HEAD_DIM]
        v = kv_ref[:, XA_WIDTH + lo:XA_WIDTH + lo + XA_HEAD_DIM]
        s = _dot_nt(q[:, lo:lo + XA_HEAD_DIM], k)
        p = jnp.exp(s - jnp.max(s, axis=-1, keepdims=True))
        p = p / jnp.sum(p, axis=-1, keepdims=True)
        oh_scr[:, lo:lo + XA_HEAD_DIM] = _dot(p.astype(BF16), v).astype(BF16)
    o_ref[...] = x + _dot(oh_scr[...], wo_ref[...])


def _xattn(x2d, g, wq, kv, wo, seq, *, bm=512):
    m, d = x2d.shape
    nseq = seq // bm
    return pl.pallas_call(
        _xattn_kernel,
        out_shape=jax.ShapeDtypeStruct((m, d), F32),
        grid=(m // bm,),
        in_specs=[
            pl.BlockSpec((bm, d), lambda i: (i, 0)),
            pl.BlockSpec((1, d), lambda i: (0, 0)),
            pl.BlockSpec((d, XA_WIDTH), lambda i: (0, 0)),
            pl.BlockSpec((N_MEM, 2 * XA_WIDTH), lambda i: (i // nseq, 0)),
            pl.BlockSpec((XA_WIDTH, d), lambda i: (0, 0)),
        ],
        out_specs=pl.BlockSpec((bm, d), lambda i: (i, 0)),
        scratch_shapes=[pltpu.VMEM((bm, XA_WIDTH), BF16)],
        compiler_params=_params(("parallel",)),
        name="xattn",
    )(x2d, g.reshape(1, d), wq, kv, wo)


def _ffn_kernel(x_ref, halo_ref, g_ref, wg_ref, wv_ref, cwg_ref, cwv_ref, cbg_ref, cbv_ref,
                wd_ref, fg_ref, o_ref, h_scr, *, nseq, final):
    i = pl.program_id(0)
    f = pl.program_id(1)
    pad = SUBLANES

    @pl.when(f == 0)
    def _():
        g = g_ref[...]
        h_scr[pad:, :] = _rms(x_ref[...], g).astype(BF16)
        keep = (i % nseq != 0).astype(F32)
        h_scr[:pad, :] = (_rms(halo_ref[...], g) * keep).astype(BF16)
        o_ref[...] = jnp.zeros(o_ref.shape, F32)

    h = h_scr[...]

    def conv(w_ref, cw_ref, cb_ref):
        u = _dot(h, w_ref[...])
        cw = cw_ref[...]
        y = (cw[2:3, :] * u + cw[1:2, :] * pltpu.roll(u, 1, 0) + cw[0:1, :] * pltpu.roll(u, 2, 0))
        return y[pad:, :] + cb_ref[...]

    gate = conv(wg_ref, cwg_ref, cbg_ref)
    val = conv(wv_ref, cwv_ref, cbv_ref)
    act = (gate * _sigmoid(gate) * val).astype(BF16)
    o_ref[...] += _dot(act, wd_ref[...])

    @pl.when(f == pl.num_programs(1) - 1)
    def _():
        y = x_ref[...] + o_ref[...]
        o_ref[...] = _rms(y, fg_ref[...]) if final else y


def _ffn(x2d, g, w_up, conv_w, conv_b, w_down, final_g, seq, *, final, bm=1024, bf=512):
    m, d = x2d.shape
    nf = D_FF // bf
    nseq = seq // bm
    hb = bm // SUBLANES
    return pl.pallas_call(
        functools.partial(_ffn_kernel, nseq=nseq, final=final),
        out_shape=jax.ShapeDtypeStruct((m, d), F32),
        grid=(m // bm, nf),
        in_specs=[
            pl.BlockSpec((bm, d), lambda i, f: (i, 0), pipeline_mode=pl.Buffered(1)),
            pl.BlockSpec((SUBLANES, d), lambda i, f: (jnp.maximum(i * hb - 1, 0), 0)),
            pl.BlockSpec((1, d), lambda i, f: (0, 0)),
            pl.BlockSpec((d, bf), lambda i, f: (0, f)),
            pl.BlockSpec((d, bf), lambda i, f: (0, nf + f)),
            pl.BlockSpec((CONV_WIDTH, bf), lambda i, f: (0, f)),
            pl.BlockSpec((CONV_WIDTH, bf), lambda i, f: (0, nf + f)),
            pl.BlockSpec((1, bf), lambda i, f: (0, f)),
            pl.BlockSpec((1, bf), lambda i, f: (0, nf + f)),
            pl.BlockSpec((bf, d), lambda i, f: (f, 0)),
            pl.BlockSpec((1, d), lambda i, f: (0, 0)),
        ],
        out_specs=pl.BlockSpec((bm, d), lambda i, f: (i, 0)),
        scratch_shapes=[pltpu.VMEM((bm + SUBLANES, d), BF16)],
        compiler_params=_params(("parallel", "arbitrary"), VMEM_LIMIT_FFN),
        name="ffn",
    )(x2d, x2d, g.reshape(1, d), w_up, w_up, conv_w, conv_w,
      conv_b.reshape(1, -1), conv_b.reshape(1, -1), w_down, final_g.reshape(1, d))


def kernel(x, mem, norm_mix_g, w_in, lambda_q1, lambda_k1, lambda_q2, lambda_k2, da_subln_g,
           w_proj_a, w_proj_b, w_out, norm_x_g, norm_mem_g, w_xq, w_xkv, w_xo, norm_ffn_g,
           w_up, conv_w, conv_b, w_down, final_norm_g):
    batch, seq, d = x.shape
    depth = w_in.shape[0]
    bf = lambda a: a.astype(BF16)
    x2d = x.reshape(batch * seq, d)
    mem2d = mem.reshape(batch * mem.shape[1], d)
    for l in range(depth):
        lam_init = 0.8 - 0.6 * math.exp(-0.3 * l)
        z = _in_proj(x2d, norm_mix_g[l], bf(w_in[l]), seq)
        oa = _diff_attn_pairs(z, lambda_q1[l], lambda_k1[l], lambda_q2[l], lambda_k2[l],
                        da_subln_g[l], batch, seq, lam_init)
        ob = _sb_attn(z, batch, seq)
        merged = _merge(oa, ob, z, bf(w_proj_a[l]), bf(w_proj_b[l]))
        x2d = _out_proj(merged, bf(w_out[l]), x2d)
        kv = _mem_kv(mem2d, norm_mem_g[l], bf(w_xkv[l]))
        x2d = _xattn(x2d, norm_x_g[l], bf(w_xq[l]), kv, bf(w_xo[l]), seq)
        x2d = _ffn(x2d, norm_ffn_g[l], bf(w_up[l]), conv_w[l], conv_b[l], bf(w_down[l]),
                   final_norm_g, seq, final=(l == depth - 1))
    return x2d.reshape(batch, seq, d)
```

```python
import functools
import math

import jax
import jax.numpy as jnp
from jax import lax
from jax.experimental import pallas as pl
from jax.experimental.pallas import tpu as pltpu

F32 = jnp.float32
BF16 = jnp.bfloat16

D_MODEL = 2048
N_MEM = 256
DA_HEAD_DIM = 64
DA_WIDTH = D_MODEL // 2
DA_HEADS = DA_WIDTH // (2 * DA_HEAD_DIM)
DA_ROT_DIM = DA_HEAD_DIM // 4
SB_HEAD_DIM = 128
SB_WIDTH = D_MODEL // 2
SB_HEADS = SB_WIDTH // SB_HEAD_DIM
XA_HEADS = 4
XA_HEAD_DIM = 128
XA_WIDTH = XA_HEADS * XA_HEAD_DIM
D_FF = 256 * ((8 * D_MODEL // 3 + 255) // 256)
CONV_WIDTH = 3
ROPE_THETA = 500000.0
EPS = 1e-6
N_IN = 3 * DA_WIDTH + 3 * SB_WIDTH + 2 * D_MODEL

LANES = 128
SUBLANES = 8
BF16_ROWS = 16
NEG_BIG = -1e30
LOG2E = math.log2(math.e)
SB_ZERO_LOG2 = -160.0
VMEM_LIMIT = 48 * 1024 * 1024
VMEM_LIMIT_FFN = 60 * 1024 * 1024

COL_QA, COL_KA, COL_VA = 0, DA_WIDTH, 2 * DA_WIDTH
COL_QB, COL_KB, COL_VB = 3 * DA_WIDTH, 3 * DA_WIDTH + SB_WIDTH, 3 * DA_WIDTH + 2 * SB_WIDTH
COL_GA = 3 * DA_WIDTH + 3 * SB_WIDTH
COL_GB = COL_GA + D_MODEL


def _params(sem, vmem_limit=VMEM_LIMIT):
    return pltpu.CompilerParams(dimension_semantics=sem, vmem_limit_bytes=vmem_limit)


def _rms(x, g):
    return x * lax.rsqrt(jnp.mean(x * x, axis=-1, keepdims=True) + EPS) * g


def _dot(a, b):
    return jnp.dot(a, b, preferred_element_type=F32)


def _dot_nt(a, b):
    return lax.dot_general(a, b, (((1,), (1,)), ((), ())), preferred_element_type=F32)


def _in_proj_kernel(x_ref, g_ref, w_ref, cos_ref, sp_ref, sm_ref, o_ref, h_scr, *, bn):
    j = pl.program_id(1)

    @pl.when(j == 0)
    def _():
        h_scr[...] = _rms(x_ref[...], g_ref[...]).astype(BF16)

    acc = _dot(h_scr[...], w_ref[...])
    col = j * bn
    is_qa = col < COL_KA
    is_qb = jnp.logical_and(col >= COL_QB, col < COL_KB)
    scale = jnp.where(is_qa, LOG2E * DA_HEAD_DIM ** -0.5,
                      jnp.where(is_qb, LOG2E * SB_HEAD_DIM ** -0.5, 1.0))
    acc = acc * scale.astype(F32)

    @pl.when(col < COL_VA)
    def _():
        cos, sp, sm = cos_ref[...], sp_ref[...], sm_ref[...]
        for c in range(bn // LANES):
            a = acc[:, c * LANES:(c + 1) * LANES]
            r = a * cos + pltpu.roll(a, DA_ROT_DIM // 2, 1) * sp \
                + pltpu.roll(a, LANES - DA_ROT_DIM // 2, 1) * sm
            o_ref[:, c * LANES:(c + 1) * LANES] = r.astype(o_ref.dtype)

    @pl.when(col >= COL_VA)
    def _():
        o_ref[...] = acc.astype(o_ref.dtype)


def _rope_lane_tables(seq):
    half = DA_ROT_DIM // 2
    inv = ROPE_THETA ** (-jnp.arange(0, DA_ROT_DIM, 2, dtype=F32) / DA_ROT_DIM)
    ang = jnp.arange(seq, dtype=F32)[:, None] * inv[None, :]
    cos, sin = jnp.cos(ang), jnp.sin(ang)
    lane = jnp.arange(LANES) % DA_HEAD_DIM
    idx = lane % half
    lo = (lane < half)[None, :]
    hi = jnp.logical_and(lane >= half, lane < 2 * half)[None, :]
    cos_t = jnp.where(jnp.logical_or(lo, hi), cos[:, idx], 1.0)
    sp_t = jnp.where(hi, sin[:, idx], 0.0)
    sm_t = jnp.where(lo, -sin[:, idx], 0.0)
    return cos_t.astype(F32), sp_t.astype(F32), sm_t.astype(F32)


def _in_proj(x2d, g, w, seq, *, bm=1024, bn=1024):
    m, d = x2d.shape
    n = w.shape[1]
    cos_t, sp_t, sm_t = _rope_lane_tables(seq)
    nseq = seq // bm
    tab_spec = pl.BlockSpec((bm, LANES), lambda i, j: (i % nseq, 0))
    return pl.pallas_call(
        functools.partial(_in_proj_kernel, bn=bn),
        out_shape=jax.ShapeDtypeStruct((m, n), BF16),
        grid=(m // bm, n // bn),
        in_specs=[
            pl.BlockSpec((bm, d), lambda i, j: (i, 0)),
            pl.BlockSpec((1, d), lambda i, j: (0, 0)),
            pl.BlockSpec((d, bn), lambda i, j: (0, j)),
            tab_spec, tab_spec, tab_spec,
        ],
        out_specs=pl.BlockSpec((bm, bn), lambda i, j: (i, j)),
        scratch_shapes=[pltpu.VMEM((bm, d), BF16)],
        compiler_params=_params(("parallel", "arbitrary")),
        name="in_proj",
    )(x2d, g.reshape(1, d), w, cos_t, sp_t, sm_t)


def _transpose_values(v_ref, vt_scr, bk):
    hd = v_ref.shape[1]
    ones = jnp.ones((vt_scr.shape[1] - hd, bk), BF16)

    def body(kb, carry):
        start = pl.multiple_of(kb * bk, bk)
        vt_scr[kb, :hd, :] = v_ref[pl.ds(start, bk), :].astype(F32).T.astype(BF16)
        vt_scr[kb, hd:, :] = ones
        return carry
    lax.fori_loop(0, vt_scr.shape[0], body, 0)


def _diff_head_pair_kernel(q_ref, k_ref, v_ref, lq1_ref, lk1_ref, lq2_ref, lk2_ref, g_ref, o_ref,
                           vt_scr, a_scr, *, bq, bk, nh, lam_init):
    qi = pl.program_id(2)
    hd = 2 * DA_HEAD_DIM
    nmap = 2 * nh

    @pl.when(qi == 0)
    def _():
        for h in range(nh):
            _transpose_values(v_ref.at[:, h * hd:(h + 1) * hd], vt_scr.at[h], bk)

    qts = []
    for h in range(nh):
        qt = q_ref[:, h * hd:(h + 1) * hd].astype(F32).T
        feat = lax.broadcasted_iota(jnp.int32, qt.shape, 0)
        qts.append(jnp.where(feat < DA_HEAD_DIM, qt, 0.0).astype(BF16))
        qts.append(jnp.where(feat >= DA_HEAD_DIM, qt, 0.0).astype(BF16))
    a_scr[...] = jnp.zeros(a_scr.shape, F32)

    def step(kb, ms, masked):
        start = pl.multiple_of(kb * bk, bk)
        k = k_ref[pl.ds(start, bk), :]
        ss = [_dot(k[:, (i // 2) * hd:(i // 2 + 1) * hd], qts[i]) for i in range(nmap)]
        if masked:
            key = start + lax.broadcasted_iota(jnp.int32, (bk, bq), 0)
            qry = qi * bq + lax.broadcasted_iota(jnp.int32, (bk, bq), 1)
            keep = key <= qry
            ss = [jnp.where(keep, s, NEG_BIG) for s in ss]
        out = []
        for i in range(nmap):
            m_new = jnp.maximum(ms[i], jnp.max(ss[i], axis=0, keepdims=True))
            alpha = jnp.exp2(ms[i] - m_new)
            p = jnp.exp2(ss[i] - m_new).astype(BF16)
            a_scr[i] = alpha * a_scr[i] + _dot(vt_scr[i // 2, kb], p)
            out.append(m_new)
        return tuple(out)

    neg = jnp.full((1, bq), NEG_BIG, F32)
    n_full = (qi * bq) // bk
    ms = lax.fori_loop(0, n_full, lambda kb, c: step(kb, c, False), (neg,) * nmap)
    step(n_full, ms, True)

    lam = (jnp.exp(jnp.sum(lq1_ref[...] * lk1_ref[...], axis=-1, keepdims=True))
           - jnp.exp(jnp.sum(lq2_ref[...] * lk2_ref[...], axis=-1, keepdims=True)) + lam_init)
    for h in range(nh):
        a1, a2 = a_scr[2 * h], a_scr[2 * h + 1]
        ot = a1[:hd, :] / a1[hd:hd + 1, :] - lam * (a2[:hd, :] / a2[hd:hd + 1, :])
        o = _rms(ot.T, g_ref[...]) * (1.0 - lam_init)
        o_ref[:, h * hd:(h + 1) * hd] = o.astype(o_ref.dtype)


def _diff_attn_pairs(z, lq1, lk1, lq2, lk2, subln_g, batch, seq, lam_init, *, bq=256, bk=1024, nh=4):
    nq = seq // bq
    hd = 2 * DA_HEAD_DIM
    w = nh * hd
    vec = lambda a: a.reshape(1, -1).astype(F32)
    small = lambda n: pl.BlockSpec((1, n), lambda b, h, i: (0, 0))
    return pl.pallas_call(
        functools.partial(_diff_head_pair_kernel, bq=bq, bk=bk, nh=nh, lam_init=lam_init),
        out_shape=jax.ShapeDtypeStruct((batch * seq, DA_WIDTH), BF16),
        grid=(batch, DA_HEADS // nh, nq),
        in_specs=[
            pl.BlockSpec((bq, w), lambda b, h, i: (b * nq + i, COL_QA // w + h)),
            pl.BlockSpec((seq, w), lambda b, h, i: (b, COL_KA // w + h), pipeline_mode=pl.Buffered(1)),
            pl.BlockSpec((seq, w), lambda b, h, i: (b, COL_VA // w + h), pipeline_mode=pl.Buffered(1)),
            small(DA_HEAD_DIM), small(DA_HEAD_DIM), small(DA_HEAD_DIM), small(DA_HEAD_DIM),
            small(hd),
        ],
        out_specs=pl.BlockSpec((bq, w), lambda b, h, i: (b * nq + i, h)),
        scratch_shapes=[
            pltpu.VMEM((nh, seq // bk, hd + BF16_ROWS, bk), BF16),
            pltpu.VMEM((2 * nh, hd + BF16_ROWS, bq), F32),
        ],
        compiler_params=_params(("arbitrary", "arbitrary", "arbitrary")),
        name="diff_attn",
    )(z, z, z, vec(lq1), vec(lk1), vec(lq2), vec(lk2), vec(subln_g))


def _sb_attn_kernel(q_ref, k_ref, v_ref, u_ref, o_ref, vt_scr, a_scr, *, bq, bk, bc):
    qi = pl.program_id(2)
    hd = q_ref.shape[1]
    nsub = bk // bc

    @pl.when(qi == 0)
    def _():
        _transpose_values(v_ref, vt_scr, bk)

    qt = q_ref[...].astype(F32).T.astype(BF16)
    a_scr[...] = jnp.zeros(a_scr.shape, F32)
    u = u_ref[...]

    def step(kb, c, masked):
        start = pl.multiple_of(kb * bk, bk)
        k = k_ref[pl.ds(start, bk), :]
        z = _dot(k, qt)
        sp = jnp.log2(1.0 + jnp.exp2(-jnp.abs(z)))
        log_beta = jnp.minimum(z, 0.0) - sp
        log_1m = log_beta - z
        if masked:
            key = start + lax.broadcasted_iota(jnp.int32, (bk, bq), 0)
            qry = qi * bq + lax.broadcasted_iota(jnp.int32, (bk, bq), 1)
            strict = key < qry
            log_1m = jnp.where(strict, log_1m, 0.0)
        l16 = log_1m.astype(BF16)
        tails = [None] * nsub
        for sb in reversed(range(nsub)):
            lo = sb * bc
            blk = l16[lo:lo + bc, :]
            t = _dot(u, blk) + c
            tails[sb] = t
            c = t[0:1, :] + blk[0:1, :].astype(F32)
        a = jnp.exp2(log_beta + jnp.concatenate(tails, axis=0))
        if masked:
            a = jnp.where(strict, a, 0.0)
        a_scr[...] += _dot(vt_scr[kb, :hd, :], a.astype(BF16))
        return c

    n_full = (qi * bq) // bk
    c = step(n_full, jnp.zeros((1, bq), F32), True)

    def more(state):
        t, cc = state
        return jnp.logical_and(t < n_full, jnp.max(cc) > SB_ZERO_LOG2)

    def walk(state):
        t, cc = state
        return t + 1, step(n_full - 1 - t, cc, False)

    lax.while_loop(more, walk, (jnp.int32(0), c))
    o_ref[...] = a_scr[...].T.astype(o_ref.dtype)


def _sb_attn(z, batch, seq, *, bq=256, bk=512, bc=256):
    nq = seq // bq
    hd = SB_HEAD_DIM
    r = lax.broadcasted_iota(jnp.int32, (bc, bc), 0)
    c = lax.broadcasted_iota(jnp.int32, (bc, bc), 1)
    u = (c > r).astype(BF16)
    return pl.pallas_call(
        functools.partial(_sb_attn_kernel, bq=bq, bk=bk, bc=bc),
        out_shape=jax.ShapeDtypeStruct((batch * seq, SB_WIDTH), BF16),
        grid=(batch, SB_HEADS, nq),
        in_specs=[
            pl.BlockSpec((bq, hd), lambda b, h, i: (b * nq + i, COL_QB // hd + h)),
            pl.BlockSpec((seq, hd), lambda b, h, i: (b, COL_KB // hd + h)),
            pl.BlockSpec((seq, hd), lambda b, h, i: (b, COL_VB // hd + h)),
            pl.BlockSpec((bc, bc), lambda b, h, i: (0, 0)),
        ],
        out_specs=pl.BlockSpec((bq, hd), lambda b, h, i: (b * nq + i, h)),
        scratch_shapes=[pltpu.VMEM((seq // bk, hd + BF16_ROWS, bk), BF16), pltpu.VMEM((hd, bq), F32)],
        compiler_params=_params(("arbitrary", "arbitrary", "arbitrary")),
        name="sb_attn",
    )(z, z, z, u)


def _sigmoid(x):
    return 1.0 / (1.0 + jnp.exp(-x))


def _merge_kernel(oa_ref, ob_ref, ga_ref, gb_ref, wa_ref, wb_ref, o_ref):
    ya = _dot(oa_ref[...], wa_ref[...])
    yb = _dot(ob_ref[...], wb_ref[...])
    ga = _sigmoid(ga_ref[...].astype(F32))
    gb = _sigmoid(gb_ref[...].astype(F32))
    o_ref[...] = (ga * ya + gb * yb).astype(o_ref.dtype)


def _merge(oa, ob, z, wa, wb, *, bm=1024, bn=1024):
    m = oa.shape[0]
    return pl.pallas_call(
        _merge_kernel,
        out_shape=jax.ShapeDtypeStruct((m, D_MODEL), BF16),
        grid=(m // bm, D_MODEL // bn),
        in_specs=[
            pl.BlockSpec((bm, DA_WIDTH), lambda i, j: (i, 0)),
            pl.BlockSpec((bm, SB_WIDTH), lambda i, j: (i, 0)),
            pl.BlockSpec((bm, bn), lambda i, j: (i, COL_GA // bn + j)),
            pl.BlockSpec((bm, bn), lambda i, j: (i, COL_GB // bn + j)),
            pl.BlockSpec((DA_WIDTH, bn), lambda i, j: (0, j)),
            pl.BlockSpec((SB_WIDTH, bn), lambda i, j: (0, j)),
        ],
        out_specs=pl.BlockSpec((bm, bn), lambda i, j: (i, j)),
        compiler_params=_params(("parallel", "parallel")),
        name="merge",
    )(oa, ob, z, z, wa, wb)


def _out_proj_kernel(a_ref, w_ref, x_ref, o_ref):
    o_ref[...] = x_ref[...] + _dot(a_ref[...], w_ref[...])


def _out_proj(a, w, x2d, *, bm=1024, bn=1024):
    m, k = a.shape
    n = w.shape[1]
    return pl.pallas_call(
        _out_proj_kernel,
        out_shape=jax.ShapeDtypeStruct((m, n), F32),
        grid=(m // bm, n // bn),
        in_specs=[
            pl.BlockSpec((bm, k), lambda i, j: (i, 0)),
            pl.BlockSpec((k, bn), lambda i, j: (0, j)),
            pl.BlockSpec((bm, bn), lambda i, j: (i, j)),
        ],
        out_specs=pl.BlockSpec((bm, bn), lambda i, j: (i, j)),
        compiler_params=_params(("parallel", "parallel")),
        name="out_proj",
    )(a, w, x2d)


def _mem_kv_kernel(x_ref, g_ref, w_ref, o_ref):
    h = _rms(x_ref[...], g_ref[...]).astype(BF16)
    o_ref[...] = _dot(h, w_ref[...]).astype(o_ref.dtype)


def _mem_kv(mem2d, g, w, *, bm=256):
    m, d = mem2d.shape
    n = w.shape[1]
    return pl.pallas_call(
        _mem_kv_kernel,
        out_shape=jax.ShapeDtypeStruct((m, n), BF16),
        grid=(m // bm,),
        in_specs=[
            pl.BlockSpec((bm, d), lambda i: (i, 0)),
            pl.BlockSpec((1, d), lambda i: (0, 0)),
            pl.BlockSpec((d, n), lambda i: (0, 0)),
        ],
        out_specs=pl.BlockSpec((bm, n), lambda i: (i, 0)),
        compiler_params=_params(("parallel",)),
        name="mem_kv",
    )(mem2d, g.reshape(1, d), w)


def _xattn_kernel(x_ref, g_ref, wq_ref, kv_ref, wo_ref, o_ref, oh_scr):
    x = x_ref[...]
    h = _rms(x, g_ref[...]).astype(BF16)
    q = (_dot(h, wq_ref[...]) * (XA_HEAD_DIM ** -0.5)).astype(BF16)
    for hh in range(XA_HEADS):
        lo = hh * XA_HEAD_DIM
        k = kv_ref[:, lo:lo + XA_HEAD_DIM]
        v = kv_ref[:, XA_WIDTH + lo:XA_WIDTH + lo + XA_HEAD_DIM]
        s = _dot_nt(q[:, lo:lo + XA_HEAD_DIM], k)
        p = jnp.exp(s - jnp.max(s, axis=-1, keepdims=True))
        p = p / jnp.sum(p, axis=-1, keepdims=True)
        oh_scr[:, lo:lo + XA_HEAD_DIM] = _dot(p.astype(BF16), v).astype(BF16)
    o_ref[...] = x + _dot(oh_scr[...], wo_ref[...])


def _xattn(x2d, g, wq, kv, wo, seq, *, bm=512):
    m, d = x2d.shape
    nseq = seq // bm
    return pl.pallas_call(
        _xattn_kernel,
        out_shape=jax.ShapeDtypeStruct((m, d), F32),
        grid=(m // bm,),
        in_specs=[
            pl.BlockSpec((bm, d), lambda i: (i, 0)),
            pl.BlockSpec((1, d), lambda i: (0, 0)),
            pl.BlockSpec((d, XA_WIDTH), lambda i: (0, 0)),
            pl.BlockSpec((N_MEM, 2 * XA_WIDTH), lambda i: (i // nseq, 0)),
            pl.BlockSpec((XA_WIDTH, d), lambda i: (0, 0)),
        ],
        out_specs=pl.BlockSpec((bm, d), lambda i: (i, 0)),
        scratch_shapes=[pltpu.VMEM((bm, XA_WIDTH), BF16)],
        compiler_params=_params(("parallel",)),
        name="xattn",
    )(x2d, g.reshape(1, d), wq, kv, wo)


def _ffn_kernel(x_ref, halo_ref, g_ref, wg_ref, wv_ref, cwg_ref, cwv_ref, cbg_ref, cbv_ref,
                wd_ref, fg_ref, o_ref, h_scr, *, nseq, final):
    i = pl.program_id(0)
    f = pl.program_id(1)
    pad = SUBLANES

    @pl.when(f == 0)
    def _():
        g = g_ref[...]
        h_scr[pad:, :] = _rms(x_ref[...], g).astype(BF16)
        keep = (i % nseq != 0).astype(F32)
        h_scr[:pad, :] = (_rms(halo_ref[...], g) * keep).astype(BF16)
        o_ref[...] = jnp.zeros(o_ref.shape, F32)

    h = h_scr[...]

    def conv(w_ref, cw_ref, cb_ref):
        u = _dot(h, w_ref[...])
        cw = cw_ref[...]
        y = (cw[2:3, :] * u + cw[1:2, :] * pltpu.roll(u, 1, 0) + cw[0:1, :] * pltpu.roll(u, 2, 0))
        return y[pad:, :] + cb_ref[...]

    gate = conv(wg_ref, cwg_ref, cbg_ref)
    val = conv(wv_ref, cwv_ref, cbv_ref)
    act = (gate * _sigmoid(gate) * val).astype(BF16)
    o_ref[...] += _dot(act, wd_ref[...])

    @pl.when(f == pl.num_programs(1) - 1)
    def _():
        y = x_ref[...] + o_ref[...]
        o_ref[...] = _rms(y, fg_ref[...]) if final else y


def _ffn(x2d, g, w_up, conv_w, conv_b, w_down, final_g, seq, *, final, bm=1024, bf=512):
    m, d = x2d.shape
    nf = D_FF // bf
    nseq = seq // bm
    hb = bm // SUBLANES
    return pl.pallas_call(
        functools.partial(_ffn_kernel, nseq=nseq, final=final),
        out_shape=jax.ShapeDtypeStruct((m, d), F32),
        grid=(m // bm, nf),
        in_specs=[
            pl.BlockSpec((bm, d), lambda i, f: (i, 0), pipeline_mode=pl.Buffered(1)),
            pl.BlockSpec((SUBLANES, d), lambda i, f: (jnp.maximum(i * hb - 1, 0), 0)),
            pl.BlockSpec((1, d), lambda i, f: (0, 0)),
            pl.BlockSpec((d, bf), lambda i, f: (0, f)),
            pl.BlockSpec((d, bf), lambda i, f: (0, nf + f)),
            pl.BlockSpec((CONV_WIDTH, bf), lambda i, f: (0, f)),
            pl.BlockSpec((CONV_WIDTH, bf), lambda i, f: (0, nf + f)),
            pl.BlockSpec((1, bf), lambda i, f: (0, f)),
            pl.BlockSpec((1, bf), lambda i, f: (0, nf + f)),
            pl.BlockSpec((bf, d), lambda i, f: (f, 0)),
            pl.BlockSpec((1, d), lambda i, f: (0, 0)),
        ],
        out_specs=pl.BlockSpec((bm, d), lambda i, f: (i, 0)),
        scratch_shapes=[pltpu.VMEM((bm + SUBLANES, d), BF16)],
        compiler_params=_params(("parallel", "arbitrary"), VMEM_LIMIT_FFN),
        name="ffn",
    )(x2d, x2d, g.reshape(1, d), w_up, w_up, conv_w, conv_w,
      conv_b.reshape(1, -1), conv_b.reshape(1, -1), w_down, final_g.reshape(1, d))


def kernel(x, mem, norm_mix_g, w_in, lambda_q1, lambda_k1, lambda_q2, lambda_k2, da_subln_g,
           w_proj_a, w_proj_b, w_out, norm_x_g, norm_mem_g, w_xq, w_xkv, w_xo, norm_ffn_g,
           w_up, conv_w, conv_b, w_down, final_norm_g):
    batch, seq, d = x.shape
    depth = w_in.shape[0]
    bf = lambda a: a.astype(BF16)
    x2d = x.reshape(batch * seq, d)
    mem2d = mem.reshape(batch * mem.shape[1], d)
    for l in range(depth):
        lam_init = 0.8 - 0.6 * math.exp(-0.3 * l)
        z = _in_proj(x2d, norm_mix_g[l], bf(w_in[l]), seq)
        oa = _diff_attn_pairs(z, lambda_q1[l], lambda_k1[l], lambda_q2[l], lambda_k2[l],
                        da_subln_g[l], batch, seq, lam_init)
        ob = _sb_attn(z, batch, seq)
        merged = _merge(oa, ob, z, bf(w_proj_a[l]), bf(w_proj_b[l]))
        x2d = _out_proj(merged, bf(w_out[l]), x2d)
        kv = _mem_kv(mem2d, norm_mem_g[l], bf(w_xkv[l]))
        x2d = _xattn(x2d, norm_x_g[l], bf(w_xq[l]), kv, bf(w_xo[l]), seq)
        x2d = _ffn(x2d, norm_ffn_g[l], bf(w_up[l]), conv_w[l], conv_b[l], bf(w_down[l]),
                   final_norm_g, seq, final=(l == depth - 1))
    return x2d.reshape(batch, seq, d)
```

```python
import functools
import math

import jax
import jax.numpy as jnp
from jax import lax
from jax.experimental import pallas as pl
from jax.experimental.pallas import tpu as pltpu

F32 = jnp.float32
BF16 = jnp.bfloat16

D_MODEL = 2048
N_MEM = 256
DA_HEAD_DIM = 64
DA_WIDTH = D_MODEL // 2
DA_HEADS = DA_WIDTH // (2 * DA_HEAD_DIM)
DA_ROT_DIM = DA_HEAD_DIM // 4
SB_HEAD_DIM = 128
SB_WIDTH = D_MODEL // 2
SB_HEADS = SB_WIDTH // SB_HEAD_DIM
XA_HEADS = 4
XA_HEAD_DIM = 128
XA_WIDTH = XA_HEADS * XA_HEAD_DIM
D_FF = 256 * ((8 * D_MODEL // 3 + 255) // 256)
CONV_WIDTH = 3
ROPE_THETA = 500000.0
EPS = 1e-6
N_IN = 3 * DA_WIDTH + 3 * SB_WIDTH + 2 * D_MODEL

LANES = 128
SUBLANES = 8
BF16_ROWS = 16
NEG_BIG = -1e30
LOG2E = math.log2(math.e)
SB_ZERO_LOG2 = -160.0
VMEM_LIMIT = 48 * 1024 * 1024
VMEM_LIMIT_FFN = 60 * 1024 * 1024

COL_QA, COL_KA, COL_VA = 0, DA_WIDTH, 2 * DA_WIDTH
COL_QB, COL_KB, COL_VB = 3 * DA_WIDTH, 3 * DA_WIDTH + SB_WIDTH, 3 * DA_WIDTH + 2 * SB_WIDTH
COL_GA = 3 * DA_WIDTH + 3 * SB_WIDTH
COL_GB = COL_GA + D_MODEL


def _params(sem, vmem_limit=VMEM_LIMIT):
    return pltpu.CompilerParams(dimension_semantics=sem, vmem_limit_bytes=vmem_limit)


def _rms(x, g):
    return x * lax.rsqrt(jnp.mean(x * x, axis=-1, keepdims=True) + EPS) * g


def _dot(a, b):
    return jnp.dot(a, b, preferred_element_type=F32)


def _dot_nt(a, b):
    return lax.dot_general(a, b, (((1,), (1,)), ((), ())), preferred_element_type=F32)


def _in_proj_kernel(x_ref, g_ref, w_ref, cos_ref, sp_ref, sm_ref, o_ref, h_scr, *, bn):
    j = pl.program_id(1)

    @pl.when(j == 0)
    def _():
        h_scr[...] = _rms(x_ref[...], g_ref[...]).astype(BF16)

    acc = _dot(h_scr[...], w_ref[...])
    col = j * bn
    is_qa = col < COL_KA
    is_qb = jnp.logical_and(col >= COL_QB, col < COL_KB)
    scale = jnp.where(is_qa, LOG2E * DA_HEAD_DIM ** -0.5,
                      jnp.where(is_qb, LOG2E * SB_HEAD_DIM ** -0.5, 1.0))
    acc = acc * scale.astype(F32)

    @pl.when(col < COL_VA)
    def _():
        cos, sp, sm = cos_ref[...], sp_ref[...], sm_ref[...]
        for c in range(bn // LANES):
            a = acc[:, c * LANES:(c + 1) * LANES]
            r = a * cos + pltpu.roll(a, DA_ROT_DIM // 2, 1) * sp \
                + pltpu.roll(a, LANES - DA_ROT_DIM // 2, 1) * sm
            o_ref[:, c * LANES:(c + 1) * LANES] = r.astype(o_ref.dtype)

    @pl.when(col >= COL_VA)
    def _():
        o_ref[...] = acc.astype(o_ref.dtype)


def _rope_lane_tables(seq):
    half = DA_ROT_DIM // 2
    inv = ROPE_THETA ** (-jnp.arange(0, DA_ROT_DIM, 2, dtype=F32) / DA_ROT_DIM)
    ang = jnp.arange(seq, dtype=F32)[:, None] * inv[None, :]
    cos, sin = jnp.cos(ang), jnp.sin(ang)
    lane = jnp.arange(LANES) % DA_HEAD_DIM
    idx = lane % half
    lo = (lane < half)[None, :]
    hi = jnp.logical_and(lane >= half, lane < 2 * half)[None, :]
    cos_t = jnp.where(jnp.logical_or(lo, hi), cos[:, idx], 1.0)
    sp_t = jnp.where(hi, sin[:, idx], 0.0)
    sm_t = jnp.where(lo, -sin[:, idx], 0.0)
    return cos_t.astype(F32), sp_t.astype(F32), sm_t.astype(F32)


def _in_proj(x2d, g, w, seq, *, bm=1024, bn=1024):
    m, d = x2d.shape
    n = w.shape[1]
    cos_t, sp_t, sm_t = _rope_lane_tables(seq)
    nseq = seq // bm
    tab_spec = pl.BlockSpec((bm, LANES), lambda i, j: (i % nseq, 0))
    return pl.pallas_call(
        functools.partial(_in_proj_kernel, bn=bn),
        out_shape=jax.ShapeDtypeStruct((m, n), BF16),
        grid=(m // bm, n // bn),
        in_specs=[
            pl.BlockSpec((bm, d), lambda i, j: (i, 0)),
            pl.BlockSpec((1, d), lambda i, j: (0, 0)),
            pl.BlockSpec((d, bn), lambda i, j: (0, j)),
            tab_spec, tab_spec, tab_spec,
        ],
        out_specs=pl.BlockSpec((bm, bn), lambda i, j: (i, j)),
        scratch_shapes=[pltpu.VMEM((bm, d), BF16)],
        compiler_params=_params(("parallel", "arbitrary")),
        name="in_proj",
    )(x2d, g.reshape(1, d), w, cos_t, sp_t, sm_t)


def _transpose_values(v_ref, vt_scr, bk):
    hd = v_ref.shape[1]
    extra = vt_scr.shape[1] - hd

    def body(kb, carry):
        start = pl.multiple_of(kb * bk, bk)
        vt_scr[kb, :hd, :] = v_ref[pl.ds(start, bk), :].astype(F32).T.astype(BF16)
        if extra:
            vt_scr[kb, hd:, :] = jnp.ones((extra, bk), BF16)
        return carry
    lax.fori_loop(0, vt_scr.shape[0], body, 0)


def _diff_head_pair_kernel(q_ref, k_ref, v_ref, lq1_ref, lk1_ref, lq2_ref, lk2_ref, g_ref, o_ref,
                           vt_scr, a_scr, *, bq, bk, nh, lam_init):
    qi = pl.program_id(2)
    hd = 2 * DA_HEAD_DIM
    nmap = 2 * nh

    @pl.when(qi == 0)
    def _():
        for h in range(nh):
            _transpose_values(v_ref.at[:, h * hd:(h + 1) * hd], vt_scr.at[h], bk)

    qts = []
    for h in range(nh):
        qt = q_ref[:, h * hd:(h + 1) * hd].astype(F32).T
        feat = lax.broadcasted_iota(jnp.int32, qt.shape, 0)
        qts.append(jnp.where(feat < DA_HEAD_DIM, qt, 0.0).astype(BF16))
        qts.append(jnp.where(feat >= DA_HEAD_DIM, qt, 0.0).astype(BF16))
    a_scr[...] = jnp.zeros(a_scr.shape, F32)

    def step(kb, ms, masked):
        start = pl.multiple_of(kb * bk, bk)
        k = k_ref[pl.ds(start, bk), :]
        ss = [_dot(k[:, (i // 2) * hd:(i // 2 + 1) * hd], qts[i]) for i in range(nmap)]
        if masked:
            key = start + lax.broadcasted_iota(jnp.int32, (bk, bq), 0)
            qry = qi * bq + lax.broadcasted_iota(jnp.int32, (bk, bq), 1)
            keep = key <= qry
            ss = [jnp.where(keep, s, NEG_BIG) for s in ss]
        out = []
        for i in range(nmap):
            m_new = jnp.maximum(ms[i], jnp.max(ss[i], axis=0, keepdims=True))
            alpha = jnp.exp2(ms[i] - m_new)
            p = jnp.exp2(ss[i] - m_new).astype(BF16)
            a_scr[i] = alpha * a_scr[i] + _dot(vt_scr[i // 2, kb], p)
            out.append(m_new)
        return tuple(out)

    neg = jnp.full((1, bq), NEG_BIG, F32)
    n_full = (qi * bq) // bk
    ms = lax.fori_loop(0, n_full, lambda kb, c: step(kb, c, False), (neg,) * nmap)
    step(n_full, ms, True)

    lam = (jnp.exp(jnp.sum(lq1_ref[...] * lk1_ref[...], axis=-1, keepdims=True))
           - jnp.exp(jnp.sum(lq2_ref[...] * lk2_ref[...], axis=-1, keepdims=True)) + lam_init)
    for h in range(nh):
        a1, a2 = a_scr[2 * h], a_scr[2 * h + 1]
        ot = a1[:hd, :] / a1[hd:hd + 1, :] - lam * (a2[:hd, :] / a2[hd:hd + 1, :])
        o = _rms(ot.T, g_ref[...]) * (1.0 - lam_init)
        o_ref[:, h * hd:(h + 1) * hd] = o.astype(o_ref.dtype)


def _diff_attn_pairs(z, lq1, lk1, lq2, lk2, subln_g, batch, seq, lam_init, *, bq=256, bk=1024, nh=4):
    nq = seq // bq
    hd = 2 * DA_HEAD_DIM
    w = nh * hd
    vec = lambda a: a.reshape(1, -1).astype(F32)
    small = lambda n: pl.BlockSpec((1, n), lambda b, h, i: (0, 0))
    return pl.pallas_call(
        functools.partial(_diff_head_pair_kernel, bq=bq, bk=bk, nh=nh, lam_init=lam_init),
        out_shape=jax.ShapeDtypeStruct((batch * seq, DA_WIDTH), BF16),
        grid=(batch, DA_HEADS // nh, nq),
        in_specs=[
            pl.BlockSpec((bq, w), lambda b, h, i: (b * nq + i, COL_QA // w + h)),
            pl.BlockSpec((seq, w), lambda b, h, i: (b, COL_KA // w + h), pipeline_mode=pl.Buffered(1)),
            pl.BlockSpec((seq, w), lambda b, h, i: (b, COL_VA // w + h), pipeline_mode=pl.Buffered(1)),
            small(DA_HEAD_DIM), small(DA_HEAD_DIM), small(DA_HEAD_DIM), small(DA_HEAD_DIM),
            small(hd),
        ],
        out_specs=pl.BlockSpec((bq, w), lambda b, h, i: (b * nq + i, h)),
        scratch_shapes=[
            pltpu.VMEM((nh, seq // bk, hd + BF16_ROWS, bk), BF16),
            pltpu.VMEM((2 * nh, hd + BF16_ROWS, bq), F32),
        ],
        compiler_params=_params(("arbitrary", "arbitrary", "arbitrary")),
        name="diff_attn",
    )(z, z, z, vec(lq1), vec(lk1), vec(lq2), vec(lk2), vec(subln_g))


def _sb_attn_kernel(q_ref, k_ref, v_ref, u_ref, o_ref, vt_scr, a_scr, *, bq, nh):
    qi = pl.program_id(2)
    hd = SB_HEAD_DIM
    bc = bq

    @pl.when(qi == 0)
    def _():
        for h in range(nh):
            _transpose_values(v_ref.at[:, h * hd:(h + 1) * hd], vt_scr.at[h], bc)

    qts = [q_ref[:, h * hd:(h + 1) * hd].astype(F32).T.astype(BF16) for h in range(nh)]
    a_scr[...] = jnp.zeros(a_scr.shape, F32)
    u = u_ref[...]

    def pair(sb0, cs, bound):
        start = pl.multiple_of(sb0 * bc, bc)
        k = k_ref[pl.ds(start, 2 * bc), :]
        zs = [_dot(k[:, h * hd:(h + 1) * hd], qts[h]) for h in range(nh)]
        if bound is not None:
            valid = start + lax.broadcasted_iota(jnp.int32, (2 * bc, bq), 0) < bound
        out = []
        for h in range(nh):
            z = zs[h]
            sp = jnp.log2(1.0 + jnp.exp2(-jnp.abs(z)))
            log_beta = jnp.minimum(z, 0.0) - sp
            log_1m = log_beta - z
            if bound is not None:
                log_1m = jnp.where(valid, log_1m, 0.0)
            l16 = log_1m.astype(BF16)
            c = cs[h]
            tails = [None, None]
            for sb in (1, 0):
                blk = l16[sb * bc:(sb + 1) * bc, :]
                t = _dot(u, blk) + c
                tails[sb] = t
                c = t[0:1, :] + blk[0:1, :].astype(F32)
            a = jnp.exp2(log_beta + jnp.concatenate(tails, axis=0))
            if bound is not None:
                a = jnp.where(valid, a, 0.0)
            a16 = a.astype(BF16)
            a_scr[h] += _dot(vt_scr[h, sb0], a16[:bc, :]) + _dot(vt_scr[h, sb0 + 1], a16[bc:, :])
            out.append(c)
        return tuple(out)

    def live(cs):
        m = cs[0]
        for c in cs[1:]:
            m = jnp.maximum(m, c)
        return jnp.max(m) > SB_ZERO_LOG2

    first = jnp.maximum(qi - 1, 0)
    qry = qi * bq + lax.broadcasted_iota(jnp.int32, (1, bq), 1)
    cs = pair(first, (jnp.zeros((1, bq), F32),) * nh, qry)

    n_pairs = first // 2

    def more(state):
        t, cc = state
        return jnp.logical_and(t < n_pairs, live(cc))

    def walk(state):
        t, cc = state
        return t + 1, pair(first - 2 - 2 * t, cc, None)

    t_end, cs = lax.while_loop(more, walk, (jnp.int32(0), cs))

    @pl.when(jnp.logical_and(jnp.logical_and(first % 2 == 1, t_end == n_pairs), live(cs)))
    def _():
        pair(0, cs, jnp.full((1, bq), bc, jnp.int32))

    for h in range(nh):
        o_ref[:, h * hd:(h + 1) * hd] = a_scr[h].T.astype(o_ref.dtype)


def _sb_attn(z, batch, seq, *, bq=256, nh=2):
    nq = seq // bq
    hd = SB_HEAD_DIM
    w = nh * hd
    r = lax.broadcasted_iota(jnp.int32, (bq, bq), 0)
    c = lax.broadcasted_iota(jnp.int32, (bq, bq), 1)
    u = (c > r).astype(BF16)
    return pl.pallas_call(
        functools.partial(_sb_attn_kernel, bq=bq, nh=nh),
        out_shape=jax.ShapeDtypeStruct((batch * seq, SB_WIDTH), BF16),
        grid=(batch, SB_HEADS // nh, nq),
        in_specs=[
            pl.BlockSpec((bq, w), lambda b, h, i: (b * nq + i, COL_QB // w + h)),
            pl.BlockSpec((seq, w), lambda b, h, i: (b, COL_KB // w + h)),
            pl.BlockSpec((seq, w), lambda b, h, i: (b, COL_VB // w + h)),
            pl.BlockSpec((bq, bq), lambda b, h, i: (0, 0)),
        ],
        out_specs=pl.BlockSpec((bq, w), lambda b, h, i: (b * nq + i, h)),
        scratch_shapes=[pltpu.VMEM((nh, seq // bq, hd, bq), BF16), pltpu.VMEM((nh, hd, bq), F32)],
        compiler_params=_params(("arbitrary", "arbitrary", "arbitrary")),
        name="sb_attn",
    )(z, z, z, u)


def _sigmoid(x):
    return 1.0 / (1.0 + jnp.exp(-x))


def _merge_kernel(oa_ref, ob_ref, ga_ref, gb_ref, wa_ref, wb_ref, o_ref):
    ya = _dot(oa_ref[...], wa_ref[...])
    yb = _dot(ob_ref[...], wb_ref[...])
    ga = _sigmoid(ga_ref[...].astype(F32))
    gb = _sigmoid(gb_ref[...].astype(F32))
    o_ref[...] = (ga * ya + gb * yb).astype(o_ref.dtype)


def _merge(oa, ob, z, wa, wb, *, bm=1024, bn=1024):
    m = oa.shape[0]
    return pl.pallas_call(
        _merge_kernel,
        out_shape=jax.ShapeDtypeStruct((m, D_MODEL), BF16),
        grid=(m // bm, D_MODEL // bn),
        in_specs=[
            pl.BlockSpec((bm, DA_WIDTH), lambda i, j: (i, 0)),
            pl.BlockSpec((bm, SB_WIDTH), lambda i, j: (i, 0)),
            pl.BlockSpec((bm, bn), lambda i, j: (i, COL_GA // bn + j)),
            pl.BlockSpec((bm, bn), lambda i, j: (i, COL_GB // bn + j)),
            pl.BlockSpec((DA_WIDTH, bn), lambda i, j: (0, j)),
            pl.BlockSpec((SB_WIDTH, bn), lambda i, j: (0, j)),
        ],
        out_specs=pl.BlockSpec((bm, bn), lambda i, j: (i, j)),
        compiler_params=_params(("parallel", "parallel")),
        name="merge",
    )(oa, ob, z, z, wa, wb)


def _out_proj_kernel(a_ref, w_ref, x_ref, o_ref):
    o_ref[...] = x_ref[...] + _dot(a_ref[...], w_ref[...])


def _out_proj(a, w, x2d, *, bm=1024, bn=1024):
    m, k = a.shape
    n = w.shape[1]
    return pl.pallas_call(
        _out_proj_kernel,
        out_shape=jax.ShapeDtypeStruct((m, n), F32),
        grid=(m // bm, n // bn),
        in_specs=[
            pl.BlockSpec((bm, k), lambda i, j: (i, 0)),
            pl.BlockSpec((k, bn), lambda i, j: (0, j)),
            pl.BlockSpec((bm, bn), lambda i, j: (i, j)),
        ],
        out_specs=pl.BlockSpec((bm, bn), lambda i, j: (i, j)),
        compiler_params=_params(("parallel", "parallel")),
        name="out_proj",
    )(a, w, x2d)


def _mem_kv_kernel(x_ref, g_ref, w_ref, o_ref):
    h = _rms(x_ref[...], g_ref[...]).astype(BF16)
    o_ref[...] = _dot(h, w_ref[...]).astype(o_ref.dtype)


def _mem_kv(mem2d, g, w, *, bm=256):
    m, d = mem2d.shape
    n = w.shape[1]
    return pl.pallas_call(
        _mem_kv_kernel,
        out_shape=jax.ShapeDtypeStruct((m, n), BF16),
        grid=(m // bm,),
        in_specs=[
            pl.BlockSpec((bm, d), lambda i: (i, 0)),
            pl.BlockSpec((1, d), lambda i: (0, 0)),
            pl.BlockSpec((d, n), lambda i: (0, 0)),
        ],
        out_specs=pl.BlockSpec((bm, n), lambda i: (i, 0)),
        compiler_params=_params(("parallel",)),
        name="mem_kv",
    )(mem2d, g.reshape(1, d), w)


def _xattn_kernel(x_ref, g_ref, wq_ref, kv_ref, wo_ref, o_ref, oh_scr):
    x = x_ref[...]
    h = _rms(x, g_ref[...]).astype(BF16)
    q = (_dot(h, wq_ref[...]) * (XA_HEAD_DIM ** -0.5)).astype(BF16)
    for hh in range(XA_HEADS):
        lo = hh * XA_HEAD_DIM
        k = kv_ref[:, lo:lo + XA_HEAD_DIM]
        v = kv_ref[:, XA_WIDTH + lo:XA_WIDTH + lo + XA_HEAD_DIM]
        s = _dot_nt(q[:, lo:lo + XA_HEAD_DIM], k)
        p = jnp.exp(s - jnp.max(s, axis=-1, keepdims=True))
        p = p / jnp.sum(p, axis=-1, keepdims=True)
        oh_scr[:, lo:lo + XA_HEAD_DIM] = _dot(p.astype(BF16), v).astype(BF16)
    o_ref[...] = x + _dot(oh_scr[...], wo_ref[...])


def _xattn(x2d, g, wq, kv, wo, seq, *, bm=512):
    m, d = x2d.shape
    nseq = seq // bm
    return pl.pallas_call(
        _xattn_kernel,
        out_shape=jax.ShapeDtypeStruct((m, d), F32),
        grid=(m // bm,),
        in_specs=[
            pl.BlockSpec((bm, d), lambda i: (i, 0)),
            pl.BlockSpec((1, d), lambda i: (0, 0)),
            pl.BlockSpec((d, XA_WIDTH), lambda i: (0, 0)),
            pl.BlockSpec((N_MEM, 2 * XA_WIDTH), lambda i: (i // nseq, 0)),
            pl.BlockSpec((XA_WIDTH, d), lambda i: (0, 0)),
        ],
        out_specs=pl.BlockSpec((bm, d), lambda i: (i, 0)),
        scratch_shapes=[pltpu.VMEM((bm, XA_WIDTH), BF16)],
        compiler_params=_params(("parallel",)),
        name="xattn",
    )(x2d, g.reshape(1, d), wq, kv, wo)


def _ffn_kernel(x_ref, halo_ref, g_ref, wg_ref, wv_ref, cwg_ref, cwv_ref, cbg_ref, cbv_ref,
                wd_ref, fg_ref, o_ref, h_scr, *, nseq, final):
    i = pl.program_id(0)
    f = pl.program_id(1)
    pad = SUBLANES

    @pl.when(f == 0)
    def _():
        g = g_ref[...]
        h_scr[pad:, :] = _rms(x_ref[...], g).astype(BF16)
        keep = (i % nseq != 0).astype(F32)
        h_scr[:pad, :] = (_rms(halo_ref[...], g) * keep).astype(BF16)
        o_ref[...] = jnp.zeros(o_ref.shape, F32)

    h = h_scr[...]

    def conv(w_ref, cw_ref, cb_ref):
        u = _dot(h, w_ref[...])
        cw = cw_ref[...]
        y = (cw[2:3, :] * u + cw[1:2, :] * pltpu.roll(u, 1, 0) + cw[0:1, :] * pltpu.roll(u, 2, 0))
        return y[pad:, :] + cb_ref[...]

    gate = conv(wg_ref, cwg_ref, cbg_ref)
    val = conv(wv_ref, cwv_ref, cbv_ref)
    act = (gate * _sigmoid(gate) * val).astype(BF16)
    o_ref[...] += _dot(act, wd_ref[...])

    @pl.when(f == pl.num_programs(1) - 1)
    def _():
        y = x_ref[...] + o_ref[...]
        o_ref[...] = _rms(y, fg_ref[...]) if final else y


def _ffn(x2d, g, w_up, conv_w, conv_b, w_down, final_g, seq, *, final, bm=1024, bf=512):
    m, d = x2d.shape
    nf = D_FF // bf
    nseq = seq // bm
    hb = bm // SUBLANES
    return pl.pallas_call(
        functools.partial(_ffn_kernel, nseq=nseq, final=final),
        out_shape=jax.ShapeDtypeStruct((m, d), F32),
        grid=(m // bm, nf),
        in_specs=[
            pl.BlockSpec((bm, d), lambda i, f: (i, 0), pipeline_mode=pl.Buffered(1)),
            pl.BlockSpec((SUBLANES, d), lambda i, f: (jnp.maximum(i * hb - 1, 0), 0)),
            pl.BlockSpec((1, d), lambda i, f: (0, 0)),
            pl.BlockSpec((d, bf), lambda i, f: (0, f)),
            pl.BlockSpec((d, bf), lambda i, f: (0, nf + f)),
            pl.BlockSpec((CONV_WIDTH, bf), lambda i, f: (0, f)),
            pl.BlockSpec((CONV_WIDTH, bf), lambda i, f: (0, nf + f)),
            pl.BlockSpec((1, bf), lambda i, f: (0, f)),
            pl.BlockSpec((1, bf), lambda i, f: (0, nf + f)),
            pl.BlockSpec((bf, d), lambda i, f: (f, 0)),
            pl.BlockSpec((1, d), lambda i, f: (0, 0)),
        ],
        out_specs=pl.BlockSpec((bm, d), lambda i, f: (i, 0)),
        scratch_shapes=[pltpu.VMEM((bm + SUBLANES, d), BF16)],
        compiler_params=_params(("parallel", "arbitrary"), VMEM_LIMIT_FFN),
        name="ffn",
    )(x2d, x2d, g.reshape(1, d), w_up, w_up, conv_w, conv_w,
      conv_b.reshape(1, -1), conv_b.reshape(1, -1), w_down, final_g.reshape(1, d))


def kernel(x, mem, norm_mix_g, w_in, lambda_q1, lambda_k1, lambda_q2, lambda_k2, da_subln_g,
           w_proj_a, w_proj_b, w_out, norm_x_g, norm_mem_g, w_xq, w_xkv, w_xo, norm_ffn_g,
           w_up, conv_w, conv_b, w_down, final_norm_g):
    batch, seq, d = x.shape
    depth = w_in.shape[0]
    bf = lambda a: a.astype(BF16)
    x2d = x.reshape(batch * seq, d)
    mem2d = mem.reshape(batch * mem.shape[1], d)
    for l in range(depth):
        lam_init = 0.8 - 0.6 * math.exp(-0.3 * l)
        z = _in_proj(x2d, norm_mix_g[l], bf(w_in[l]), seq)
        oa = _diff_attn_pairs(z, lambda_q1[l], lambda_k1[l], lambda_q2[l], lambda_k2[l],
                        da_subln_g[l], batch, seq, lam_init)
        ob = _sb_attn(z, batch, seq)
        merged = _merge(oa, ob, z, bf(w_proj_a[l]), bf(w_proj_b[l]))
        x2d = _out_proj(merged, bf(w_out[l]), x2d)
        kv = _mem_kv(mem2d, norm_mem_g[l], bf(w_xkv[l]))
        x2d = _xattn(x2d, norm_x_g[l], bf(w_xq[l]), kv, bf(w_xo[l]), seq)
        x2d = _ffn(x2d, norm_ffn_g[l], bf(w_up[l]), conv_w[l], conv_b[l], bf(w_down[l]),
                   final_norm_g, seq, final=(l == depth - 1))
    return x2d.reshape(batch, seq, d)
```

```python
import functools
import math

import jax
import jax.numpy as jnp
from jax import lax
from jax.experimental import pallas as pl
from jax.experimental.pallas import tpu as pltpu

F32 = jnp.float32
BF16 = jnp.bfloat16

D_MODEL = 2048
N_MEM = 256
DA_HEAD_DIM = 64
DA_WIDTH = D_MODEL // 2
DA_HEADS = DA_WIDTH // (2 * DA_HEAD_DIM)
DA_ROT_DIM = DA_HEAD_DIM // 4
SB_HEAD_DIM = 128
SB_WIDTH = D_MODEL // 2
SB_HEADS = SB_WIDTH // SB_HEAD_DIM
XA_HEADS = 4
XA_HEAD_DIM = 128
XA_WIDTH = XA_HEADS * XA_HEAD_DIM
D_FF = 256 * ((8 * D_MODEL // 3 + 255) // 256)
CONV_WIDTH = 3
ROPE_THETA = 500000.0
EPS = 1e-6
N_IN = 3 * DA_WIDTH + 3 * SB_WIDTH + 2 * D_MODEL

LANES = 128
SUBLANES = 8
BF16_ROWS = 16
NEG_BIG = -1e30
LOG2E = math.log2(math.e)
SB_ZERO_LOG2 = -160.0
VMEM_LIMIT = 48 * 1024 * 1024
VMEM_LIMIT_FFN = 60 * 1024 * 1024

COL_QA, COL_KA, COL_VA = 0, DA_WIDTH, 2 * DA_WIDTH
COL_QB, COL_KB, COL_VB = 3 * DA_WIDTH, 3 * DA_WIDTH + SB_WIDTH, 3 * DA_WIDTH + 2 * SB_WIDTH
COL_GA = 3 * DA_WIDTH + 3 * SB_WIDTH
COL_GB = COL_GA + D_MODEL


def _params(sem, vmem_limit=VMEM_LIMIT):
    return pltpu.CompilerParams(dimension_semantics=sem, vmem_limit_bytes=vmem_limit)


def _rms(x, g):
    return x * lax.rsqrt(jnp.mean(x * x, axis=-1, keepdims=True) + EPS) * g


def _dot(a, b):
    return jnp.dot(a, b, preferred_element_type=F32)


def _dot_nt(a, b):
    return lax.dot_general(a, b, (((1,), (1,)), ((), ())), preferred_element_type=F32)


def _in_proj_kernel(x_ref, g_ref, w_ref, cos_ref, sp_ref, sm_ref, o_ref, h_scr, *, bn):
    j = pl.program_id(1)

    @pl.when(j == 0)
    def _():
        h_scr[...] = _rms(x_ref[...], g_ref[...]).astype(BF16)

    acc = _dot(h_scr[...], w_ref[...])
    col = j * bn
    is_qa = col < COL_KA
    is_qb = jnp.logical_and(col >= COL_QB, col < COL_KB)
    scale = jnp.where(is_qa, LOG2E * DA_HEAD_DIM ** -0.5,
                      jnp.where(is_qb, LOG2E * SB_HEAD_DIM ** -0.5, 1.0))
    acc = acc * scale.astype(F32)

    @pl.when(col < COL_VA)
    def _():
        cos, sp, sm = cos_ref[...], sp_ref[...], sm_ref[...]
        for c in range(bn // LANES):
            a = acc[:, c * LANES:(c + 1) * LANES]
            r = a * cos + pltpu.roll(a, DA_ROT_DIM // 2, 1) * sp \
                + pltpu.roll(a, LANES - DA_ROT_DIM // 2, 1) * sm
            o_ref[:, c * LANES:(c + 1) * LANES] = r.astype(o_ref.dtype)

    @pl.when(col >= COL_VA)
    def _():
        o_ref[...] = acc.astype(o_ref.dtype)


def _rope_lane_tables(seq):
    half = DA_ROT_DIM // 2
    inv = ROPE_THETA ** (-jnp.arange(0, DA_ROT_DIM, 2, dtype=F32) / DA_ROT_DIM)
    ang = jnp.arange(seq, dtype=F32)[:, None] * inv[None, :]
    cos, sin = jnp.cos(ang), jnp.sin(ang)
    lane = jnp.arange(LANES) % DA_HEAD_DIM
    idx = lane % half
    lo = (lane < half)[None, :]
    hi = jnp.logical_and(lane >= half, lane < 2 * half)[None, :]
    cos_t = jnp.where(jnp.logical_or(lo, hi), cos[:, idx], 1.0)
    sp_t = jnp.where(hi, sin[:, idx], 0.0)
    sm_t = jnp.where(lo, -sin[:, idx], 0.0)
    return cos_t.astype(F32), sp_t.astype(F32), sm_t.astype(F32)


def _in_proj(x2d, g, w, seq, *, bm=1024, bn=1024):
    m, d = x2d.shape
    n = w.shape[1]
    cos_t, sp_t, sm_t = _rope_lane_tables(seq)
    nseq = seq // bm
    tab_spec = pl.BlockSpec((bm, LANES), lambda i, j: (i % nseq, 0))
    return pl.pallas_call(
        functools.partial(_in_proj_kernel, bn=bn),
        out_shape=jax.ShapeDtypeStruct((m, n), BF16),
        grid=(m // bm, n // bn),
        in_specs=[
            pl.BlockSpec((bm, d), lambda i, j: (i, 0)),
            pl.BlockSpec((1, d), lambda i, j: (0, 0)),
            pl.BlockSpec((d, bn), lambda i, j: (0, j)),
            tab_spec, tab_spec, tab_spec,
        ],
        out_specs=pl.BlockSpec((bm, bn), lambda i, j: (i, j)),
        scratch_shapes=[pltpu.VMEM((bm, d), BF16)],
        compiler_params=_params(("parallel", "arbitrary")),
        name="in_proj",
    )(x2d, g.reshape(1, d), w, cos_t, sp_t, sm_t)


def _transpose_values(v_ref, vt_scr, bk):
    hd = v_ref.shape[1]
    extra = vt_scr.shape[1] - hd

    def body(kb, carry):
        start = pl.multiple_of(kb * bk, bk)
        vt_scr[kb, :hd, :] = v_ref[pl.ds(start, bk), :].astype(F32).T.astype(BF16)
        if extra:
            vt_scr[kb, hd:, :] = jnp.ones((extra, bk), BF16)
        return carry
    lax.fori_loop(0, vt_scr.shape[0], body, 0)


def _diff_head_pair_kernel(q_ref, k_ref, v_ref, lq1_ref, lk1_ref, lq2_ref, lk2_ref, g_ref, o_ref,
                           vt_scr, a_scr, *, bq, bk, nh, lam_init):
    qi = pl.program_id(2)
    hd = 2 * DA_HEAD_DIM
    nmap = 2 * nh

    @pl.when(qi == 0)
    def _():
        for h in range(nh):
            _transpose_values(v_ref.at[:, h * hd:(h + 1) * hd], vt_scr.at[h], bk)

    qts = []
    for h in range(nh):
        qt = q_ref[:, h * hd:(h + 1) * hd].astype(F32).T
        feat = lax.broadcasted_iota(jnp.int32, qt.shape, 0)
        qts.append(jnp.where(feat < DA_HEAD_DIM, qt, 0.0).astype(BF16))
        qts.append(jnp.where(feat >= DA_HEAD_DIM, qt, 0.0).astype(BF16))
    a_scr[...] = jnp.zeros(a_scr.shape, F32)

    def step(kb, ms, nkeys=None):
        masked = nkeys is not None
        nk = nkeys if masked else bk
        start = pl.multiple_of(kb * bk, bk)
        k = k_ref[pl.ds(start, nk), :]
        ss = [_dot(k[:, (i // 2) * hd:(i // 2 + 1) * hd], qts[i]) for i in range(nmap)]
        if masked:
            key = start + lax.broadcasted_iota(jnp.int32, (nk, bq), 0)
            qry = qi * bq + lax.broadcasted_iota(jnp.int32, (nk, bq), 1)
            keep = key <= qry
            ss = [jnp.where(keep, s, NEG_BIG) for s in ss]
        out = []
        for i in range(nmap):
            m_new = jnp.maximum(ms[i], jnp.max(ss[i], axis=0, keepdims=True))
            alpha = jnp.exp2(ms[i] - m_new)
            p = jnp.exp2(ss[i] - m_new).astype(BF16)
            a_scr[i] = alpha * a_scr[i] + _dot(vt_scr[i // 2, kb, :, :nk], p)
            out.append(m_new)
        return tuple(out)

    neg = jnp.full((1, bq), NEG_BIG, F32)
    n_full = (qi * bq) // bk
    ms = lax.fori_loop(0, n_full, lambda kb, c: step(kb, c), (neg,) * nmap)
    per = bk // bq
    lax.switch(qi % per, [functools.partial(step, n_full, nkeys=(r + 1) * bq) for r in range(per)], ms)

    lam = (jnp.exp(jnp.sum(lq1_ref[...] * lk1_ref[...], axis=-1, keepdims=True))
           - jnp.exp(jnp.sum(lq2_ref[...] * lk2_ref[...], axis=-1, keepdims=True)) + lam_init)
    for h in range(nh):
        a1, a2 = a_scr[2 * h], a_scr[2 * h + 1]
        ot = a1[:hd, :] / a1[hd:hd + 1, :] - lam * (a2[:hd, :] / a2[hd:hd + 1, :])
        o = _rms(ot.T, g_ref[...]) * (1.0 - lam_init)
        o_ref[:, h * hd:(h + 1) * hd] = o.astype(o_ref.dtype)


def _diff_attn_pairs(z, lq1, lk1, lq2, lk2, subln_g, batch, seq, lam_init, *, bq=256, bk=1024, nh=4):
    nq = seq // bq
    hd = 2 * DA_HEAD_DIM
    w = nh * hd
    vec = lambda a: a.reshape(1, -1).astype(F32)
    small = lambda n: pl.BlockSpec((1, n), lambda b, h, i: (0, 0))
    return pl.pallas_call(
        functools.partial(_diff_head_pair_kernel, bq=bq, bk=bk, nh=nh, lam_init=lam_init),
        out_shape=jax.ShapeDtypeStruct((batch * seq, DA_WIDTH), BF16),
        grid=(batch, DA_HEADS // nh, nq),
        in_specs=[
            pl.BlockSpec((bq, w), lambda b, h, i: (b * nq + i, COL_QA // w + h)),
            pl.BlockSpec((seq, w), lambda b, h, i: (b, COL_KA // w + h), pipeline_mode=pl.Buffered(1)),
            pl.BlockSpec((seq, w), lambda b, h, i: (b, COL_VA // w + h), pipeline_mode=pl.Buffered(1)),
            small(DA_HEAD_DIM), small(DA_HEAD_DIM), small(DA_HEAD_DIM), small(DA_HEAD_DIM),
            small(hd),
        ],
        out_specs=pl.BlockSpec((bq, w), lambda b, h, i: (b * nq + i, h)),
        scratch_shapes=[
            pltpu.VMEM((nh, seq // bk, hd + BF16_ROWS, bk), BF16),
            pltpu.VMEM((2 * nh, hd + BF16_ROWS, bq), F32),
        ],
        compiler_params=_params(("arbitrary", "arbitrary", "arbitrary")),
        name="diff_attn",
    )(z, z, z, vec(lq1), vec(lk1), vec(lq2), vec(lk2), vec(subln_g))


def _sb_attn_kernel(q_ref, k_ref, v_ref, u_ref, o_ref, vt_scr, a_scr, *, bq, nh):
    qi = pl.program_id(2)
    hd = SB_HEAD_DIM
    bc = bq

    @pl.when(qi == 0)
    def _():
        for h in range(nh):
            _transpose_values(v_ref.at[:, h * hd:(h + 1) * hd], vt_scr.at[h], bc)

    qts = [q_ref[:, h * hd:(h + 1) * hd].astype(F32).T.astype(BF16) for h in range(nh)]
    a_scr[...] = jnp.zeros(a_scr.shape, F32)
    u = u_ref[...]

    def pair(sb0, cs, bound):
        start = pl.multiple_of(sb0 * bc, bc)
        k = k_ref[pl.ds(start, 2 * bc), :]
        zs = [_dot(k[:, h * hd:(h + 1) * hd], qts[h]) for h in range(nh)]
        if bound is not None:
            valid = start + lax.broadcasted_iota(jnp.int32, (2 * bc, bq), 0) < bound
        out = []
        for h in range(nh):
            z = zs[h]
            sp = jnp.log2(1.0 + jnp.exp2(-jnp.abs(z)))
            log_beta = jnp.minimum(z, 0.0) - sp
            log_1m = log_beta - z
            if bound is not None:
                log_1m = jnp.where(valid, log_1m, 0.0)
            l16 = log_1m.astype(BF16)
            c = cs[h]
            tails = [None, None]
            for sb in (1, 0):
                blk = l16[sb * bc:(sb + 1) * bc, :]
                t = _dot(u, blk) + c
                tails[sb] = t
                c = t[0:1, :] + blk[0:1, :].astype(F32)
            a = jnp.exp2(log_beta + jnp.concatenate(tails, axis=0))
            if bound is not None:
                a = jnp.where(valid, a, 0.0)
            a16 = a.astype(BF16)
            a_scr[h] += _dot(vt_scr[h, sb0], a16[:bc, :]) + _dot(vt_scr[h, sb0 + 1], a16[bc:, :])
            out.append(c)
        return tuple(out)

    def live(cs):
        m = cs[0]
        for c in cs[1:]:
            m = jnp.maximum(m, c)
        return jnp.max(m) > SB_ZERO_LOG2

    first = jnp.maximum(qi - 1, 0)
    qry = qi * bq + lax.broadcasted_iota(jnp.int32, (1, bq), 1)
    cs = pair(first, (jnp.zeros((1, bq), F32),) * nh, qry)

    n_pairs = first // 2

    def more(state):
        t, cc = state
        return jnp.logical_and(t < n_pairs, live(cc))

    def walk(state):
        t, cc = state
        return t + 1, pair(first - 2 - 2 * t, cc, None)

    t_end, cs = lax.while_loop(more, walk, (jnp.int32(0), cs))

    @pl.when(jnp.logical_and(jnp.logical_and(first % 2 == 1, t_end == n_pairs), live(cs)))
    def _():
        pair(0, cs, jnp.full((1, bq), bc, jnp.int32))

    for h in range(nh):
        o_ref[:, h * hd:(h + 1) * hd] = a_scr[h].T.astype(o_ref.dtype)


def _sb_attn(z, batch, seq, *, bq=256, nh=2):
    nq = seq // bq
    hd = SB_HEAD_DIM
    w = nh * hd
    r = lax.broadcasted_iota(jnp.int32, (bq, bq), 0)
    c = lax.broadcasted_iota(jnp.int32, (bq, bq), 1)
    u = (c > r).astype(BF16)
    return pl.pallas_call(
        functools.partial(_sb_attn_kernel, bq=bq, nh=nh),
        out_shape=jax.ShapeDtypeStruct((batch * seq, SB_WIDTH), BF16),
        grid=(batch, SB_HEADS // nh, nq),
        in_specs=[
            pl.BlockSpec((bq, w), lambda b, h, i: (b * nq + i, COL_QB // w + h)),
            pl.BlockSpec((seq, w), lambda b, h, i: (b, COL_KB // w + h)),
            pl.BlockSpec((seq, w), lambda b, h, i: (b, COL_VB // w + h)),
            pl.BlockSpec((bq, bq), lambda b, h, i: (0, 0)),
        ],
        out_specs=pl.BlockSpec((bq, w), lambda b, h, i: (b * nq + i, h)),
        scratch_shapes=[pltpu.VMEM((nh, seq // bq, hd, bq), BF16), pltpu.VMEM((nh, hd, bq), F32)],
        compiler_params=_params(("arbitrary", "arbitrary", "arbitrary")),
        name="sb_attn",
    )(z, z, z, u)


def _sigmoid(x):
    return 1.0 / (1.0 + jnp.exp(-x))


def _merge_kernel(oa_ref, ob_ref, ga_ref, gb_ref, wa_ref, wb_ref, o_ref):
    ya = _dot(oa_ref[...], wa_ref[...])
    yb = _dot(ob_ref[...], wb_ref[...])
    ga = _sigmoid(ga_ref[...].astype(F32))
    gb = _sigmoid(gb_ref[...].astype(F32))
    o_ref[...] = (ga * ya + gb * yb).astype(o_ref.dtype)


def _merge(oa, ob, z, wa, wb, *, bm=1024, bn=1024):
    m = oa.shape[0]
    return pl.pallas_call(
        _merge_kernel,
        out_shape=jax.ShapeDtypeStruct((m, D_MODEL), BF16),
        grid=(m // bm, D_MODEL // bn),
        in_specs=[
            pl.BlockSpec((bm, DA_WIDTH), lambda i, j: (i, 0)),
            pl.BlockSpec((bm, SB_WIDTH), lambda i, j: (i, 0)),
            pl.BlockSpec((bm, bn), lambda i, j: (i, COL_GA // bn + j)),
            pl.BlockSpec((bm, bn), lambda i, j: (i, COL_GB // bn + j)),
            pl.BlockSpec((DA_WIDTH, bn), lambda i, j: (0, j)),
            pl.BlockSpec((SB_WIDTH, bn), lambda i, j: (0, j)),
        ],
        out_specs=pl.BlockSpec((bm, bn), lambda i, j: (i, j)),
        compiler_params=_params(("parallel", "parallel")),
        name="merge",
    )(oa, ob, z, z, wa, wb)


def _out_proj_kernel(a_ref, w_ref, x_ref, o_ref):
    o_ref[...] = x_ref[...] + _dot(a_ref[...], w_ref[...])


def _out_proj(a, w, x2d, *, bm=1024, bn=1024):
    m, k = a.shape
    n = w.shape[1]
    return pl.pallas_call(
        _out_proj_kernel,
        out_shape=jax.ShapeDtypeStruct((m, n), F32),
        grid=(m // bm, n // bn),
        in_specs=[
            pl.BlockSpec((bm, k), lambda i, j: (i, 0)),
            pl.BlockSpec((k, bn), lambda i, j: (0, j)),
            pl.BlockSpec((bm, bn), lambda i, j: (i, j)),
        ],
        out_specs=pl.BlockSpec((bm, bn), lambda i, j: (i, j)),
        compiler_params=_params(("parallel", "parallel")),
        name="out_proj",
    )(a, w, x2d)


def _mem_kv_kernel(x_ref, g_ref, w_ref, o_ref):
    h = _rms(x_ref[...], g_ref[...]).astype(BF16)
    o_ref[...] = _dot(h, w_ref[...]).astype(o_ref.dtype)


def _mem_kv(mem2d, g, w, *, bm=256):
    m, d = mem2d.shape
    n = w.shape[1]
    return pl.pallas_call(
        _mem_kv_kernel,
        out_shape=jax.ShapeDtypeStruct((m, n), BF16),
        grid=(m // bm,),
        in_specs=[
            pl.BlockSpec((bm, d), lambda i: (i, 0)),
            pl.BlockSpec((1, d), lambda i: (0, 0)),
            pl.BlockSpec((d, n), lambda i: (0, 0)),
        ],
        out_specs=pl.BlockSpec((bm, n), lambda i: (i, 0)),
        compiler_params=_params(("parallel",)),
        name="mem_kv",
    )(mem2d, g.reshape(1, d), w)


def _xattn_kernel(x_ref, g_ref, wq_ref, kv_ref, wo_ref, o_ref, oh_scr):
    x = x_ref[...]
    h = _rms(x, g_ref[...]).astype(BF16)
    q = (_dot(h, wq_ref[...]) * (XA_HEAD_DIM ** -0.5)).astype(BF16)
    for hh in range(XA_HEADS):
        lo = hh * XA_HEAD_DIM
        k = kv_ref[:, lo:lo + XA_HEAD_DIM]
        v = kv_ref[:, XA_WIDTH + lo:XA_WIDTH + lo + XA_HEAD_DIM]
        s = _dot_nt(q[:, lo:lo + XA_HEAD_DIM], k)
        p = jnp.exp(s - jnp.max(s, axis=-1, keepdims=True))
        p = p / jnp.sum(p, axis=-1, keepdims=True)
        oh_scr[:, lo:lo + XA_HEAD_DIM] = _dot(p.astype(BF16), v).astype(BF16)
    o_ref[...] = x + _dot(oh_scr[...], wo_ref[...])


def _xattn(x2d, g, wq, kv, wo, seq, *, bm=512):
    m, d = x2d.shape
    nseq = seq // bm
    return pl.pallas_call(
        _xattn_kernel,
        out_shape=jax.ShapeDtypeStruct((m, d), F32),
        grid=(m // bm,),
        in_specs=[
            pl.BlockSpec((bm, d), lambda i: (i, 0)),
            pl.BlockSpec((1, d), lambda i: (0, 0)),
            pl.BlockSpec((d, XA_WIDTH), lambda i: (0, 0)),
            pl.BlockSpec((N_MEM, 2 * XA_WIDTH), lambda i: (i // nseq, 0)),
            pl.BlockSpec((XA_WIDTH, d), lambda i: (0, 0)),
        ],
        out_specs=pl.BlockSpec((bm, d), lambda i: (i, 0)),
        scratch_shapes=[pltpu.VMEM((bm, XA_WIDTH), BF16)],
        compiler_params=_params(("parallel",)),
        name="xattn",
    )(x2d, g.reshape(1, d), wq, kv, wo)


def _ffn_kernel(x_ref, halo_ref, g_ref, wg_ref, wv_ref, cwg_ref, cwv_ref, cbg_ref, cbv_ref,
                wd_ref, fg_ref, o_ref, h_scr, *, nseq, final):
    i = pl.program_id(0)
    f = pl.program_id(1)
    pad = SUBLANES

    @pl.when(f == 0)
    def _():
        g = g_ref[...]
        h_scr[pad:, :] = _rms(x_ref[...], g).astype(BF16)
        keep = (i % nseq != 0).astype(F32)
        h_scr[:pad, :] = (_rms(halo_ref[...], g) * keep).astype(BF16)
        o_ref[...] = jnp.zeros(o_ref.shape, F32)

    h = h_scr[...]

    def conv(w_ref, cw_ref, cb_ref):
        u = _dot(h, w_ref[...])
        cw = cw_ref[...]
        y = (cw[2:3, :] * u + cw[1:2, :] * pltpu.roll(u, 1, 0) + cw[0:1, :] * pltpu.roll(u, 2, 0))
        return y[pad:, :] + cb_ref[...]

    gate = conv(wg_ref, cwg_ref, cbg_ref)
    val = conv(wv_ref, cwv_ref, cbv_ref)
    act = (gate * _sigmoid(gate) * val).astype(BF16)
    o_ref[...] += _dot(act, wd_ref[...])

    @pl.when(f == pl.num_programs(1) - 1)
    def _():
        y = x_ref[...] + o_ref[...]
        o_ref[...] = _rms(y, fg_ref[...]) if final else y


def _ffn(x2d, g, w_up, conv_w, conv_b, w_down, final_g, seq, *, final, bm=1024, bf=512):
    m, d = x2d.shape
    nf = D_FF // bf
    nseq = seq // bm
    hb = bm // SUBLANES
    return pl.pallas_call(
        functools.partial(_ffn_kernel, nseq=nseq, final=final),
        out_shape=jax.ShapeDtypeStruct((m, d), F32),
        grid=(m // bm, nf),
        in_specs=[
            pl.BlockSpec((bm, d), lambda i, f: (i, 0), pipeline_mode=pl.Buffered(1)),
            pl.BlockSpec((SUBLANES, d), lambda i, f: (jnp.maximum(i * hb - 1, 0), 0)),
            pl.BlockSpec((1, d), lambda i, f: (0, 0)),
            pl.BlockSpec((d, bf), lambda i, f: (0, f)),
            pl.BlockSpec((d, bf), lambda i, f: (0, nf + f)),
            pl.BlockSpec((CONV_WIDTH, bf), lambda i, f: (0, f)),
            pl.BlockSpec((CONV_WIDTH, bf), lambda i, f: (0, nf + f)),
            pl.BlockSpec((1, bf), lambda i, f: (0, f)),
            pl.BlockSpec((1, bf), lambda i, f: (0, nf + f)),
            pl.BlockSpec((bf, d), lambda i, f: (f, 0)),
            pl.BlockSpec((1, d), lambda i, f: (0, 0)),
        ],
        out_specs=pl.BlockSpec((bm, d), lambda i, f: (i, 0)),
        scratch_shapes=[pltpu.VMEM((bm + SUBLANES, d), BF16)],
        compiler_params=_params(("parallel", "arbitrary"), VMEM_LIMIT_FFN),
        name="ffn",
    )(x2d, x2d, g.reshape(1, d), w_up, w_up, conv_w, conv_w,
      conv_b.reshape(1, -1), conv_b.reshape(1, -1), w_down, final_g.reshape(1, d))


def kernel(x, mem, norm_mix_g, w_in, lambda_q1, lambda_k1, lambda_q2, lambda_k2, da_subln_g,
           w_proj_a, w_proj_b, w_out, norm_x_g, norm_mem_g, w_xq, w_xkv, w_xo, norm_ffn_g,
           w_up, conv_w, conv_b, w_down, final_norm_g):
    batch, seq, d = x.shape
    depth = w_in.shape[0]
    bf = lambda a: a.astype(BF16)
    x2d = x.reshape(batch * seq, d)
    mem2d = mem.reshape(batch * mem.shape[1], d)
    for l in range(depth):
        lam_init = 0.8 - 0.6 * math.exp(-0.3 * l)
        z = _in_proj(x2d, norm_mix_g[l], bf(w_in[l]), seq)
        oa = _diff_attn_pairs(z, lambda_q1[l], lambda_k1[l], lambda_q2[l], lambda_k2[l],
                        da_subln_g[l], batch, seq, lam_init)
        ob = _sb_attn(z, batch, seq)
        merged = _merge(oa, ob, z, bf(w_proj_a[l]), bf(w_proj_b[l]))
        x2d = _out_proj(merged, bf(w_out[l]), x2d)
        kv = _mem_kv(mem2d, norm_mem_g[l], bf(w_xkv[l]))
        x2d = _xattn(x2d, norm_x_g[l], bf(w_xq[l]), kv, bf(w_xo[l]), seq)
        x2d = _ffn(x2d, norm_ffn_g[l], bf(w_up[l]), conv_w[l], conv_b[l], bf(w_down[l]),
                   final_norm_g, seq, final=(l == depth - 1))
    return x2d.reshape(batch, seq, d)
```

```python
import functools
import math

import jax
import jax.numpy as jnp
from jax import lax
from jax.experimental import pallas as pl
from jax.experimental.pallas import tpu as pltpu

F32 = jnp.float32
BF16 = jnp.bfloat16

D_MODEL = 2048
N_MEM = 256
DA_HEAD_DIM = 64
DA_WIDTH = D_MODEL // 2
DA_HEADS = DA_WIDTH // (2 * DA_HEAD_DIM)
DA_ROT_DIM = DA_HEAD_DIM // 4
SB_HEAD_DIM = 128
SB_WIDTH = D_MODEL // 2
SB_HEADS = SB_WIDTH // SB_HEAD_DIM
XA_HEADS = 4
XA_HEAD_DIM = 128
XA_WIDTH = XA_HEADS * XA_HEAD_DIM
D_FF = 256 * ((8 * D_MODEL // 3 + 255) // 256)
CONV_WIDTH = 3
ROPE_THETA = 500000.0
EPS = 1e-6
N_IN = 3 * DA_WIDTH + 3 * SB_WIDTH + 2 * D_MODEL

LANES = 128
SUBLANES = 8
BF16_ROWS = 16
NEG_BIG = -1e30
LOG2E = math.log2(math.e)
SB_ZERO_LOG2 = -160.0
VMEM_LIMIT = 48 * 1024 * 1024
VMEM_LIMIT_FFN = 60 * 1024 * 1024

COL_QA, COL_KA, COL_VA = 0, DA_WIDTH, 2 * DA_WIDTH
COL_QB, COL_KB, COL_VB = 3 * DA_WIDTH, 3 * DA_WIDTH + SB_WIDTH, 3 * DA_WIDTH + 2 * SB_WIDTH
COL_GA = 3 * DA_WIDTH + 3 * SB_WIDTH
COL_GB = COL_GA + D_MODEL


def _params(sem, vmem_limit=VMEM_LIMIT):
    return pltpu.CompilerParams(dimension_semantics=sem, vmem_limit_bytes=vmem_limit)


def _rms(x, g):
    return x * lax.rsqrt(jnp.mean(x * x, axis=-1, keepdims=True) + EPS) * g


def _dot(a, b):
    return jnp.dot(a, b, preferred_element_type=F32)


def _dot_nt(a, b):
    return lax.dot_general(a, b, (((1,), (1,)), ((), ())), preferred_element_type=F32)


def _in_proj_kernel(x_ref, g_ref, w_ref, cos_ref, sp_ref, sm_ref, o_ref, h_scr, *, bn):
    j = pl.program_id(1)

    @pl.when(j == 0)
    def _():
        h_scr[...] = _rms(x_ref[...], g_ref[...]).astype(BF16)

    acc = _dot(h_scr[...], w_ref[...])
    col = j * bn
    is_qa = col < COL_KA
    is_qb = jnp.logical_and(col >= COL_QB, col < COL_KB)
    scale = jnp.where(is_qa, LOG2E * DA_HEAD_DIM ** -0.5,
                      jnp.where(is_qb, LOG2E * SB_HEAD_DIM ** -0.5, 1.0))
    acc = acc * scale.astype(F32)

    @pl.when(col < COL_VA)
    def _():
        cos, sp, sm = cos_ref[...], sp_ref[...], sm_ref[...]
        for c in range(bn // LANES):
            a = acc[:, c * LANES:(c + 1) * LANES]
            r = a * cos + pltpu.roll(a, DA_ROT_DIM // 2, 1) * sp \
                + pltpu.roll(a, LANES - DA_ROT_DIM // 2, 1) * sm
            o_ref[:, c * LANES:(c + 1) * LANES] = r.astype(o_ref.dtype)

    @pl.when(col >= COL_VA)
    def _():
        o_ref[...] = acc.astype(o_ref.dtype)


def _rope_lane_tables(seq):
    half = DA_ROT_DIM // 2
    inv = ROPE_THETA ** (-jnp.arange(0, DA_ROT_DIM, 2, dtype=F32) / DA_ROT_DIM)
    ang = jnp.arange(seq, dtype=F32)[:, None] * inv[None, :]
    cos, sin = jnp.cos(ang), jnp.sin(ang)
    lane = jnp.arange(LANES) % DA_HEAD_DIM
    idx = lane % half
    lo = (lane < half)[None, :]
    hi = jnp.logical_and(lane >= half, lane < 2 * half)[None, :]
    cos_t = jnp.where(jnp.logical_or(lo, hi), cos[:, idx], 1.0)
    sp_t = jnp.where(hi, sin[:, idx], 0.0)
    sm_t = jnp.where(lo, -sin[:, idx], 0.0)
    return cos_t.astype(F32), sp_t.astype(F32), sm_t.astype(F32)


def _in_proj(x2d, g, w, seq, *, bm=1024, bn=1024):
    m, d = x2d.shape
    n = w.shape[1]
    cos_t, sp_t, sm_t = _rope_lane_tables(seq)
    nseq = seq // bm
    tab_spec = pl.BlockSpec((bm, LANES), lambda i, j: (i % nseq, 0))
    return pl.pallas_call(
        functools.partial(_in_proj_kernel, bn=bn),
        out_shape=jax.ShapeDtypeStruct((m, n), BF16),
        grid=(m // bm, n // bn),
        in_specs=[
            pl.BlockSpec((bm, d), lambda i, j: (i, 0)),
            pl.BlockSpec((1, d), lambda i, j: (0, 0)),
            pl.BlockSpec((d, bn), lambda i, j: (0, j)),
            tab_spec, tab_spec, tab_spec,
        ],
        out_specs=pl.BlockSpec((bm, bn), lambda i, j: (i, j)),
        scratch_shapes=[pltpu.VMEM((bm, d), BF16)],
        compiler_params=_params(("parallel", "arbitrary")),
        name="in_proj",
    )(x2d, g.reshape(1, d), w, cos_t, sp_t, sm_t)


def _transpose_values(v_ref, vt_scr, bk):
    hd = v_ref.shape[1]
    extra = vt_scr.shape[1] - hd

    def body(kb, carry):
        start = pl.multiple_of(kb * bk, bk)
        vt_scr[kb, :hd, :] = v_ref[pl.ds(start, bk), :].astype(F32).T.astype(BF16)
        if extra:
            vt_scr[kb, hd:, :] = jnp.ones((extra, bk), BF16)
        return carry
    lax.fori_loop(0, vt_scr.shape[0], body, 0)


def _diff_head_pair_kernel(q_ref, k_ref, v_ref, lq1_ref, lk1_ref, lq2_ref, lk2_ref, g_ref, o_ref,
                           vt_scr, a_scr, *, bq, bk, nh, lam_init):
    qi = pl.program_id(2)
    hd = 2 * DA_HEAD_DIM
    nmap = 2 * nh

    @pl.when(qi == 0)
    def _():
        for h in range(nh):
            _transpose_values(v_ref.at[:, h * hd:(h + 1) * hd], vt_scr.at[h], bk)

    qts = []
    for h in range(nh):
        qt = q_ref[:, h * hd:(h + 1) * hd].astype(F32).T
        feat = lax.broadcasted_iota(jnp.int32, qt.shape, 0)
        qts.append(jnp.where(feat < DA_HEAD_DIM, qt, 0.0).astype(BF16))
        qts.append(jnp.where(feat >= DA_HEAD_DIM, qt, 0.0).astype(BF16))
    a_scr[...] = jnp.zeros(a_scr.shape, F32)

    def step(kb, ms, nkeys=None):
        masked = nkeys is not None
        nk = nkeys if masked else bk
        start = pl.multiple_of(kb * bk, bk)
        k = k_ref[pl.ds(start, nk), :]
        ss = [_dot(k[:, (i // 2) * hd:(i // 2 + 1) * hd], qts[i]) for i in range(nmap)]
        if masked:
            key = start + lax.broadcasted_iota(jnp.int32, (nk, bq), 0)
            qry = qi * bq + lax.broadcasted_iota(jnp.int32, (nk, bq), 1)
            keep = key <= qry
            ss = [jnp.where(keep, s, NEG_BIG) for s in ss]
        out = []
        for i in range(nmap):
            m_new = jnp.maximum(ms[i], jnp.max(ss[i], axis=0, keepdims=True))
            alpha = jnp.exp2(ms[i] - m_new)
            p = jnp.exp2(ss[i] - m_new).astype(BF16)
            a_scr[i] = alpha * a_scr[i] + _dot(vt_scr[i // 2, kb, :, :nk], p)
            out.append(m_new)
        return tuple(out)

    neg = jnp.full((1, bq), NEG_BIG, F32)
    n_full = (qi * bq) // bk
    ms = lax.fori_loop(0, n_full, lambda kb, c: step(kb, c), (neg,) * nmap)
    per = bk // bq
    lax.switch(qi % per, [functools.partial(step, n_full, nkeys=(r + 1) * bq) for r in range(per)], ms)

    lam = (jnp.exp(jnp.sum(lq1_ref[...] * lk1_ref[...], axis=-1, keepdims=True))
           - jnp.exp(jnp.sum(lq2_ref[...] * lk2_ref[...], axis=-1, keepdims=True)) + lam_init)
    for h in range(nh):
        a1, a2 = a_scr[2 * h], a_scr[2 * h + 1]
        ot = a1[:hd, :] / a1[hd:hd + 1, :] - lam * (a2[:hd, :] / a2[hd:hd + 1, :])
        o = _rms(ot.T, g_ref[...]) * (1.0 - lam_init)
        o_ref[:, h * hd:(h + 1) * hd] = o.astype(o_ref.dtype)


def _diff_attn_pairs(z, lq1, lk1, lq2, lk2, subln_g, batch, seq, lam_init, *, bq=256, bk=1024, nh=4):
    nq = seq // bq
    hd = 2 * DA_HEAD_DIM
    w = nh * hd
    vec = lambda a: a.reshape(1, -1).astype(F32)
    small = lambda n: pl.BlockSpec((1, n), lambda b, h, i: (0, 0))
    return pl.pallas_call(
        functools.partial(_diff_head_pair_kernel, bq=bq, bk=bk, nh=nh, lam_init=lam_init),
        out_shape=jax.ShapeDtypeStruct((batch * seq, DA_WIDTH), BF16),
        grid=(batch, DA_HEADS // nh, nq),
        in_specs=[
            pl.BlockSpec((bq, w), lambda b, h, i: (b * nq + i, COL_QA // w + h)),
            pl.BlockSpec((seq, w), lambda b, h, i: (b, COL_KA // w + h), pipeline_mode=pl.Buffered(1)),
            pl.BlockSpec((seq, w), lambda b, h, i: (b, COL_VA // w + h), pipeline_mode=pl.Buffered(1)),
            small(DA_HEAD_DIM), small(DA_HEAD_DIM), small(DA_HEAD_DIM), small(DA_HEAD_DIM),
            small(hd),
        ],
        out_specs=pl.BlockSpec((bq, w), lambda b, h, i: (b * nq + i, h)),
        scratch_shapes=[
            pltpu.VMEM((nh, seq // bk, hd + BF16_ROWS, bk), BF16),
            pltpu.VMEM((2 * nh, hd + BF16_ROWS, bq), F32),
        ],
        compiler_params=_params(("arbitrary", "arbitrary", "arbitrary")),
        name="diff_attn",
    )(z, z, z, vec(lq1), vec(lk1), vec(lq2), vec(lk2), vec(subln_g))


def _sb_attn_kernel(q_ref, k_ref, v_ref, u_ref, o_ref, vt_scr, a_scr, *, bq, nh):
    qi = pl.program_id(2)
    hd = SB_HEAD_DIM
    bc = bq

    @pl.when(qi == 0)
    def _():
        for h in range(nh):
            _transpose_values(v_ref.at[:, h * hd:(h + 1) * hd], vt_scr.at[h], bc)

    qts = [q_ref[:, h * hd:(h + 1) * hd].astype(F32).T.astype(BF16) for h in range(nh)]
    a_scr[...] = jnp.zeros(a_scr.shape, F32)
    u = u_ref[...]

    def pair(sb0, cs, bound):
        start = pl.multiple_of(sb0 * bc, bc)
        k = k_ref[pl.ds(start, 2 * bc), :]
        zs = [_dot(k[:, h * hd:(h + 1) * hd], qts[h]) for h in range(nh)]
        if bound is not None:
            valid = start + lax.broadcasted_iota(jnp.int32, (2 * bc, bq), 0) < bound
        out = []
        for h in range(nh):
            z = zs[h]
            sp = jnp.log2(1.0 + jnp.exp2(-jnp.abs(z)))
            log_beta = jnp.minimum(z, 0.0) - sp
            log_1m = log_beta - z
            if bound is not None:
                log_1m = jnp.where(valid, log_1m, 0.0)
            l16 = log_1m.astype(BF16)
            c = cs[h]
            tails = [None, None]
            for sb in (1, 0):
                blk = l16[sb * bc:(sb + 1) * bc, :]
                t = _dot(u, blk) + c
                tails[sb] = t
                c = t[0:1, :] + blk[0:1, :].astype(F32)
            a = jnp.exp2(log_beta + jnp.concatenate(tails, axis=0))
            if bound is not None:
                a = jnp.where(valid, a, 0.0)
            a16 = a.astype(BF16)
            a_scr[h] += _dot(vt_scr[h, sb0], a16[:bc, :]) + _dot(vt_scr[h, sb0 + 1], a16[bc:, :])
            out.append(c)
        return tuple(out)

    def live(cs):
        m = cs[0]
        for c in cs[1:]:
            m = jnp.maximum(m, c)
        return jnp.max(m) > SB_ZERO_LOG2

    first = jnp.maximum(qi - 1, 0)
    qry = qi * bq + lax.broadcasted_iota(jnp.int32, (1, bq), 1)
    cs = pair(first, (jnp.zeros((1, bq), F32),) * nh, qry)

    n_pairs = first // 2

    def more(state):
        t, cc = state
        return jnp.logical_and(t < n_pairs, live(cc))

    def walk(state):
        t, cc = state
        return t + 1, pair(first - 2 - 2 * t, cc, None)

    t_end, cs = lax.while_loop(more, walk, (jnp.int32(0), cs))

    @pl.when(jnp.logical_and(jnp.logical_and(first % 2 == 1, t_end == n_pairs), live(cs)))
    def _():
        pair(0, cs, jnp.full((1, bq), bc, jnp.int32))

    for h in range(nh):
        o_ref[:, h * hd:(h + 1) * hd] = a_scr[h].T.astype(o_ref.dtype)


def _sb_attn(z, batch, seq, *, bq=256, nh=4):
    nq = seq // bq
    hd = SB_HEAD_DIM
    w = nh * hd
    r = lax.broadcasted_iota(jnp.int32, (bq, bq), 0)
    c = lax.broadcasted_iota(jnp.int32, (bq, bq), 1)
    u = (c > r).astype(BF16)
    return pl.pallas_call(
        functools.partial(_sb_attn_kernel, bq=bq, nh=nh),
        out_shape=jax.ShapeDtypeStruct((batch * seq, SB_WIDTH), BF16),
        grid=(batch, SB_HEADS // nh, nq),
        in_specs=[
            pl.BlockSpec((bq, w), lambda b, h, i: (b * nq + i, COL_QB // w + h)),
            pl.BlockSpec((seq, w), lambda b, h, i: (b, COL_KB // w + h), pipeline_mode=pl.Buffered(1)),
            pl.BlockSpec((seq, w), lambda b, h, i: (b, COL_VB // w + h), pipeline_mode=pl.Buffered(1)),
            pl.BlockSpec((bq, bq), lambda b, h, i: (0, 0)),
        ],
        out_specs=pl.BlockSpec((bq, w), lambda b, h, i: (b * nq + i, h)),
        scratch_shapes=[pltpu.VMEM((nh, seq // bq, hd, bq), BF16), pltpu.VMEM((nh, hd, bq), F32)],
        compiler_params=_params(("arbitrary", "arbitrary", "arbitrary")),
        name="sb_attn",
    )(z, z, z, u)


def _sigmoid(x):
    return 1.0 / (1.0 + jnp.exp(-x))


def _merge_kernel(oa_ref, ob_ref, ga_ref, gb_ref, wa_ref, wb_ref, o_ref):
    ya = _dot(oa_ref[...], wa_ref[...])
    yb = _dot(ob_ref[...], wb_ref[...])
    ga = _sigmoid(ga_ref[...].astype(F32))
    gb = _sigmoid(gb_ref[...].astype(F32))
    o_ref[...] = (ga * ya + gb * yb).astype(o_ref.dtype)


def _merge(oa, ob, z, wa, wb, *, bm=1024, bn=1024):
    m = oa.shape[0]
    return pl.pallas_call(
        _merge_kernel,
        out_shape=jax.ShapeDtypeStruct((m, D_MODEL), BF16),
        grid=(m // bm, D_MODEL // bn),
        in_specs=[
            pl.BlockSpec((bm, DA_WIDTH), lambda i, j: (i, 0)),
            pl.BlockSpec((bm, SB_WIDTH), lambda i, j: (i, 0)),
            pl.BlockSpec((bm, bn), lambda i, j: (i, COL_GA // bn + j)),
            pl.BlockSpec((bm, bn), lambda i, j: (i, COL_GB // bn + j)),
            pl.BlockSpec((DA_WIDTH, bn), lambda i, j: (0, j)),
            pl.BlockSpec((SB_WIDTH, bn), lambda i, j: (0, j)),
        ],
        out_specs=pl.BlockSpec((bm, bn), lambda i, j: (i, j)),
        compiler_params=_params(("parallel", "parallel")),
        name="merge",
    )(oa, ob, z, z, wa, wb)


def _out_proj_kernel(a_ref, w_ref, x_ref, o_ref):
    o_ref[...] = x_ref[...] + _dot(a_ref[...], w_ref[...])


def _out_proj(a, w, x2d, *, bm=1024, bn=1024):
    m, k = a.shape
    n = w.shape[1]
    return pl.pallas_call(
        _out_proj_kernel,
        out_shape=jax.ShapeDtypeStruct((m, n), F32),
        grid=(m // bm, n // bn),
        in_specs=[
            pl.BlockSpec((bm, k), lambda i, j: (i, 0)),
            pl.BlockSpec((k, bn), lambda i, j: (0, j)),
            pl.BlockSpec((bm, bn), lambda i, j: (i, j)),
        ],
        out_specs=pl.BlockSpec((bm, bn), lambda i, j: (i, j)),
        compiler_params=_params(("parallel", "parallel")),
        name="out_proj",
    )(a, w, x2d)


def _mem_kv_kernel(x_ref, g_ref, w_ref, o_ref):
    h = _rms(x_ref[...], g_ref[...]).astype(BF16)
    o_ref[...] = _dot(h, w_ref[...]).astype(o_ref.dtype)


def _mem_kv(mem2d, g, w, *, bm=256):
    m, d = mem2d.shape
    n = w.shape[1]
    return pl.pallas_call(
        _mem_kv_kernel,
        out_shape=jax.ShapeDtypeStruct((m, n), BF16),
        grid=(m // bm,),
        in_specs=[
            pl.BlockSpec((bm, d), lambda i: (i, 0)),
            pl.BlockSpec((1, d), lambda i: (0, 0)),
            pl.BlockSpec((d, n), lambda i: (0, 0)),
        ],
        out_specs=pl.BlockSpec((bm, n), lambda i: (i, 0)),
        compiler_params=_params(("parallel",)),
        name="mem_kv",
    )(mem2d, g.reshape(1, d), w)


def _xattn_kernel(x_ref, g_ref, wq_ref, kv_ref, wo_ref, o_ref, oh_scr):
    x = x_ref[...]
    h = _rms(x, g_ref[...]).astype(BF16)
    q = (_dot(h, wq_ref[...]) * (XA_HEAD_DIM ** -0.5)).astype(BF16)
    for hh in range(XA_HEADS):
        lo = hh * XA_HEAD_DIM
        k = kv_ref[:, lo:lo + XA_HEAD_DIM]
        v = kv_ref[:, XA_WIDTH + lo:XA_WIDTH + lo + XA_HEAD_DIM]
        s = _dot_nt(q[:, lo:lo + XA_HEAD_DIM], k)
        p = jnp.exp(s - jnp.max(s, axis=-1, keepdims=True))
        p = p / jnp.sum(p, axis=-1, keepdims=True)
        oh_scr[:, lo:lo + XA_HEAD_DIM] = _dot(p.astype(BF16), v).astype(BF16)
    o_ref[...] = x + _dot(oh_scr[...], wo_ref[...])


def _xattn(x2d, g, wq, kv, wo, seq, *, bm=512):
    m, d = x2d.shape
    nseq = seq // bm
    return pl.pallas_call(
        _xattn_kernel,
        out_shape=jax.ShapeDtypeStruct((m, d), F32),
        grid=(m // bm,),
        in_specs=[
            pl.BlockSpec((bm, d), lambda i: (i, 0)),
            pl.BlockSpec((1, d), lambda i: (0, 0)),
            pl.BlockSpec((d, XA_WIDTH), lambda i: (0, 0)),
            pl.BlockSpec((N_MEM, 2 * XA_WIDTH), lambda i: (i // nseq, 0)),
            pl.BlockSpec((XA_WIDTH, d), lambda i: (0, 0)),
        ],
        out_specs=pl.BlockSpec((bm, d), lambda i: (i, 0)),
        scratch_shapes=[pltpu.VMEM((bm, XA_WIDTH), BF16)],
        compiler_params=_params(("parallel",)),
        name="xattn",
    )(x2d, g.reshape(1, d), wq, kv, wo)


def _ffn_kernel(x_ref, halo_ref, g_ref, wg_ref, wv_ref, cwg_ref, cwv_ref, cbg_ref, cbv_ref,
                wd_ref, fg_ref, o_ref, h_scr, *, nseq, final):
    i = pl.program_id(0)
    f = pl.program_id(1)
    pad = SUBLANES

    @pl.when(f == 0)
    def _():
        g = g_ref[...]
        h_scr[pad:, :] = _rms(x_ref[...], g).astype(BF16)
        keep = (i % nseq != 0).astype(F32)
        h_scr[:pad, :] = (_rms(halo_ref[...], g) * keep).astype(BF16)
        o_ref[...] = jnp.zeros(o_ref.shape, F32)

    h = h_scr[...]

    def conv(w_ref, cw_ref, cb_ref):
        u = _dot(h, w_ref[...])
        cw = cw_ref[...]
        y = (cw[2:3, :] * u + cw[1:2, :] * pltpu.roll(u, 1, 0) + cw[0:1, :] * pltpu.roll(u, 2, 0))
        return y[pad:, :] + cb_ref[...]

    gate = conv(wg_ref, cwg_ref, cbg_ref)
    val = conv(wv_ref, cwv_ref, cbv_ref)
    act = (gate * _sigmoid(gate) * val).astype(BF16)
    o_ref[...] += _dot(act, wd_ref[...])

    @pl.when(f == pl.num_programs(1) - 1)
    def _():
        y = x_ref[...] + o_ref[...]
        o_ref[...] = _rms(y, fg_ref[...]) if final else y


def _ffn(x2d, g, w_up, conv_w, conv_b, w_down, final_g, seq, *, final, bm=1024, bf=512):
    m, d = x2d.shape
    nf = D_FF // bf
    nseq = seq // bm
    hb = bm // SUBLANES
    return pl.pallas_call(
        functools.partial(_ffn_kernel, nseq=nseq, final=final),
        out_shape=jax.ShapeDtypeStruct((m, d), F32),
        grid=(m // bm, nf),
        in_specs=[
            pl.BlockSpec((bm, d), lambda i, f: (i, 0), pipeline_mode=pl.Buffered(1)),
            pl.BlockSpec((SUBLANES, d), lambda i, f: (jnp.maximum(i * hb - 1, 0), 0)),
            pl.BlockSpec((1, d), lambda i, f: (0, 0)),
            pl.BlockSpec((d, bf), lambda i, f: (0, f)),
            pl.BlockSpec((d, bf), lambda i, f: (0, nf + f)),
            pl.BlockSpec((CONV_WIDTH, bf), lambda i, f: (0, f)),
            pl.BlockSpec((CONV_WIDTH, bf), lambda i, f: (0, nf + f)),
            pl.BlockSpec((1, bf), lambda i, f: (0, f)),
            pl.BlockSpec((1, bf), lambda i, f: (0, nf + f)),
            pl.BlockSpec((bf, d), lambda i, f: (f, 0)),
            pl.BlockSpec((1, d), lambda i, f: (0, 0)),
        ],
        out_specs=pl.BlockSpec((bm, d), lambda i, f: (i, 0)),
        scratch_shapes=[pltpu.VMEM((bm + SUBLANES, d), BF16)],
        compiler_params=_params(("parallel", "arbitrary"), VMEM_LIMIT_FFN),
        name="ffn",
    )(x2d, x2d, g.reshape(1, d), w_up, w_up, conv_w, conv_w,
      conv_b.reshape(1, -1), conv_b.reshape(1, -1), w_down, final_g.reshape(1, d))


def kernel(x, mem, norm_mix_g, w_in, lambda_q1, lambda_k1, lambda_q2, lambda_k2, da_subln_g,
           w_proj_a, w_proj_b, w_out, norm_x_g, norm_mem_g, w_xq, w_xkv, w_xo, norm_ffn_g,
           w_up, conv_w, conv_b, w_down, final_norm_g):
    batch, seq, d = x.shape
    depth = w_in.shape[0]
    bf = lambda a: a.astype(BF16)
    x2d = x.reshape(batch * seq, d)
    mem2d = mem.reshape(batch * mem.shape[1], d)
    for l in range(depth):
        lam_init = 0.8 - 0.6 * math.exp(-0.3 * l)
        z = _in_proj(x2d, norm_mix_g[l], bf(w_in[l]), seq)
        oa = _diff_attn_pairs(z, lambda_q1[l], lambda_k1[l], lambda_q2[l], lambda_k2[l],
                        da_subln_g[l], batch, seq, lam_init)
        ob = _sb_attn(z, batch, seq)
        merged = _merge(oa, ob, z, bf(w_proj_a[l]), bf(w_proj_b[l]))
        x2d = _out_proj(merged, bf(w_out[l]), x2d)
        kv = _mem_kv(mem2d, norm_mem_g[l], bf(w_xkv[l]))
        x2d = _xattn(x2d, norm_x_g[l], bf(w_xq[l]), kv, bf(w_xo[l]), seq)
        x2d = _ffn(x2d, norm_ffn_g[l], bf(w_up[l]), conv_w[l], conv_b[l], bf(w_down[l]),
                   final_norm_g, seq, final=(l == depth - 1))
    return x2d.reshape(batch, seq, d)
```

```python
import functools
import math

import jax
import jax.numpy as jnp
from jax import lax
from jax.experimental import pallas as pl
from jax.experimental.pallas import tpu as pltpu

F32 = jnp.float32
BF16 = jnp.bfloat16

D_MODEL = 2048
N_MEM = 256
DA_HEAD_DIM = 64
DA_WIDTH = D_MODEL // 2
DA_HEADS = DA_WIDTH // (2 * DA_HEAD_DIM)
DA_ROT_DIM = DA_HEAD_DIM // 4
SB_HEAD_DIM = 128
SB_WIDTH = D_MODEL // 2
SB_HEADS = SB_WIDTH // SB_HEAD_DIM
XA_HEADS = 4
XA_HEAD_DIM = 128
XA_WIDTH = XA_HEADS * XA_HEAD_DIM
D_FF = 256 * ((8 * D_MODEL // 3 + 255) // 256)
CONV_WIDTH = 3
ROPE_THETA = 500000.0
EPS = 1e-6
N_IN = 3 * DA_WIDTH + 3 * SB_WIDTH + 2 * D_MODEL

LANES = 128
SUBLANES = 8
BF16_ROWS = 16
NEG_BIG = -1e30
LOG2E = math.log2(math.e)
SB_ZERO_LOG2 = -160.0
FROZEN_MAX_RISE = 64.0
VMEM_LIMIT = 48 * 1024 * 1024
VMEM_LIMIT_FFN = 60 * 1024 * 1024

COL_QA, COL_KA, COL_VA = 0, DA_WIDTH, 2 * DA_WIDTH
COL_QB, COL_KB, COL_VB = 3 * DA_WIDTH, 3 * DA_WIDTH + SB_WIDTH, 3 * DA_WIDTH + 2 * SB_WIDTH
COL_GA = 3 * DA_WIDTH + 3 * SB_WIDTH
COL_GB = COL_GA + D_MODEL


def _params(sem, vmem_limit=VMEM_LIMIT):
    return pltpu.CompilerParams(dimension_semantics=sem, vmem_limit_bytes=vmem_limit)


def _rms(x, g):
    return x * lax.rsqrt(jnp.mean(x * x, axis=-1, keepdims=True) + EPS) * g


def _dot(a, b):
    return jnp.dot(a, b, preferred_element_type=F32)


def _dot_nt(a, b):
    return lax.dot_general(a, b, (((1,), (1,)), ((), ())), preferred_element_type=F32)


def _in_proj_kernel(x_ref, g_ref, w_ref, cos_ref, sp_ref, sm_ref, o_ref, h_scr, *, bn):
    j = pl.program_id(1)

    @pl.when(j == 0)
    def _():
        h_scr[...] = _rms(x_ref[...], g_ref[...]).astype(BF16)

    acc = _dot(h_scr[...], w_ref[...])
    col = j * bn
    is_qa = col < COL_KA
    is_qb = jnp.logical_and(col >= COL_QB, col < COL_KB)
    scale = jnp.where(is_qa, LOG2E * DA_HEAD_DIM ** -0.5,
                      jnp.where(is_qb, LOG2E * SB_HEAD_DIM ** -0.5, 1.0))
    acc = acc * scale.astype(F32)

    @pl.when(col < COL_VA)
    def _():
        cos, sp, sm = cos_ref[...], sp_ref[...], sm_ref[...]
        for c in range(bn // LANES):
            a = acc[:, c * LANES:(c + 1) * LANES]
            r = a * cos + pltpu.roll(a, DA_ROT_DIM // 2, 1) * sp \
                + pltpu.roll(a, LANES - DA_ROT_DIM // 2, 1) * sm
            o_ref[:, c * LANES:(c + 1) * LANES] = r.astype(o_ref.dtype)

    @pl.when(col >= COL_VA)
    def _():
        o_ref[...] = acc.astype(o_ref.dtype)


def _rope_lane_tables(seq):
    half = DA_ROT_DIM // 2
    inv = ROPE_THETA ** (-jnp.arange(0, DA_ROT_DIM, 2, dtype=F32) / DA_ROT_DIM)
    ang = jnp.arange(seq, dtype=F32)[:, None] * inv[None, :]
    cos, sin = jnp.cos(ang), jnp.sin(ang)
    lane = jnp.arange(LANES) % DA_HEAD_DIM
    idx = lane % half
    lo = (lane < half)[None, :]
    hi = jnp.logical_and(lane >= half, lane < 2 * half)[None, :]
    cos_t = jnp.where(jnp.logical_or(lo, hi), cos[:, idx], 1.0)
    sp_t = jnp.where(hi, sin[:, idx], 0.0)
    sm_t = jnp.where(lo, -sin[:, idx], 0.0)
    return cos_t.astype(F32), sp_t.astype(F32), sm_t.astype(F32)


def _in_proj(x2d, g, w, seq, *, bm=1024, bn=1024):
    m, d = x2d.shape
    n = w.shape[1]
    cos_t, sp_t, sm_t = _rope_lane_tables(seq)
    nseq = seq // bm
    tab_spec = pl.BlockSpec((bm, LANES), lambda i, j: (i % nseq, 0))
    return pl.pallas_call(
        functools.partial(_in_proj_kernel, bn=bn),
        out_shape=jax.ShapeDtypeStruct((m, n), BF16),
        grid=(m // bm, n // bn),
        in_specs=[
            pl.BlockSpec((bm, d), lambda i, j: (i, 0)),
            pl.BlockSpec((1, d), lambda i, j: (0, 0)),
            pl.BlockSpec((d, bn), lambda i, j: (0, j)),
            tab_spec, tab_spec, tab_spec,
        ],
        out_specs=pl.BlockSpec((bm, bn), lambda i, j: (i, j)),
        scratch_shapes=[pltpu.VMEM((bm, d), BF16)],
        compiler_params=_params(("parallel", "arbitrary")),
        name="in_proj",
    )(x2d, g.reshape(1, d), w, cos_t, sp_t, sm_t)


def _transpose_values(v_ref, vt_scr, bk):
    hd = v_ref.shape[1]
    extra = vt_scr.shape[1] - hd

    def body(kb, carry):
        start = pl.multiple_of(kb * bk, bk)
        vt_scr[kb, :hd, :] = v_ref[pl.ds(start, bk), :].astype(F32).T.astype(BF16)
        if extra:
            vt_scr[kb, hd:, :] = jnp.ones((extra, bk), BF16)
        return carry
    lax.fori_loop(0, vt_scr.shape[0], body, 0)


def _diff_head_pair_kernel(q_ref, k_ref, v_ref, lq1_ref, lk1_ref, lq2_ref, lk2_ref, g_ref, o_ref,
                           vt_scr, a_scr, *, bq, bk, nh, lam_init):
    qi = pl.program_id(2)
    hd = 2 * DA_HEAD_DIM
    nmap = 2 * nh

    @pl.when(qi == 0)
    def _():
        for h in range(nh):
            _transpose_values(v_ref.at[:, h * hd:(h + 1) * hd], vt_scr.at[h], bk)

    qts = []
    for h in range(nh):
        qt = q_ref[:, h * hd:(h + 1) * hd].astype(F32).T
        feat = lax.broadcasted_iota(jnp.int32, qt.shape, 0)
        qts.append(jnp.where(feat < DA_HEAD_DIM, qt, 0.0).astype(BF16))
        qts.append(jnp.where(feat >= DA_HEAD_DIM, qt, 0.0).astype(BF16))
    def scores(kb, nkeys):
        nk = bk if nkeys is None else nkeys
        start = pl.multiple_of(kb * bk, bk)
        k = k_ref[pl.ds(start, nk), :]
        ss = [_dot(k[:, (i // 2) * hd:(i // 2 + 1) * hd], qts[i]) for i in range(nmap)]
        if nkeys is not None:
            key = start + lax.broadcasted_iota(jnp.int32, (nk, bq), 0)
            qry = qi * bq + lax.broadcasted_iota(jnp.int32, (nk, bq), 1)
            keep = key <= qry
            ss = [jnp.where(keep, s, NEG_BIG) for s in ss]
        return nk, ss


    def online(kb, ms, nkeys=None):
        nk, ss = scores(kb, nkeys)
        out = []
        for i in range(nmap):
            m_new = jnp.maximum(ms[i], jnp.max(ss[i], axis=0, keepdims=True))
            alpha = jnp.exp2(ms[i] - m_new)
            p = jnp.exp2(ss[i] - m_new).astype(BF16)
            a_scr[i] = alpha * a_scr[i] + _dot(vt_scr[i // 2, kb, :, :nk], p)
            out.append(m_new)
        return tuple(out)

    def frozen(kb, carry, nkeys=None):
        ms, gs = carry
        nk, ss = scores(kb, nkeys)
        out = []
        for i in range(nmap):
            p = jnp.exp2(ss[i] - ms[i]).astype(BF16)
            a_scr[i] += _dot(vt_scr[i // 2, kb, :, :nk], p)
            out.append(jnp.maximum(gs[i], jnp.max(ss[i], axis=0, keepdims=True)))
        return ms, tuple(out)

    neg = jnp.full((1, bq), NEG_BIG, F32)
    n_full = (qi * bq) // bk
    per = bk // bq
    diag = lambda fn: [functools.partial(fn, n_full, nkeys=(r + 1) * bq) for r in range(per)]

    def run_online():
        a_scr[...] = jnp.zeros(a_scr.shape, F32)
        ms = lax.fori_loop(0, n_full, online, (neg,) * nmap)
        lax.switch(qi % per, diag(online), ms)

    def run_frozen():
        a_scr[...] = jnp.zeros(a_scr.shape, F32)
        ms = online(0, (neg,) * nmap)
        carry = lax.fori_loop(1, n_full, frozen, (ms, ms))
        ms, gs = lax.switch(qi % per, diag(frozen), carry)
        rise = gs[0] - ms[0]
        for i in range(1, nmap):
            rise = jnp.maximum(rise, gs[i] - ms[i])
        return (jnp.max(rise) <= FROZEN_MAX_RISE).astype(jnp.int32)

    ok = lax.cond(n_full >= 1, run_frozen, lambda: jnp.int32(0))

    @pl.when(ok == 0)
    def _():
        run_online()

    lam = (jnp.exp(jnp.sum(lq1_ref[...] * lk1_ref[...], axis=-1, keepdims=True))
           - jnp.exp(jnp.sum(lq2_ref[...] * lk2_ref[...], axis=-1, keepdims=True)) + lam_init)
    for h in range(nh):
        a1, a2 = a_scr[2 * h], a_scr[2 * h + 1]
        ot = a1[:hd, :] / a1[hd:hd + 1, :] - lam * (a2[:hd, :] / a2[hd:hd + 1, :])
        o = _rms(ot.T, g_ref[...]) * (1.0 - lam_init)
        o_ref[:, h * hd:(h + 1) * hd] = o.astype(o_ref.dtype)


def _diff_attn_pairs(z, lq1, lk1, lq2, lk2, subln_g, batch, seq, lam_init, *, bq=256, bk=1024, nh=4):
    nq = seq // bq
    hd = 2 * DA_HEAD_DIM
    w = nh * hd
    vec = lambda a: a.reshape(1, -1).astype(F32)
    small = lambda n: pl.BlockSpec((1, n), lambda b, h, i: (0, 0))
    return pl.pallas_call(
        functools.partial(_diff_head_pair_kernel, bq=bq, bk=bk, nh=nh, lam_init=lam_init),
        out_shape=jax.ShapeDtypeStruct((batch * seq, DA_WIDTH), BF16),
        grid=(batch, DA_HEADS // nh, nq),
        in_specs=[
            pl.BlockSpec((bq, w), lambda b, h, i: (b * nq + i, COL_QA // w + h)),
            pl.BlockSpec((seq, w), lambda b, h, i: (b, COL_KA // w + h), pipeline_mode=pl.Buffered(1)),
            pl.BlockSpec((seq, w), lambda b, h, i: (b, COL_VA // w + h), pipeline_mode=pl.Buffered(1)),
            small(DA_HEAD_DIM), small(DA_HEAD_DIM), small(DA_HEAD_DIM), small(DA_HEAD_DIM),
            small(hd),
        ],
        out_specs=pl.BlockSpec((bq, w), lambda b, h, i: (b * nq + i, h)),
        scratch_shapes=[
            pltpu.VMEM((nh, seq // bk, hd + BF16_ROWS, bk), BF16),
            pltpu.VMEM((2 * nh, hd + BF16_ROWS, bq), F32),
        ],
        compiler_params=_params(("arbitrary", "arbitrary", "arbitrary")),
        name="diff_attn",
    )(z, z, z, vec(lq1), vec(lk1), vec(lq2), vec(lk2), vec(subln_g))


def _sb_attn_kernel(q_ref, k_ref, v_ref, u_ref, o_ref, vt_scr, a_scr, *, bq, nh):
    qi = pl.program_id(2)
    hd = SB_HEAD_DIM
    bc = bq

    @pl.when(qi == 0)
    def _():
        for h in range(nh):
            _transpose_values(v_ref.at[:, h * hd:(h + 1) * hd], vt_scr.at[h], bc)

    qts = [q_ref[:, h * hd:(h + 1) * hd].astype(F32).T.astype(BF16) for h in range(nh)]
    a_scr[...] = jnp.zeros(a_scr.shape, F32)
    u = u_ref[...]

    def pair(sb0, cs, bound):
        start = pl.multiple_of(sb0 * bc, bc)
        k = k_ref[pl.ds(start, 2 * bc), :]
        zs = [_dot(k[:, h * hd:(h + 1) * hd], qts[h]) for h in range(nh)]
        if bound is not None:
            valid = start + lax.broadcasted_iota(jnp.int32, (2 * bc, bq), 0) < bound
        out = []
        for h in range(nh):
            z = zs[h]
            sp = jnp.log2(1.0 + jnp.exp2(-jnp.abs(z)))
            log_beta = jnp.minimum(z, 0.0) - sp
            log_1m = log_beta - z
            if bound is not None:
                log_1m = jnp.where(valid, log_1m, 0.0)
            l16 = log_1m.astype(BF16)
            c = cs[h]
            tails = [None, None]
            for sb in (1, 0):
                blk = l16[sb * bc:(sb + 1) * bc, :]
                t = _dot(u, blk) + c
                tails[sb] = t
                c = t[0:1, :] + blk[0:1, :].astype(F32)
            a = jnp.exp2(log_beta + jnp.concatenate(tails, axis=0))
            if bound is not None:
                a = jnp.where(valid, a, 0.0)
            a16 = a.astype(BF16)
            a_scr[h] += _dot(vt_scr[h, sb0], a16[:bc, :]) + _dot(vt_scr[h, sb0 + 1], a16[bc:, :])
            out.append(c)
        return tuple(out)

    def live(cs):
        m = cs[0]
        for c in cs[1:]:
            m = jnp.maximum(m, c)
        return jnp.max(m) > SB_ZERO_LOG2

    first = jnp.maximum(qi - 1, 0)
    qry = qi * bq + lax.broadcasted_iota(jnp.int32, (1, bq), 1)
    cs = pair(first, (jnp.zeros((1, bq), F32),) * nh, qry)

    n_pairs = first // 2

    def more(state):
        t, cc = state
        return jnp.logical_and(t < n_pairs, live(cc))

    def walk(state):
        t, cc = state
        return t + 1, pair(first - 2 - 2 * t, cc, None)

    t_end, cs = lax.while_loop(more, walk, (jnp.int32(0), cs))

    @pl.when(jnp.logical_and(jnp.logical_and(first % 2 == 1, t_end == n_pairs), live(cs)))
    def _():
        pair(0, cs, jnp.full((1, bq), bc, jnp.int32))

    for h in range(nh):
        o_ref[:, h * hd:(h + 1) * hd] = a_scr[h].T.astype(o_ref.dtype)


def _sb_attn(z, batch, seq, *, bq=256, nh=4):
    nq = seq // bq
    hd = SB_HEAD_DIM
    w = nh * hd
    r = lax.broadcasted_iota(jnp.int32, (bq, bq), 0)
    c = lax.broadcasted_iota(jnp.int32, (bq, bq), 1)
    u = (c > r).astype(BF16)
    return pl.pallas_call(
        functools.partial(_sb_attn_kernel, bq=bq, nh=nh),
        out_shape=jax.ShapeDtypeStruct((batch * seq, SB_WIDTH), BF16),
        grid=(batch, SB_HEADS // nh, nq),
        in_specs=[
            pl.BlockSpec((bq, w), lambda b, h, i: (b * nq + i, COL_QB // w + h)),
            pl.BlockSpec((seq, w), lambda b, h, i: (b, COL_KB // w + h), pipeline_mode=pl.Buffered(1)),
            pl.BlockSpec((seq, w), lambda b, h, i: (b, COL_VB // w + h), pipeline_mode=pl.Buffered(1)),
            pl.BlockSpec((bq, bq), lambda b, h, i: (0, 0)),
        ],
        out_specs=pl.BlockSpec((bq, w), lambda b, h, i: (b * nq + i, h)),
        scratch_shapes=[pltpu.VMEM((nh, seq // bq, hd, bq), BF16), pltpu.VMEM((nh, hd, bq), F32)],
        compiler_params=_params(("arbitrary", "arbitrary", "arbitrary")),
        name="sb_attn",
    )(z, z, z, u)


def _sigmoid(x):
    return 1.0 / (1.0 + jnp.exp(-x))


def _merge_kernel(oa_ref, ob_ref, ga_ref, gb_ref, wa_ref, wb_ref, o_ref):
    ya = _dot(oa_ref[...], wa_ref[...])
    yb = _dot(ob_ref[...], wb_ref[...])
    ga = _sigmoid(ga_ref[...].astype(F32))
    gb = _sigmoid(gb_ref[...].astype(F32))
    o_ref[...] = (ga * ya + gb * yb).astype(o_ref.dtype)


def _merge(oa, ob, z, wa, wb, *, bm=1024, bn=1024):
    m = oa.shape[0]
    return pl.pallas_call(
        _merge_kernel,
        out_shape=jax.ShapeDtypeStruct((m, D_MODEL), BF16),
        grid=(m // bm, D_MODEL // bn),
        in_specs=[
            pl.BlockSpec((bm, DA_WIDTH), lambda i, j: (i, 0)),
            pl.BlockSpec((bm, SB_WIDTH), lambda i, j: (i, 0)),
            pl.BlockSpec((bm, bn), lambda i, j: (i, COL_GA // bn + j)),
            pl.BlockSpec((bm, bn), lambda i, j: (i, COL_GB // bn + j)),
            pl.BlockSpec((DA_WIDTH, bn), lambda i, j: (0, j)),
            pl.BlockSpec((SB_WIDTH, bn), lambda i, j: (0, j)),
        ],
        out_specs=pl.BlockSpec((bm, bn), lambda i, j: (i, j)),
        compiler_params=_params(("parallel", "parallel")),
        name="merge",
    )(oa, ob, z, z, wa, wb)


def _out_proj_kernel(a_ref, w_ref, x_ref, o_ref):
    o_ref[...] = x_ref[...] + _dot(a_ref[...], w_ref[...])


def _out_proj(a, w, x2d, *, bm=1024, bn=1024):
    m, k = a.shape
    n = w.shape[1]
    return pl.pallas_call(
        _out_proj_kernel,
        out_shape=jax.ShapeDtypeStruct((m, n), F32),
        grid=(m // bm, n // bn),
        in_specs=[
            pl.BlockSpec((bm, k), lambda i, j: (i, 0)),
            pl.BlockSpec((k, bn), lambda i, j: (0, j)),
            pl.BlockSpec((bm, bn), lambda i, j: (i, j)),
        ],
        out_specs=pl.BlockSpec((bm, bn), lambda i, j: (i, j)),
        compiler_params=_params(("parallel", "parallel")),
        name="out_proj",
    )(a, w, x2d)


def _mem_kv_kernel(x_ref, g_ref, w_ref, o_ref):
    h = _rms(x_ref[...], g_ref[...]).astype(BF16)
    o_ref[...] = _dot(h, w_ref[...]).astype(o_ref.dtype)


def _mem_kv(mem2d, g, w, *, bm=256):
    m, d = mem2d.shape
    n = w.shape[1]
    return pl.pallas_call(
        _mem_kv_kernel,
        out_shape=jax.ShapeDtypeStruct((m, n), BF16),
        grid=(m // bm,),
        in_specs=[
            pl.BlockSpec((bm, d), lambda i: (i, 0)),
            pl.BlockSpec((1, d), lambda i: (0, 0)),
            pl.BlockSpec((d, n), lambda i: (0, 0)),
        ],
        out_specs=pl.BlockSpec((bm, n), lambda i: (i, 0)),
        compiler_params=_params(("parallel",)),
        name="mem_kv",
    )(mem2d, g.reshape(1, d), w)


def _xattn_kernel(x_ref, g_ref, wq_ref, kv_ref, wo_ref, o_ref, oh_scr):
    x = x_ref[...]
    h = _rms(x, g_ref[...]).astype(BF16)
    q = (_dot(h, wq_ref[...]) * (XA_HEAD_DIM ** -0.5)).astype(BF16)
    for hh in range(XA_HEADS):
        lo = hh * XA_HEAD_DIM
        k = kv_ref[:, lo:lo + XA_HEAD_DIM]
        v = kv_ref[:, XA_WIDTH + lo:XA_WIDTH + lo + XA_HEAD_DIM]
        s = _dot_nt(q[:, lo:lo + XA_HEAD_DIM], k)
        p = jnp.exp(s - jnp.max(s, axis=-1, keepdims=True))
        p = p / jnp.sum(p, axis=-1, keepdims=True)
        oh_scr[:, lo:lo + XA_HEAD_DIM] = _dot(p.astype(BF16), v).astype(BF16)
    o_ref[...] = x + _dot(oh_scr[...], wo_ref[...])


def _xattn(x2d, g, wq, kv, wo, seq, *, bm=512):
    m, d = x2d.shape
    nseq = seq // bm
    return pl.pallas_call(
        _xattn_kernel,
        out_shape=jax.ShapeDtypeStruct((m, d), F32),
        grid=(m // bm,),
        in_specs=[
            pl.BlockSpec((bm, d), lambda i: (i, 0)),
            pl.BlockSpec((1, d), lambda i: (0, 0)),
            pl.BlockSpec((d, XA_WIDTH), lambda i: (0, 0)),
            pl.BlockSpec((N_MEM, 2 * XA_WIDTH), lambda i: (i // nseq, 0)),
            pl.BlockSpec((XA_WIDTH, d), lambda i: (0, 0)),
        ],
        out_specs=pl.BlockSpec((bm, d), lambda i: (i, 0)),
        scratch_shapes=[pltpu.VMEM((bm, XA_WIDTH), BF16)],
        compiler_params=_params(("parallel",)),
        name="xattn",
    )(x2d, g.reshape(1, d), wq, kv, wo)


def _ffn_kernel(x_ref, halo_ref, g_ref, wg_ref, wv_ref, cwg_ref, cwv_ref, cbg_ref, cbv_ref,
                wd_ref, fg_ref, o_ref, h_scr, *, nseq, final):
    i = pl.program_id(0)
    f = pl.program_id(1)
    pad = SUBLANES

    @pl.when(f == 0)
    def _():
        g = g_ref[...]
        h_scr[pad:, :] = _rms(x_ref[...], g).astype(BF16)
        keep = (i % nseq != 0).astype(F32)
        h_scr[:pad, :] = (_rms(halo_ref[...], g) * keep).astype(BF16)
        o_ref[...] = jnp.zeros(o_ref.shape, F32)

    h = h_scr[...]

    def conv(w_ref, cw_ref, cb_ref):
        u = _dot(h, w_ref[...])
        cw = cw_ref[...]
        y = (cw[2:3, :] * u + cw[1:2, :] * pltpu.roll(u, 1, 0) + cw[0:1, :] * pltpu.roll(u, 2, 0))
        return y[pad:, :] + cb_ref[...]

    gate = conv(wg_ref, cwg_ref, cbg_ref)
    val = conv(wv_ref, cwv_ref, cbv_ref)
    act = (gate * _sigmoid(gate) * val).astype(BF16)
    o_ref[...] += _dot(act, wd_ref[...])

    @pl.when(f == pl.num_programs(1) - 1)
    def _():
        y = x_ref[...] + o_ref[...]
        o_ref[...] = _rms(y, fg_ref[...]) if final else y


def _ffn(x2d, g, w_up, conv_w, conv_b, w_down, final_g, seq, *, final, bm=1024, bf=512):
    m, d = x2d.shape
    nf = D_FF // bf
    nseq = seq // bm
    hb = bm // SUBLANES
    return pl.pallas_call(
        functools.partial(_ffn_kernel, nseq=nseq, final=final),
        out_shape=jax.ShapeDtypeStruct((m, d), F32),
        grid=(m // bm, nf),
        in_specs=[
            pl.BlockSpec((bm, d), lambda i, f: (i, 0), pipeline_mode=pl.Buffered(1)),
            pl.BlockSpec((SUBLANES, d), lambda i, f: (jnp.maximum(i * hb - 1, 0), 0)),
            pl.BlockSpec((1, d), lambda i, f: (0, 0)),
            pl.BlockSpec((d, bf), lambda i, f: (0, f)),
            pl.BlockSpec((d, bf), lambda i, f: (0, nf + f)),
            pl.BlockSpec((CONV_WIDTH, bf), lambda i, f: (0, f)),
            pl.BlockSpec((CONV_WIDTH, bf), lambda i, f: (0, nf + f)),
            pl.BlockSpec((1, bf), lambda i, f: (0, f)),
            pl.BlockSpec((1, bf), lambda i, f: (0, nf + f)),
            pl.BlockSpec((bf, d), lambda i, f: (f, 0)),
            pl.BlockSpec((1, d), lambda i, f: (0, 0)),
        ],
        out_specs=pl.BlockSpec((bm, d), lambda i, f: (i, 0)),
        scratch_shapes=[pltpu.VMEM((bm + SUBLANES, d), BF16)],
        compiler_params=_params(("parallel", "arbitrary"), VMEM_LIMIT_FFN),
        name="ffn",
    )(x2d, x2d, g.reshape(1, d), w_up, w_up, conv_w, conv_w,
      conv_b.reshape(1, -1), conv_b.reshape(1, -1), w_down, final_g.reshape(1, d))


def kernel(x, mem, norm_mix_g, w_in, lambda_q1, lambda_k1, lambda_q2, lambda_k2, da_subln_g,
           w_proj_a, w_proj_b, w_out, norm_x_g, norm_mem_g, w_xq, w_xkv, w_xo, norm_ffn_g,
           w_up, conv_w, conv_b, w_down, final_norm_g):
    batch, seq, d = x.shape
    depth = w_in.shape[0]
    bf = lambda a: a.astype(BF16)
    x2d = x.reshape(batch * seq, d)
    mem2d = mem.reshape(batch * mem.shape[1], d)
    for l in range(depth):
        lam_init = 0.8 - 0.6 * math.exp(-0.3 * l)
        z = _in_proj(x2d, norm_mix_g[l], bf(w_in[l]), seq)
        oa = _diff_attn_pairs(z, lambda_q1[l], lambda_k1[l], lambda_q2[l], lambda_k2[l],
                        da_subln_g[l], batch, seq, lam_init)
        ob = _sb_attn(z, batch, seq)
        merged = _merge(oa, ob, z, bf(w_proj_a[l]), bf(w_proj_b[l]))
        x2d = _out_proj(merged, bf(w_out[l]), x2d)
        kv = _mem_kv(mem2d, norm_mem_g[l], bf(w_xkv[l]))
        x2d = _xattn(x2d, norm_x_g[l], bf(w_xq[l]), kv, bf(w_xo[l]), seq)
        x2d = _ffn(x2d, norm_ffn_g[l], bf(w_up[l]), conv_w[l], conv_b[l], bf(w_down[l]),
                   final_norm_g, seq, final=(l == depth - 1))
    return x2d.reshape(batch, seq, d)
```

```python
import functools
import math

import jax
import jax.numpy as jnp
from jax import lax
from jax.experimental import pallas as pl
from jax.experimental.pallas import tpu as pltpu

F32 = jnp.float32
BF16 = jnp.bfloat16

D_MODEL = 2048
N_MEM = 256
DA_HEAD_DIM = 64
DA_WIDTH = D_MODEL // 2
DA_HEADS = DA_WIDTH // (2 * DA_HEAD_DIM)
DA_ROT_DIM = DA_HEAD_DIM // 4
SB_HEAD_DIM = 128
SB_WIDTH = D_MODEL // 2
SB_HEADS = SB_WIDTH // SB_HEAD_DIM
XA_HEADS = 4
XA_HEAD_DIM = 128
XA_WIDTH = XA_HEADS * XA_HEAD_DIM
D_FF = 256 * ((8 * D_MODEL // 3 + 255) // 256)
CONV_WIDTH = 3
ROPE_THETA = 500000.0
EPS = 1e-6
N_IN = 3 * DA_WIDTH + 3 * SB_WIDTH + 2 * D_MODEL

LANES = 128
SUBLANES = 8
BF16_ROWS = 16
NEG_BIG = -1e30
LOG2E = math.log2(math.e)
SB_ZERO_LOG2 = -160.0
FROZEN_MAX_RISE = 64.0
VMEM_LIMIT = 48 * 1024 * 1024
VMEM_LIMIT_FFN = 60 * 1024 * 1024

COL_QA, COL_KA, COL_VA = 0, DA_WIDTH, 2 * DA_WIDTH
COL_QB, COL_KB, COL_VB = 3 * DA_WIDTH, 3 * DA_WIDTH + SB_WIDTH, 3 * DA_WIDTH + 2 * SB_WIDTH
COL_GA = 3 * DA_WIDTH + 3 * SB_WIDTH
COL_GB = COL_GA + D_MODEL


def _params(sem, vmem_limit=VMEM_LIMIT):
    return pltpu.CompilerParams(dimension_semantics=sem, vmem_limit_bytes=vmem_limit)


def _rms(x, g):
    return x * lax.rsqrt(jnp.mean(x * x, axis=-1, keepdims=True) + EPS) * g


def _dot(a, b):
    return jnp.dot(a, b, preferred_element_type=F32)


def _dot_nt(a, b):
    return lax.dot_general(a, b, (((1,), (1,)), ((), ())), preferred_element_type=F32)


def _in_proj_kernel(x_ref, g_ref, w_ref, cos_ref, sp_ref, sm_ref, o_ref, h_scr, *, bn):
    j = pl.program_id(1)

    @pl.when(j == 0)
    def _():
        h_scr[...] = _rms(x_ref[...], g_ref[...]).astype(BF16)

    acc = _dot(h_scr[...], w_ref[...])
    col = j * bn
    is_qa = col < COL_KA
    is_qb = jnp.logical_and(col >= COL_QB, col < COL_KB)
    scale = jnp.where(is_qa, LOG2E * DA_HEAD_DIM ** -0.5,
                      jnp.where(is_qb, LOG2E * SB_HEAD_DIM ** -0.5, 1.0))
    acc = acc * scale.astype(F32)

    @pl.when(col < COL_VA)
    def _():
        cos, sp, sm = cos_ref[...], sp_ref[...], sm_ref[...]
        for c in range(bn // LANES):
            a = acc[:, c * LANES:(c + 1) * LANES]
            r = a * cos + pltpu.roll(a, DA_ROT_DIM // 2, 1) * sp \
                + pltpu.roll(a, LANES - DA_ROT_DIM // 2, 1) * sm
            o_ref[:, c * LANES:(c + 1) * LANES] = r.astype(o_ref.dtype)

    @pl.when(col >= COL_VA)
    def _():
        o_ref[...] = acc.astype(o_ref.dtype)


def _rope_lane_tables(seq):
    half = DA_ROT_DIM // 2
    inv = ROPE_THETA ** (-jnp.arange(0, DA_ROT_DIM, 2, dtype=F32) / DA_ROT_DIM)
    ang = jnp.arange(seq, dtype=F32)[:, None] * inv[None, :]
    cos, sin = jnp.cos(ang), jnp.sin(ang)
    lane = jnp.arange(LANES) % DA_HEAD_DIM
    idx = lane % half
    lo = (lane < half)[None, :]
    hi = jnp.logical_and(lane >= half, lane < 2 * half)[None, :]
    cos_t = jnp.where(jnp.logical_or(lo, hi), cos[:, idx], 1.0)
    sp_t = jnp.where(hi, sin[:, idx], 0.0)
    sm_t = jnp.where(lo, -sin[:, idx], 0.0)
    return cos_t.astype(F32), sp_t.astype(F32), sm_t.astype(F32)


def _in_proj(x2d, g, w, seq, *, bm=1024, bn=1024):
    m, d = x2d.shape
    n = w.shape[1]
    cos_t, sp_t, sm_t = _rope_lane_tables(seq)
    nseq = seq // bm
    tab_spec = pl.BlockSpec((bm, LANES), lambda i, j: (i % nseq, 0))
    return pl.pallas_call(
        functools.partial(_in_proj_kernel, bn=bn),
        out_shape=jax.ShapeDtypeStruct((m, n), BF16),
        grid=(m // bm, n // bn),
        in_specs=[
            pl.BlockSpec((bm, d), lambda i, j: (i, 0)),
            pl.BlockSpec((1, d), lambda i, j: (0, 0)),
            pl.BlockSpec((d, bn), lambda i, j: (0, j)),
            tab_spec, tab_spec, tab_spec,
        ],
        out_specs=pl.BlockSpec((bm, bn), lambda i, j: (i, j)),
        scratch_shapes=[pltpu.VMEM((bm, d), BF16)],
        compiler_params=_params(("parallel", "arbitrary")),
        name="in_proj",
    )(x2d, g.reshape(1, d), w, cos_t, sp_t, sm_t)


def _transpose_values(v_ref, vt_scr, bk):
    hd = v_ref.shape[1]
    extra = vt_scr.shape[1] - hd

    def body(kb, carry):
        start = pl.multiple_of(kb * bk, bk)
        vt_scr[kb, :hd, :] = v_ref[pl.ds(start, bk), :].astype(F32).T.astype(BF16)
        if extra:
            vt_scr[kb, hd:, :] = jnp.ones((extra, bk), BF16)
        return carry
    lax.fori_loop(0, vt_scr.shape[0], body, 0)


def _diff_head_pair_kernel(q_ref, k_ref, v_ref, lq1_ref, lk1_ref, lq2_ref, lk2_ref, g_ref, o_ref,
                           vt_scr, a_scr, *, bq, bk, nh, lam_init):
    qi = pl.program_id(2)
    hd = 2 * DA_HEAD_DIM
    nmap = 2 * nh

    @pl.when(qi == 0)
    def _():
        for h in range(nh):
            _transpose_values(v_ref.at[:, h * hd:(h + 1) * hd], vt_scr.at[h], bk)

    qts = []
    for h in range(nh):
        qt = q_ref[:, h * hd:(h + 1) * hd].astype(F32).T
        feat = lax.broadcasted_iota(jnp.int32, qt.shape, 0)
        qts.append(jnp.where(feat < DA_HEAD_DIM, qt, 0.0).astype(BF16))
        qts.append(jnp.where(feat >= DA_HEAD_DIM, qt, 0.0).astype(BF16))
    def scores(kb, nkeys):
        nk = bk if nkeys is None else nkeys
        start = pl.multiple_of(kb * bk, bk)
        k = k_ref[pl.ds(start, nk), :]
        ss = [_dot(k[:, (i // 2) * hd:(i // 2 + 1) * hd], qts[i]) for i in range(nmap)]
        if nkeys is not None:
            key = start + lax.broadcasted_iota(jnp.int32, (nk, bq), 0)
            qry = qi * bq + lax.broadcasted_iota(jnp.int32, (nk, bq), 1)
            keep = key <= qry
            ss = [jnp.where(keep, s, NEG_BIG) for s in ss]
        return nk, ss


    def online(kb, ms, nkeys=None):
        nk, ss = scores(kb, nkeys)
        out = []
        for i in range(nmap):
            m_new = jnp.maximum(ms[i], jnp.max(ss[i], axis=0, keepdims=True))
            alpha = jnp.exp2(ms[i] - m_new)
            p = jnp.exp2(ss[i] - m_new).astype(BF16)
            a_scr[i] = alpha * a_scr[i] + _dot(vt_scr[i // 2, kb, :, :nk], p)
            out.append(m_new)
        return tuple(out)

    def frozen(kb, carry, nkeys=None):
        ms, gs = carry
        nk, ss = scores(kb, nkeys)
        out = []
        for i in range(nmap):
            p = jnp.exp2(ss[i] - ms[i]).astype(BF16)
            a_scr[i] += _dot(vt_scr[i // 2, kb, :, :nk], p)
            out.append(jnp.maximum(gs[i], jnp.max(ss[i], axis=0, keepdims=True)))
        return ms, tuple(out)

    neg = jnp.full((1, bq), NEG_BIG, F32)
    n_full = (qi * bq) // bk
    per = bk // bq
    diag = lambda fn: [functools.partial(fn, n_full, nkeys=(r + 1) * bq) for r in range(per)]

    def run_online():
        a_scr[...] = jnp.zeros(a_scr.shape, F32)
        ms = lax.fori_loop(0, n_full, online, (neg,) * nmap)
        lax.switch(qi % per, diag(online), ms)

    def run_frozen():
        a_scr[...] = jnp.zeros(a_scr.shape, F32)
        ms = lax.switch(qi % per, diag(online), (neg,) * nmap)
        ms, gs = lax.fori_loop(0, n_full, frozen, (ms, ms))
        rise = gs[0] - ms[0]
        for i in range(1, nmap):
            rise = jnp.maximum(rise, gs[i] - ms[i])
        return jnp.max(rise) <= FROZEN_MAX_RISE

    @pl.when(jnp.logical_not(run_frozen()))
    def _():
        run_online()

    lam = (jnp.exp(jnp.sum(lq1_ref[...] * lk1_ref[...], axis=-1, keepdims=True))
           - jnp.exp(jnp.sum(lq2_ref[...] * lk2_ref[...], axis=-1, keepdims=True)) + lam_init)
    for h in range(nh):
        a1, a2 = a_scr[2 * h], a_scr[2 * h + 1]
        ot = a1[:hd, :] / a1[hd:hd + 1, :] - lam * (a2[:hd, :] / a2[hd:hd + 1, :])
        o = _rms(ot.T, g_ref[...]) * (1.0 - lam_init)
        o_ref[:, h * hd:(h + 1) * hd] = o.astype(o_ref.dtype)


def _diff_attn_pairs(z, lq1, lk1, lq2, lk2, subln_g, batch, seq, lam_init, *, bq=256, bk=1024, nh=4):
    nq = seq // bq
    hd = 2 * DA_HEAD_DIM
    w = nh * hd
    vec = lambda a: a.reshape(1, -1).astype(F32)
    small = lambda n: pl.BlockSpec((1, n), lambda b, h, i: (0, 0))
    return pl.pallas_call(
        functools.partial(_diff_head_pair_kernel, bq=bq, bk=bk, nh=nh, lam_init=lam_init),
        out_shape=jax.ShapeDtypeStruct((batch * seq, DA_WIDTH), BF16),
        grid=(batch, DA_HEADS // nh, nq),
        in_specs=[
            pl.BlockSpec((bq, w), lambda b, h, i: (b * nq + i, COL_QA // w + h)),
            pl.BlockSpec((seq, w), lambda b, h, i: (b, COL_KA // w + h), pipeline_mode=pl.Buffered(1)),
            pl.BlockSpec((seq, w), lambda b, h, i: (b, COL_VA // w + h), pipeline_mode=pl.Buffered(1)),
            small(DA_HEAD_DIM), small(DA_HEAD_DIM), small(DA_HEAD_DIM), small(DA_HEAD_DIM),
            small(hd),
        ],
        out_specs=pl.BlockSpec((bq, w), lambda b, h, i: (b * nq + i, h)),
        scratch_shapes=[
            pltpu.VMEM((nh, seq // bk, hd + BF16_ROWS, bk), BF16),
            pltpu.VMEM((2 * nh, hd + BF16_ROWS, bq), F32),
        ],
        compiler_params=_params(("arbitrary", "arbitrary", "arbitrary")),
        name="diff_attn",
    )(z, z, z, vec(lq1), vec(lk1), vec(lq2), vec(lk2), vec(subln_g))


def _sb_attn_kernel(q_ref, k_ref, v_ref, u_ref, o_ref, vt_scr, a_scr, *, bq, nh):
    qi = pl.program_id(2)
    hd = SB_HEAD_DIM
    bc = bq

    @pl.when(qi == 0)
    def _():
        for h in range(nh):
            _transpose_values(v_ref.at[:, h * hd:(h + 1) * hd], vt_scr.at[h], bc)

    qts = [q_ref[:, h * hd:(h + 1) * hd].astype(F32).T.astype(BF16) for h in range(nh)]
    a_scr[...] = jnp.zeros(a_scr.shape, F32)
    u = u_ref[...]

    def pair(sb0, cs, bound):
        start = pl.multiple_of(sb0 * bc, bc)
        k = k_ref[pl.ds(start, 2 * bc), :]
        zs = [_dot(k[:, h * hd:(h + 1) * hd], qts[h]) for h in range(nh)]
        if bound is not None:
            valid = start + lax.broadcasted_iota(jnp.int32, (2 * bc, bq), 0) < bound
        out = []
        for h in range(nh):
            z = zs[h]
            sp = jnp.log2(1.0 + jnp.exp2(-jnp.abs(z)))
            log_beta = jnp.minimum(z, 0.0) - sp
            log_1m = log_beta - z
            if bound is not None:
                log_1m = jnp.where(valid, log_1m, 0.0)
            l16 = log_1m.astype(BF16)
            c = cs[h]
            tails = [None, None]
            for sb in (1, 0):
                blk = l16[sb * bc:(sb + 1) * bc, :]
                t = _dot(u, blk) + c
                tails[sb] = t
                c = t[0:1, :] + blk[0:1, :].astype(F32)
            a = jnp.exp2(log_beta + jnp.concatenate(tails, axis=0))
            if bound is not None:
                a = jnp.where(valid, a, 0.0)
            a16 = a.astype(BF16)
            a_scr[h] += _dot(vt_scr[h, sb0], a16[:bc, :]) + _dot(vt_scr[h, sb0 + 1], a16[bc:, :])
            out.append(c)
        return tuple(out)

    def live(cs):
        m = cs[0]
        for c in cs[1:]:
            m = jnp.maximum(m, c)
        return jnp.max(m) > SB_ZERO_LOG2

    first = jnp.maximum(qi - 1, 0)
    qry = qi * bq + lax.broadcasted_iota(jnp.int32, (1, bq), 1)
    cs = pair(first, (jnp.zeros((1, bq), F32),) * nh, qry)

    n_pairs = first // 2

    def more(state):
        t, cc = state
        return jnp.logical_and(t < n_pairs, live(cc))

    def walk(state):
        t, cc = state
        return t + 1, pair(first - 2 - 2 * t, cc, None)

    t_end, cs = lax.while_loop(more, walk, (jnp.int32(0), cs))

    @pl.when(jnp.logical_and(jnp.logical_and(first % 2 == 1, t_end == n_pairs), live(cs)))
    def _():
        pair(0, cs, jnp.full((1, bq), bc, jnp.int32))

    for h in range(nh):
        o_ref[:, h * hd:(h + 1) * hd] = a_scr[h].T.astype(o_ref.dtype)


def _sb_attn(z, batch, seq, *, bq=256, nh=4):
    nq = seq // bq
    hd = SB_HEAD_DIM
    w = nh * hd
    r = lax.broadcasted_iota(jnp.int32, (bq, bq), 0)
    c = lax.broadcasted_iota(jnp.int32, (bq, bq), 1)
    u = (c > r).astype(BF16)
    return pl.pallas_call(
        functools.partial(_sb_attn_kernel, bq=bq, nh=nh),
        out_shape=jax.ShapeDtypeStruct((batch * seq, SB_WIDTH), BF16),
        grid=(batch, SB_HEADS // nh, nq),
        in_specs=[
            pl.BlockSpec((bq, w), lambda b, h, i: (b * nq + i, COL_QB // w + h)),
            pl.BlockSpec((seq, w), lambda b, h, i: (b, COL_KB // w + h), pipeline_mode=pl.Buffered(1)),
            pl.BlockSpec((seq, w), lambda b, h, i: (b, COL_VB // w + h), pipeline_mode=pl.Buffered(1)),
            pl.BlockSpec((bq, bq), lambda b, h, i: (0, 0)),
        ],
        out_specs=pl.BlockSpec((bq, w), lambda b, h, i: (b * nq + i, h)),
        scratch_shapes=[pltpu.VMEM((nh, seq // bq, hd, bq), BF16), pltpu.VMEM((nh, hd, bq), F32)],
        compiler_params=_params(("arbitrary", "arbitrary", "arbitrary")),
        name="sb_attn",
    )(z, z, z, u)


def _sigmoid(x):
    return 1.0 / (1.0 + jnp.exp(-x))


def _merge_kernel(oa_ref, ob_ref, ga_ref, gb_ref, wa_ref, wb_ref, o_ref):
    ya = _dot(oa_ref[...], wa_ref[...])
    yb = _dot(ob_ref[...], wb_ref[...])
    ga = _sigmoid(ga_ref[...].astype(F32))
    gb = _sigmoid(gb_ref[...].astype(F32))
    o_ref[...] = (ga * ya + gb * yb).astype(o_ref.dtype)


def _merge(oa, ob, z, wa, wb, *, bm=1024, bn=1024):
    m = oa.shape[0]
    return pl.pallas_call(
        _merge_kernel,
        out_shape=jax.ShapeDtypeStruct((m, D_MODEL), BF16),
        grid=(m // bm, D_MODEL // bn),
        in_specs=[
            pl.BlockSpec((bm, DA_WIDTH), lambda i, j: (i, 0)),
            pl.BlockSpec((bm, SB_WIDTH), lambda i, j: (i, 0)),
            pl.BlockSpec((bm, bn), lambda i, j: (i, COL_GA // bn + j)),
            pl.BlockSpec((bm, bn), lambda i, j: (i, COL_GB // bn + j)),
            pl.BlockSpec((DA_WIDTH, bn), lambda i, j: (0, j)),
            pl.BlockSpec((SB_WIDTH, bn), lambda i, j: (0, j)),
        ],
        out_specs=pl.BlockSpec((bm, bn), lambda i, j: (i, j)),
        compiler_params=_params(("parallel", "parallel")),
        name="merge",
    )(oa, ob, z, z, wa, wb)


def _out_proj_kernel(a_ref, w_ref, x_ref, o_ref):
    o_ref[...] = x_ref[...] + _dot(a_ref[...], w_ref[...])


def _out_proj(a, w, x2d, *, bm=1024, bn=1024):
    m, k = a.shape
    n = w.shape[1]
    return pl.pallas_call(
        _out_proj_kernel,
        out_shape=jax.ShapeDtypeStruct((m, n), F32),
        grid=(m // bm, n // bn),
        in_specs=[
            pl.BlockSpec((bm, k), lambda i, j: (i, 0)),
            pl.BlockSpec((k, bn), lambda i, j: (0, j)),
            pl.BlockSpec((bm, bn), lambda i, j: (i, j)),
        ],
        out_specs=pl.BlockSpec((bm, bn), lambda i, j: (i, j)),
        compiler_params=_params(("parallel", "parallel")),
        name="out_proj",
    )(a, w, x2d)


def _mem_kv_kernel(x_ref, g_ref, w_ref, o_ref):
    h = _rms(x_ref[...], g_ref[...]).astype(BF16)
    o_ref[...] = _dot(h, w_ref[...]).astype(o_ref.dtype)


def _mem_kv(mem2d, g, w, *, bm=256):
    m, d = mem2d.shape
    n = w.shape[1]
    return pl.pallas_call(
        _mem_kv_kernel,
        out_shape=jax.ShapeDtypeStruct((m, n), BF16),
        grid=(m // bm,),
        in_specs=[
            pl.BlockSpec((bm, d), lambda i: (i, 0)),
            pl.BlockSpec((1, d), lambda i: (0, 0)),
            pl.BlockSpec((d, n), lambda i: (0, 0)),
        ],
        out_specs=pl.BlockSpec((bm, n), lambda i: (i, 0)),
        compiler_params=_params(("parallel",)),
        name="mem_kv",
    )(mem2d, g.reshape(1, d), w)


def _xattn_kernel(x_ref, g_ref, wq_ref, kv_ref, wo_ref, o_ref, oh_scr):
    x = x_ref[...]
    h = _rms(x, g_ref[...]).astype(BF16)
    q = (_dot(h, wq_ref[...]) * (XA_HEAD_DIM ** -0.5)).astype(BF16)
    for hh in range(XA_HEADS):
        lo = hh * XA_HEAD_DIM
        k = kv_ref[:, lo:lo + XA_HEAD_DIM]
        v = kv_ref[:, XA_WIDTH + lo:XA_WIDTH + lo + XA_HEAD_DIM]
        s = _dot_nt(q[:, lo:lo + XA_HEAD_DIM], k)
        p = jnp.exp(s - jnp.max(s, axis=-1, keepdims=True))
        p = p / jnp.sum(p, axis=-1, keepdims=True)
        oh_scr[:, lo:lo + XA_HEAD_DIM] = _dot(p.astype(BF16), v).astype(BF16)
    o_ref[...] = x + _dot(oh_scr[...], wo_ref[...])


def _xattn(x2d, g, wq, kv, wo, seq, *, bm=512):
    m, d = x2d.shape
    nseq = seq // bm
    return pl.pallas_call(
        _xattn_kernel,
        out_shape=jax.ShapeDtypeStruct((m, d), F32),
        grid=(m // bm,),
        in_specs=[
            pl.BlockSpec((bm, d), lambda i: (i, 0)),
            pl.BlockSpec((1, d), lambda i: (0, 0)),
            pl.BlockSpec((d, XA_WIDTH), lambda i: (0, 0)),
            pl.BlockSpec((N_MEM, 2 * XA_WIDTH), lambda i: (i // nseq, 0)),
            pl.BlockSpec((XA_WIDTH, d), lambda i: (0, 0)),
        ],
        out_specs=pl.BlockSpec((bm, d), lambda i: (i, 0)),
        scratch_shapes=[pltpu.VMEM((bm, XA_WIDTH), BF16)],
        compiler_params=_params(("parallel",)),
        name="xattn",
    )(x2d, g.reshape(1, d), wq, kv, wo)


def _ffn_kernel(x_ref, halo_ref, g_ref, wg_ref, wv_ref, cwg_ref, cwv_ref, cbg_ref, cbv_ref,
                wd_ref, fg_ref, o_ref, h_scr, *, nseq, final):
    i = pl.program_id(0)
    f = pl.program_id(1)
    pad = SUBLANES

    @pl.when(f == 0)
    def _():
        g = g_ref[...]
        h_scr[pad:, :] = _rms(x_ref[...], g).astype(BF16)
        keep = (i % nseq != 0).astype(F32)
        h_scr[:pad, :] = (_rms(halo_ref[...], g) * keep).astype(BF16)
        o_ref[...] = jnp.zeros(o_ref.shape, F32)

    h = h_scr[...]

    def conv(w_ref, cw_ref, cb_ref):
        u = _dot(h, w_ref[...])
        cw = cw_ref[...]
        y = (cw[2:3, :] * u + cw[1:2, :] * pltpu.roll(u, 1, 0) + cw[0:1, :] * pltpu.roll(u, 2, 0))
        return y[pad:, :] + cb_ref[...]

    gate = conv(wg_ref, cwg_ref, cbg_ref)
    val = conv(wv_ref, cwv_ref, cbv_ref)
    act = (gate * _sigmoid(gate) * val).astype(BF16)
    o_ref[...] += _dot(act, wd_ref[...])

    @pl.when(f == pl.num_programs(1) - 1)
    def _():
        y = x_ref[...] + o_ref[...]
        o_ref[...] = _rms(y, fg_ref[...]) if final else y


def _ffn(x2d, g, w_up, conv_w, conv_b, w_down, final_g, seq, *, final, bm=1024, bf=512):
    m, d = x2d.shape
    nf = D_FF // bf
    nseq = seq // bm
    hb = bm // SUBLANES
    return pl.pallas_call(
        functools.partial(_ffn_kernel, nseq=nseq, final=final),
        out_shape=jax.ShapeDtypeStruct((m, d), F32),
        grid=(m // bm, nf),
        in_specs=[
            pl.BlockSpec((bm, d), lambda i, f: (i, 0), pipeline_mode=pl.Buffered(1)),
            pl.BlockSpec((SUBLANES, d), lambda i, f: (jnp.maximum(i * hb - 1, 0), 0)),
            pl.BlockSpec((1, d), lambda i, f: (0, 0)),
            pl.BlockSpec((d, bf), lambda i, f: (0, f)),
            pl.BlockSpec((d, bf), lambda i, f: (0, nf + f)),
            pl.BlockSpec((CONV_WIDTH, bf), lambda i, f: (0, f)),
            pl.BlockSpec((CONV_WIDTH, bf), lambda i, f: (0, nf + f)),
            pl.BlockSpec((1, bf), lambda i, f: (0, f)),
            pl.BlockSpec((1, bf), lambda i, f: (0, nf + f)),
            pl.BlockSpec((bf, d), lambda i, f: (f, 0)),
            pl.BlockSpec((1, d), lambda i, f: (0, 0)),
        ],
        out_specs=pl.BlockSpec((bm, d), lambda i, f: (i, 0)),
        scratch_shapes=[pltpu.VMEM((bm + SUBLANES, d), BF16)],
        compiler_params=_params(("parallel", "arbitrary"), VMEM_LIMIT_FFN),
        name="ffn",
    )(x2d, x2d, g.reshape(1, d), w_up, w_up, conv_w, conv_w,
      conv_b.reshape(1, -1), conv_b.reshape(1, -1), w_down, final_g.reshape(1, d))


def kernel(x, mem, norm_mix_g, w_in, lambda_q1, lambda_k1, lambda_q2, lambda_k2, da_subln_g,
           w_proj_a, w_proj_b, w_out, norm_x_g, norm_mem_g, w_xq, w_xkv, w_xo, norm_ffn_g,
           w_up, conv_w, conv_b, w_down, final_norm_g):
    batch, seq, d = x.shape
    depth = w_in.shape[0]
    bf = lambda a: a.astype(BF16)
    x2d = x.reshape(batch * seq, d)
    mem2d = mem.reshape(batch * mem.shape[1], d)
    for l in range(depth):
        lam_init = 0.8 - 0.6 * math.exp(-0.3 * l)
        z = _in_proj(x2d, norm_mix_g[l], bf(w_in[l]), seq)
        oa = _diff_attn_pairs(z, lambda_q1[l], lambda_k1[l], lambda_q2[l], lambda_k2[l],
                        da_subln_g[l], batch, seq, lam_init)
        ob = _sb_attn(z, batch, seq)
        merged = _merge(oa, ob, z, bf(w_proj_a[l]), bf(w_proj_b[l]))
        x2d = _out_proj(merged, bf(w_out[l]), x2d)
        kv = _mem_kv(mem2d, norm_mem_g[l], bf(w_xkv[l]))
        x2d = _xattn(x2d, norm_x_g[l], bf(w_xq[l]), kv, bf(w_xo[l]), seq)
        x2d = _ffn(x2d, norm_ffn_g[l], bf(w_up[l]), conv_w[l], conv_b[l], bf(w_down[l]),
                   final_norm_g, seq, final=(l == depth - 1))
    return x2d.reshape(batch, seq, d)
```

```python
import functools
import math

import jax
import jax.numpy as jnp
from jax import lax
from jax.experimental import pallas as pl
from jax.experimental.pallas import tpu as pltpu

F32 = jnp.float32
BF16 = jnp.bfloat16

D_MODEL = 2048
N_MEM = 256
DA_HEAD_DIM = 64
DA_WIDTH = D_MODEL // 2
DA_HEADS = DA_WIDTH // (2 * DA_HEAD_DIM)
DA_ROT_DIM = DA_HEAD_DIM // 4
SB_HEAD_DIM = 128
SB_WIDTH = D_MODEL // 2
SB_HEADS = SB_WIDTH // SB_HEAD_DIM
XA_HEADS = 4
XA_HEAD_DIM = 128
XA_WIDTH = XA_HEADS * XA_HEAD_DIM
D_FF = 256 * ((8 * D_MODEL // 3 + 255) // 256)
CONV_WIDTH = 3
ROPE_THETA = 500000.0
EPS = 1e-6
N_IN = 3 * DA_WIDTH + 3 * SB_WIDTH + 2 * D_MODEL

LANES = 128
SUBLANES = 8
BF16_ROWS = 16
NEG_BIG = -1e30
LOG2E = math.log2(math.e)
SB_ZERO_LOG2 = -160.0
FROZEN_MAX_RISE = 64.0
VMEM_LIMIT = 48 * 1024 * 1024
VMEM_LIMIT_BIG = 60 * 1024 * 1024

COL_QA, COL_KA, COL_VA = 0, DA_WIDTH, 2 * DA_WIDTH
COL_QB, COL_KB, COL_VB = 3 * DA_WIDTH, 3 * DA_WIDTH + SB_WIDTH, 3 * DA_WIDTH + 2 * SB_WIDTH
COL_GA = 3 * DA_WIDTH + 3 * SB_WIDTH
COL_GB = COL_GA + D_MODEL


def _params(sem, vmem_limit=VMEM_LIMIT):
    return pltpu.CompilerParams(dimension_semantics=sem, vmem_limit_bytes=vmem_limit)


def _rms(x, g):
    return x * lax.rsqrt(jnp.mean(x * x, axis=-1, keepdims=True) + EPS) * g


def _dot(a, b):
    return jnp.dot(a, b, preferred_element_type=F32)


def _dot_nt(a, b):
    return lax.dot_general(a, b, (((1,), (1,)), ((), ())), preferred_element_type=F32)


def _in_proj_kernel(x_ref, g_ref, w_ref, cos_ref, sp_ref, sm_ref, o_ref, h_scr, *, bn):
    j = pl.program_id(1)

    @pl.when(j == 0)
    def _():
        h_scr[...] = _rms(x_ref[...], g_ref[...]).astype(BF16)

    acc = _dot(h_scr[...], w_ref[...])
    col = j * bn
    is_qa = col < COL_KA
    is_qb = jnp.logical_and(col >= COL_QB, col < COL_KB)
    scale = jnp.where(is_qa, LOG2E * DA_HEAD_DIM ** -0.5,
                      jnp.where(is_qb, LOG2E * SB_HEAD_DIM ** -0.5, 1.0))
    acc = acc * scale.astype(F32)

    @pl.when(col < COL_VA)
    def _():
        cos, sp, sm = cos_ref[...], sp_ref[...], sm_ref[...]
        for c in range(bn // LANES):
            a = acc[:, c * LANES:(c + 1) * LANES]
            r = a * cos + pltpu.roll(a, DA_ROT_DIM // 2, 1) * sp \
                + pltpu.roll(a, LANES - DA_ROT_DIM // 2, 1) * sm
            o_ref[:, c * LANES:(c + 1) * LANES] = r.astype(o_ref.dtype)

    @pl.when(col >= COL_VA)
    def _():
        o_ref[...] = acc.astype(o_ref.dtype)


def _rope_lane_tables(seq):
    half = DA_ROT_DIM // 2
    inv = ROPE_THETA ** (-jnp.arange(0, DA_ROT_DIM, 2, dtype=F32) / DA_ROT_DIM)
    ang = jnp.arange(seq, dtype=F32)[:, None] * inv[None, :]
    cos, sin = jnp.cos(ang), jnp.sin(ang)
    lane = jnp.arange(LANES) % DA_HEAD_DIM
    idx = lane % half
    lo = (lane < half)[None, :]
    hi = jnp.logical_and(lane >= half, lane < 2 * half)[None, :]
    cos_t = jnp.where(jnp.logical_or(lo, hi), cos[:, idx], 1.0)
    sp_t = jnp.where(hi, sin[:, idx], 0.0)
    sm_t = jnp.where(lo, -sin[:, idx], 0.0)
    return cos_t.astype(F32), sp_t.astype(F32), sm_t.astype(F32)


def _in_proj(x2d, g, w, seq, *, bm=1024, bn=1024):
    m, d = x2d.shape
    n = w.shape[1]
    cos_t, sp_t, sm_t = _rope_lane_tables(seq)
    nseq = seq // bm
    tab_spec = pl.BlockSpec((bm, LANES), lambda i, j: (i % nseq, 0))
    return pl.pallas_call(
        functools.partial(_in_proj_kernel, bn=bn),
        out_shape=jax.ShapeDtypeStruct((m, n), BF16),
        grid=(m // bm, n // bn),
        in_specs=[
            pl.BlockSpec((bm, d), lambda i, j: (i, 0)),
            pl.BlockSpec((1, d), lambda i, j: (0, 0)),
            pl.BlockSpec((d, bn), lambda i, j: (0, j)),
            tab_spec, tab_spec, tab_spec,
        ],
        out_specs=pl.BlockSpec((bm, bn), lambda i, j: (i, j)),
        scratch_shapes=[pltpu.VMEM((bm, d), BF16)],
        compiler_params=_params(("parallel", "arbitrary")),
        name="in_proj",
    )(x2d, g.reshape(1, d), w, cos_t, sp_t, sm_t)


def _transpose_values(v_ref, vt_scr, bk):
    hd = v_ref.shape[1]
    extra = vt_scr.shape[1] - hd

    def body(kb, carry):
        start = pl.multiple_of(kb * bk, bk)
        vt_scr[kb, :hd, :] = v_ref[pl.ds(start, bk), :].astype(F32).T.astype(BF16)
        if extra:
            vt_scr[kb, hd:, :] = jnp.ones((extra, bk), BF16)
        return carry
    lax.fori_loop(0, vt_scr.shape[0], body, 0)


def _diff_head_pair_kernel(q_ref, k_ref, v_ref, lq1_ref, lk1_ref, lq2_ref, lk2_ref, g_ref, o_ref,
                           vt_scr, a_scr, *, bq, bk, nh, lam_init):
    qi = pl.program_id(2)
    hd = 2 * DA_HEAD_DIM
    nmap = 2 * nh

    @pl.when(qi == 0)
    def _():
        for h in range(nh):
            _transpose_values(v_ref.at[:, h * hd:(h + 1) * hd], vt_scr.at[h], bk)

    qts = []
    for h in range(nh):
        qt = q_ref[:, h * hd:(h + 1) * hd].astype(F32).T
        feat = lax.broadcasted_iota(jnp.int32, qt.shape, 0)
        qts.append(jnp.where(feat < DA_HEAD_DIM, qt, 0.0).astype(BF16))
        qts.append(jnp.where(feat >= DA_HEAD_DIM, qt, 0.0).astype(BF16))
    def scores(kb, nkeys):
        nk = bk if nkeys is None else nkeys
        start = pl.multiple_of(kb * bk, bk)
        k = k_ref[pl.ds(start, nk), :]
        ss = [_dot(k[:, (i // 2) * hd:(i // 2 + 1) * hd], qts[i]) for i in range(nmap)]
        if nkeys is not None:
            key = start + lax.broadcasted_iota(jnp.int32, (nk, bq), 0)
            qry = qi * bq + lax.broadcasted_iota(jnp.int32, (nk, bq), 1)
            keep = key <= qry
            ss = [jnp.where(keep, s, NEG_BIG) for s in ss]
        return nk, ss


    def online(kb, ms, nkeys=None):
        nk, ss = scores(kb, nkeys)
        out = []
        for i in range(nmap):
            m_new = jnp.maximum(ms[i], jnp.max(ss[i], axis=0, keepdims=True))
            alpha = jnp.exp2(ms[i] - m_new)
            p = jnp.exp2(ss[i] - m_new).astype(BF16)
            a_scr[i] = alpha * a_scr[i] + _dot(vt_scr[i // 2, kb, :, :nk], p)
            out.append(m_new)
        return tuple(out)

    def frozen(kb, carry, nkeys=None):
        ms, gs = carry
        nk, ss = scores(kb, nkeys)
        out = []
        for i in range(nmap):
            p = jnp.exp2(ss[i] - ms[i]).astype(BF16)
            a_scr[i] += _dot(vt_scr[i // 2, kb, :, :nk], p)
            out.append(jnp.maximum(gs[i], jnp.max(ss[i], axis=0, keepdims=True)))
        return ms, tuple(out)

    neg = jnp.full((1, bq), NEG_BIG, F32)
    n_full = (qi * bq) // bk
    per = bk // bq
    diag = lambda fn: [functools.partial(fn, n_full, nkeys=(r + 1) * bq) for r in range(per)]

    def run_online():
        a_scr[...] = jnp.zeros(a_scr.shape, F32)
        ms = lax.fori_loop(0, n_full, online, (neg,) * nmap)
        lax.switch(qi % per, diag(online), ms)

    def run_frozen():
        a_scr[...] = jnp.zeros(a_scr.shape, F32)
        ms = lax.switch(qi % per, diag(online), (neg,) * nmap)
        ms, gs = lax.fori_loop(0, n_full, frozen, (ms, ms))
        rise = gs[0] - ms[0]
        for i in range(1, nmap):
            rise = jnp.maximum(rise, gs[i] - ms[i])
        return jnp.max(rise) <= FROZEN_MAX_RISE

    @pl.when(jnp.logical_not(run_frozen()))
    def _():
        run_online()

    lam = (jnp.exp(jnp.sum(lq1_ref[...] * lk1_ref[...], axis=-1, keepdims=True))
           - jnp.exp(jnp.sum(lq2_ref[...] * lk2_ref[...], axis=-1, keepdims=True)) + lam_init)
    for h in range(nh):
        a1, a2 = a_scr[2 * h], a_scr[2 * h + 1]
        ot = a1[:hd, :] / a1[hd:hd + 1, :] - lam * (a2[:hd, :] / a2[hd:hd + 1, :])
        o = _rms(ot.T, g_ref[...]) * (1.0 - lam_init)
        o_ref[:, h * hd:(h + 1) * hd] = o.astype(o_ref.dtype)


def _diff_attn_pairs(z, lq1, lk1, lq2, lk2, subln_g, batch, seq, lam_init, *, bq=256, bk=1024, nh=4):
    nq = seq // bq
    hd = 2 * DA_HEAD_DIM
    w = nh * hd
    vec = lambda a: a.reshape(1, -1).astype(F32)
    small = lambda n: pl.BlockSpec((1, n), lambda b, h, i: (0, 0))
    return pl.pallas_call(
        functools.partial(_diff_head_pair_kernel, bq=bq, bk=bk, nh=nh, lam_init=lam_init),
        out_shape=jax.ShapeDtypeStruct((batch * seq, DA_WIDTH), BF16),
        grid=(batch, DA_HEADS // nh, nq),
        in_specs=[
            pl.BlockSpec((bq, w), lambda b, h, i: (b * nq + i, COL_QA // w + h)),
            pl.BlockSpec((seq, w), lambda b, h, i: (b, COL_KA // w + h), pipeline_mode=pl.Buffered(1)),
            pl.BlockSpec((seq, w), lambda b, h, i: (b, COL_VA // w + h), pipeline_mode=pl.Buffered(1)),
            small(DA_HEAD_DIM), small(DA_HEAD_DIM), small(DA_HEAD_DIM), small(DA_HEAD_DIM),
            small(hd),
        ],
        out_specs=pl.BlockSpec((bq, w), lambda b, h, i: (b * nq + i, h)),
        scratch_shapes=[
            pltpu.VMEM((nh, seq // bk, hd + BF16_ROWS, bk), BF16),
            pltpu.VMEM((2 * nh, hd + BF16_ROWS, bq), F32),
        ],
        compiler_params=_params(("arbitrary", "arbitrary", "arbitrary")),
        name="diff_attn",
    )(z, z, z, vec(lq1), vec(lk1), vec(lq2), vec(lk2), vec(subln_g))


def _sb_attn_kernel(q_ref, k_ref, v_ref, u_ref, o_ref, vt_scr, a_scr, *, bq, nh):
    qi = pl.program_id(2)
    hd = SB_HEAD_DIM
    bc = bq

    @pl.when(qi == 0)
    def _():
        for h in range(nh):
            _transpose_values(v_ref.at[:, h * hd:(h + 1) * hd], vt_scr.at[h], bc)

    qts = [q_ref[:, h * hd:(h + 1) * hd].astype(F32).T.astype(BF16) for h in range(nh)]
    a_scr[...] = jnp.zeros(a_scr.shape, F32)
    u = u_ref[...]

    def pair(sb0, cs, bound):
        start = pl.multiple_of(sb0 * bc, bc)
        k = k_ref[pl.ds(start, 2 * bc), :]
        zs = [_dot(k[:, h * hd:(h + 1) * hd], qts[h]) for h in range(nh)]
        if bound is not None:
            valid = start + lax.broadcasted_iota(jnp.int32, (2 * bc, bq), 0) < bound
        out = []
        for h in range(nh):
            z = zs[h]
            sp = jnp.log2(1.0 + jnp.exp2(-jnp.abs(z)))
            log_beta = jnp.minimum(z, 0.0) - sp
            log_1m = log_beta - z
            if bound is not None:
                log_1m = jnp.where(valid, log_1m, 0.0)
            l16 = log_1m.astype(BF16)
            c = cs[h]
            tails = [None, None]
            for sb in (1, 0):
                blk = l16[sb * bc:(sb + 1) * bc, :]
                t = _dot(u, blk) + c
                tails[sb] = t
                c = t[0:1, :] + blk[0:1, :].astype(F32)
            a = jnp.exp2(log_beta + jnp.concatenate(tails, axis=0))
            if bound is not None:
                a = jnp.where(valid, a, 0.0)
            a16 = a.astype(BF16)
            a_scr[h] += _dot(vt_scr[h, sb0], a16[:bc, :]) + _dot(vt_scr[h, sb0 + 1], a16[bc:, :])
            out.append(c)
        return tuple(out)

    def live(cs):
        m = cs[0]
        for c in cs[1:]:
            m = jnp.maximum(m, c)
        return jnp.max(m) > SB_ZERO_LOG2

    first = jnp.maximum(qi - 1, 0)
    qry = qi * bq + lax.broadcasted_iota(jnp.int32, (1, bq), 1)
    cs = pair(first, (jnp.zeros((1, bq), F32),) * nh, qry)

    n_pairs = first // 2

    def more(state):
        t, cc = state
        return jnp.logical_and(t < n_pairs, live(cc))

    def walk(state):
        t, cc = state
        return t + 1, pair(first - 2 - 2 * t, cc, None)

    t_end, cs = lax.while_loop(more, walk, (jnp.int32(0), cs))

    @pl.when(jnp.logical_and(jnp.logical_and(first % 2 == 1, t_end == n_pairs), live(cs)))
    def _():
        pair(0, cs, jnp.full((1, bq), bc, jnp.int32))

    for h in range(nh):
        o_ref[:, h * hd:(h + 1) * hd] = a_scr[h].T.astype(o_ref.dtype)


def _sb_attn(z, batch, seq, *, bq=256, nh=4):
    nq = seq // bq
    hd = SB_HEAD_DIM
    w = nh * hd
    r = lax.broadcasted_iota(jnp.int32, (bq, bq), 0)
    c = lax.broadcasted_iota(jnp.int32, (bq, bq), 1)
    u = (c > r).astype(BF16)
    return pl.pallas_call(
        functools.partial(_sb_attn_kernel, bq=bq, nh=nh),
        out_shape=jax.ShapeDtypeStruct((batch * seq, SB_WIDTH), BF16),
        grid=(batch, SB_HEADS // nh, nq),
        in_specs=[
            pl.BlockSpec((bq, w), lambda b, h, i: (b * nq + i, COL_QB // w + h)),
            pl.BlockSpec((seq, w), lambda b, h, i: (b, COL_KB // w + h), pipeline_mode=pl.Buffered(1)),
            pl.BlockSpec((seq, w), lambda b, h, i: (b, COL_VB // w + h), pipeline_mode=pl.Buffered(1)),
            pl.BlockSpec((bq, bq), lambda b, h, i: (0, 0)),
        ],
        out_specs=pl.BlockSpec((bq, w), lambda b, h, i: (b * nq + i, h)),
        scratch_shapes=[pltpu.VMEM((nh, seq // bq, hd, bq), BF16), pltpu.VMEM((nh, hd, bq), F32)],
        compiler_params=_params(("arbitrary", "arbitrary", "arbitrary")),
        name="sb_attn",
    )(z, z, z, u)


def _sigmoid(x):
    return 1.0 / (1.0 + jnp.exp(-x))


def _merge_out_kernel(oa_ref, ob_ref, ga_ref, gb_ref, wa_ref, wb_ref, wo_ref, x_ref, o_ref):
    ya = _dot(oa_ref[...], wa_ref[...])
    yb = _dot(ob_ref[...], wb_ref[...])
    ga = _sigmoid(ga_ref[...].astype(F32))
    gb = _sigmoid(gb_ref[...].astype(F32))
    merged = (ga * ya + gb * yb).astype(BF16)
    o_ref[...] = x_ref[...] + _dot(merged, wo_ref[...])


def _merge_out(oa, ob, z, wa, wb, wo, x2d, *, bm=512):
    m, d = x2d.shape
    resident = lambda shape: pl.BlockSpec(shape, lambda i: (0, 0), pipeline_mode=pl.Buffered(1))
    return pl.pallas_call(
        _merge_out_kernel,
        out_shape=jax.ShapeDtypeStruct((m, d), F32),
        grid=(m // bm,),
        in_specs=[
            pl.BlockSpec((bm, DA_WIDTH), lambda i: (i, 0)),
            pl.BlockSpec((bm, SB_WIDTH), lambda i: (i, 0)),
            pl.BlockSpec((bm, d), lambda i: (i, COL_GA // d)),
            pl.BlockSpec((bm, d), lambda i: (i, COL_GB // d)),
            resident((DA_WIDTH, d)), resident((SB_WIDTH, d)), resident((d, d)),
            pl.BlockSpec((bm, d), lambda i: (i, 0)),
        ],
        out_specs=pl.BlockSpec((bm, d), lambda i: (i, 0)),
        compiler_params=_params(("parallel",), VMEM_LIMIT_BIG),
        name="merge_out",
    )(oa, ob, z, z, wa, wb, wo, x2d)


def _mem_kv_kernel(x_ref, g_ref, w_ref, o_ref):
    h = _rms(x_ref[...], g_ref[...]).astype(BF16)
    o_ref[...] = _dot(h, w_ref[...]).astype(o_ref.dtype)


def _mem_kv(mem2d, g, w, *, bm=256):
    m, d = mem2d.shape
    n = w.shape[1]
    return pl.pallas_call(
        _mem_kv_kernel,
        out_shape=jax.ShapeDtypeStruct((m, n), BF16),
        grid=(m // bm,),
        in_specs=[
            pl.BlockSpec((bm, d), lambda i: (i, 0)),
            pl.BlockSpec((1, d), lambda i: (0, 0)),
            pl.BlockSpec((d, n), lambda i: (0, 0)),
        ],
        out_specs=pl.BlockSpec((bm, n), lambda i: (i, 0)),
        compiler_params=_params(("parallel",)),
        name="mem_kv",
    )(mem2d, g.reshape(1, d), w)


def _xattn_kernel(x_ref, g_ref, wq_ref, kv_ref, wo_ref, o_ref, oh_scr):
    x = x_ref[...]
    h = _rms(x, g_ref[...]).astype(BF16)
    q = (_dot(h, wq_ref[...]) * (XA_HEAD_DIM ** -0.5)).astype(BF16)
    for hh in range(XA_HEADS):
        lo = hh * XA_HEAD_DIM
        k = kv_ref[:, lo:lo + XA_HEAD_DIM]
        v = kv_ref[:, XA_WIDTH + lo:XA_WIDTH + lo + XA_HEAD_DIM]
        s = _dot_nt(q[:, lo:lo + XA_HEAD_DIM], k)
        p = jnp.exp(s - jnp.max(s, axis=-1, keepdims=True))
        p = p / jnp.sum(p, axis=-1, keepdims=True)
        oh_scr[:, lo:lo + XA_HEAD_DIM] = _dot(p.astype(BF16), v).astype(BF16)
    o_ref[...] = x + _dot(oh_scr[...], wo_ref[...])


def _xattn(x2d, g, wq, kv, wo, seq, *, bm=1024):
    m, d = x2d.shape
    nseq = seq // bm
    return pl.pallas_call(
        _xattn_kernel,
        out_shape=jax.ShapeDtypeStruct((m, d), F32),
        grid=(m // bm,),
        in_specs=[
            pl.BlockSpec((bm, d), lambda i: (i, 0)),
            pl.BlockSpec((1, d), lambda i: (0, 0)),
            pl.BlockSpec((d, XA_WIDTH), lambda i: (0, 0)),
            pl.BlockSpec((N_MEM, 2 * XA_WIDTH), lambda i: (i // nseq, 0)),
            pl.BlockSpec((XA_WIDTH, d), lambda i: (0, 0)),
        ],
        out_specs=pl.BlockSpec((bm, d), lambda i: (i, 0)),
        scratch_shapes=[pltpu.VMEM((bm, XA_WIDTH), BF16)],
        compiler_params=_params(("parallel",)),
        name="xattn",
    )(x2d, g.reshape(1, d), wq, kv, wo)


def _ffn_kernel(x_ref, halo_ref, g_ref, wg_ref, wv_ref, cwg_ref, cwv_ref, cbg_ref, cbv_ref,
                wd_ref, fg_ref, o_ref, h_scr, *, nseq, final):
    i = pl.program_id(0)
    f = pl.program_id(1)
    pad = SUBLANES

    @pl.when(f == 0)
    def _():
        g = g_ref[...]
        h_scr[pad:, :] = _rms(x_ref[...], g).astype(BF16)
        keep = (i % nseq != 0).astype(F32)
        h_scr[:pad, :] = (_rms(halo_ref[...], g) * keep).astype(BF16)
        o_ref[...] = jnp.zeros(o_ref.shape, F32)

    h = h_scr[...]

    def conv(w_ref, cw_ref, cb_ref):
        u = _dot(h, w_ref[...])
        cw = cw_ref[...]
        y = (cw[2:3, :] * u + cw[1:2, :] * pltpu.roll(u, 1, 0) + cw[0:1, :] * pltpu.roll(u, 2, 0))
        return y[pad:, :] + cb_ref[...]

    gate = conv(wg_ref, cwg_ref, cbg_ref)
    val = conv(wv_ref, cwv_ref, cbv_ref)
    act = (gate * _sigmoid(gate) * val).astype(BF16)
    o_ref[...] += _dot(act, wd_ref[...])

    @pl.when(f == pl.num_programs(1) - 1)
    def _():
        y = x_ref[...] + o_ref[...]
        o_ref[...] = _rms(y, fg_ref[...]) if final else y


def _ffn(x2d, g, w_up, conv_w, conv_b, w_down, final_g, seq, *, final, bm=1024, bf=512):
    m, d = x2d.shape
    nf = D_FF // bf
    nseq = seq // bm
    hb = bm // SUBLANES
    return pl.pallas_call(
        functools.partial(_ffn_kernel, nseq=nseq, final=final),
        out_shape=jax.ShapeDtypeStruct((m, d), F32),
        grid=(m // bm, nf),
        in_specs=[
            pl.BlockSpec((bm, d), lambda i, f: (i, 0), pipeline_mode=pl.Buffered(1)),
            pl.BlockSpec((SUBLANES, d), lambda i, f: (jnp.maximum(i * hb - 1, 0), 0)),
            pl.BlockSpec((1, d), lambda i, f: (0, 0)),
            pl.BlockSpec((d, bf), lambda i, f: (0, f)),
            pl.BlockSpec((d, bf), lambda i, f: (0, nf + f)),
            pl.BlockSpec((CONV_WIDTH, bf), lambda i, f: (0, f)),
            pl.BlockSpec((CONV_WIDTH, bf), lambda i, f: (0, nf + f)),
            pl.BlockSpec((1, bf), lambda i, f: (0, f)),
            pl.BlockSpec((1, bf), lambda i, f: (0, nf + f)),
            pl.BlockSpec((bf, d), lambda i, f: (f, 0)),
            pl.BlockSpec((1, d), lambda i, f: (0, 0)),
        ],
        out_specs=pl.BlockSpec((bm, d), lambda i, f: (i, 0)),
        scratch_shapes=[pltpu.VMEM((bm + SUBLANES, d), BF16)],
        compiler_params=_params(("parallel", "arbitrary"), VMEM_LIMIT_BIG),
        name="ffn",
    )(x2d, x2d, g.reshape(1, d), w_up, w_up, conv_w, conv_w,
      conv_b.reshape(1, -1), conv_b.reshape(1, -1), w_down, final_g.reshape(1, d))


def kernel(x, mem, norm_mix_g, w_in, lambda_q1, lambda_k1, lambda_q2, lambda_k2, da_subln_g,
           w_proj_a, w_proj_b, w_out, norm_x_g, norm_mem_g, w_xq, w_xkv, w_xo, norm_ffn_g,
           w_up, conv_w, conv_b, w_down, final_norm_g):
    batch, seq, d = x.shape
    depth = w_in.shape[0]
    bf = lambda a: a.astype(BF16)
    x2d = x.reshape(batch * seq, d)
    mem2d = mem.reshape(batch * mem.shape[1], d)
    for l in range(depth):
        lam_init = 0.8 - 0.6 * math.exp(-0.3 * l)
        z = _in_proj(x2d, norm_mix_g[l], bf(w_in[l]), seq)
        oa = _diff_attn_pairs(z, lambda_q1[l], lambda_k1[l], lambda_q2[l], lambda_k2[l],
                        da_subln_g[l], batch, seq, lam_init)
        ob = _sb_attn(z, batch, seq)
        x2d = _merge_out(oa, ob, z, bf(w_proj_a[l]), bf(w_proj_b[l]), bf(w_out[l]), x2d)
        kv = _mem_kv(mem2d, norm_mem_g[l], bf(w_xkv[l]))
        x2d = _xattn(x2d, norm_x_g[l], bf(w_xq[l]), kv, bf(w_xo[l]), seq)
        x2d = _ffn(x2d, norm_ffn_g[l], bf(w_up[l]), conv_w[l], conv_b[l], bf(w_down[l]),
                   final_norm_g, seq, final=(l == depth - 1))
    return x2d.reshape(batch, seq, d)
```

```python
import functools
import math

import jax
import jax.numpy as jnp
from jax import lax
from jax.experimental import pallas as pl
from jax.experimental.pallas import tpu as pltpu

F32 = jnp.float32
BF16 = jnp.bfloat16

D_MODEL = 2048
N_MEM = 256
DA_HEAD_DIM = 64
DA_WIDTH = D_MODEL // 2
DA_HEADS = DA_WIDTH // (2 * DA_HEAD_DIM)
DA_ROT_DIM = DA_HEAD_DIM // 4
SB_HEAD_DIM = 128
SB_WIDTH = D_MODEL // 2
SB_HEADS = SB_WIDTH // SB_HEAD_DIM
XA_HEADS = 4
XA_HEAD_DIM = 128
XA_WIDTH = XA_HEADS * XA_HEAD_DIM
D_FF = 256 * ((8 * D_MODEL // 3 + 255) // 256)
CONV_WIDTH = 3
ROPE_THETA = 500000.0
EPS = 1e-6
N_IN = 3 * DA_WIDTH + 3 * SB_WIDTH + 2 * D_MODEL

LANES = 128
SUBLANES = 8
BF16_ROWS = 16
NEG_BIG = -1e30
LOG2E = math.log2(math.e)
SB_ZERO_LOG2 = -160.0
TRANSPOSE_KEYS = 1024
FROZEN_MAX_RISE = 64.0
VMEM_LIMIT = 48 * 1024 * 1024
VMEM_LIMIT_BIG = 60 * 1024 * 1024

COL_QA, COL_KA, COL_VA = 0, DA_WIDTH, 2 * DA_WIDTH
COL_QB, COL_KB, COL_VB = 3 * DA_WIDTH, 3 * DA_WIDTH + SB_WIDTH, 3 * DA_WIDTH + 2 * SB_WIDTH
COL_GA = 3 * DA_WIDTH + 3 * SB_WIDTH
COL_GB = COL_GA + D_MODEL


def _params(sem, vmem_limit=VMEM_LIMIT):
    return pltpu.CompilerParams(dimension_semantics=sem, vmem_limit_bytes=vmem_limit)


def _rms(x, g):
    return x * lax.rsqrt(jnp.mean(x * x, axis=-1, keepdims=True) + EPS) * g


def _dot(a, b):
    return jnp.dot(a, b, preferred_element_type=F32)


def _dot_nt(a, b):
    return lax.dot_general(a, b, (((1,), (1,)), ((), ())), preferred_element_type=F32)


def _in_proj_kernel(x_ref, g_ref, w_ref, cos_ref, sp_ref, sm_ref, o_ref, h_scr, *, bn):
    j = pl.program_id(1)

    @pl.when(j == 0)
    def _():
        h_scr[...] = _rms(x_ref[...], g_ref[...]).astype(BF16)

    acc = _dot(h_scr[...], w_ref[...])
    col = j * bn
    is_qa = col < COL_KA
    is_qb = jnp.logical_and(col >= COL_QB, col < COL_KB)
    scale = jnp.where(is_qa, LOG2E * DA_HEAD_DIM ** -0.5,
                      jnp.where(is_qb, LOG2E * SB_HEAD_DIM ** -0.5, 1.0))
    acc = acc * scale.astype(F32)

    @pl.when(col < COL_VA)
    def _():
        cos, sp, sm = cos_ref[...], sp_ref[...], sm_ref[...]
        for c in range(bn // LANES):
            a = acc[:, c * LANES:(c + 1) * LANES]
            r = a * cos + pltpu.roll(a, DA_ROT_DIM // 2, 1) * sp \
                + pltpu.roll(a, LANES - DA_ROT_DIM // 2, 1) * sm
            o_ref[:, c * LANES:(c + 1) * LANES] = r.astype(o_ref.dtype)

    @pl.when(col >= COL_VA)
    def _():
        o_ref[...] = acc.astype(o_ref.dtype)


def _rope_lane_tables(seq):
    half = DA_ROT_DIM // 2
    inv = ROPE_THETA ** (-jnp.arange(0, DA_ROT_DIM, 2, dtype=F32) / DA_ROT_DIM)
    ang = jnp.arange(seq, dtype=F32)[:, None] * inv[None, :]
    cos, sin = jnp.cos(ang), jnp.sin(ang)
    lane = jnp.arange(LANES) % DA_HEAD_DIM
    idx = lane % half
    lo = (lane < half)[None, :]
    hi = jnp.logical_and(lane >= half, lane < 2 * half)[None, :]
    cos_t = jnp.where(jnp.logical_or(lo, hi), cos[:, idx], 1.0)
    sp_t = jnp.where(hi, sin[:, idx], 0.0)
    sm_t = jnp.where(lo, -sin[:, idx], 0.0)
    return cos_t.astype(F32), sp_t.astype(F32), sm_t.astype(F32)


def _in_proj(x2d, g, w, seq, *, bm=1024, bn=1024):
    m, d = x2d.shape
    n = w.shape[1]
    cos_t, sp_t, sm_t = _rope_lane_tables(seq)
    nseq = seq // bm
    tab_spec = pl.BlockSpec((bm, LANES), lambda i, j: (i % nseq, 0))
    return pl.pallas_call(
        functools.partial(_in_proj_kernel, bn=bn),
        out_shape=jax.ShapeDtypeStruct((m, n), BF16),
        grid=(m // bm, n // bn),
        in_specs=[
            pl.BlockSpec((bm, d), lambda i, j: (i, 0)),
            pl.BlockSpec((1, d), lambda i, j: (0, 0)),
            pl.BlockSpec((d, bn), lambda i, j: (0, j)),
            tab_spec, tab_spec, tab_spec,
        ],
        out_specs=pl.BlockSpec((bm, bn), lambda i, j: (i, j)),
        scratch_shapes=[pltpu.VMEM((bm, d), BF16)],
        compiler_params=_params(("parallel", "arbitrary")),
        name="in_proj",
    )(x2d, g.reshape(1, d), w, cos_t, sp_t, sm_t)


def _transpose_values(v_ref, vt_scr, bk):
    hd = v_ref.shape[1]
    extra = vt_scr.shape[1] - hd
    per = max(1, TRANSPOSE_KEYS // bk)
    chunk = per * bk

    def body(c, carry):
        start = pl.multiple_of(c * chunk, chunk)
        vt = v_ref[pl.ds(start, chunk), :].astype(F32).T.astype(BF16)
        for s in range(per):
            vt_scr[c * per + s, :hd, :] = vt[:, s * bk:(s + 1) * bk]
            if extra:
                vt_scr[c * per + s, hd:, :] = jnp.ones((extra, bk), BF16)
        return carry
    lax.fori_loop(0, vt_scr.shape[0] // per, body, 0)


def _diff_head_pair_kernel(q_ref, k_ref, v_ref, lq1_ref, lk1_ref, lq2_ref, lk2_ref, g_ref, o_ref,
                           vt_scr, a_scr, *, bq, bk, nh, lam_init):
    qi = pl.program_id(2)
    hd = 2 * DA_HEAD_DIM
    nmap = 2 * nh

    @pl.when(qi == 0)
    def _():
        for h in range(nh):
            _transpose_values(v_ref.at[:, h * hd:(h + 1) * hd], vt_scr.at[h], bk)

    qts = []
    for h in range(nh):
        qt = q_ref[:, h * hd:(h + 1) * hd].astype(F32).T
        feat = lax.broadcasted_iota(jnp.int32, qt.shape, 0)
        qts.append(jnp.where(feat < DA_HEAD_DIM, qt, 0.0).astype(BF16))
        qts.append(jnp.where(feat >= DA_HEAD_DIM, qt, 0.0).astype(BF16))
    def scores(start, nk, causal):
        k = k_ref[pl.ds(start, nk), :]
        ss = [_dot(k[:, (i // 2) * hd:(i // 2 + 1) * hd], qts[i]) for i in range(nmap)]
        if causal:
            key = start + lax.broadcasted_iota(jnp.int32, (nk, bq), 0)
            qry = qi * bq + lax.broadcasted_iota(jnp.int32, (nk, bq), 1)
            keep = key <= qry
            ss = [jnp.where(keep, s, NEG_BIG) for s in ss]
        return ss


    def online(kb, ms, nkeys=None):
        nk = bk if nkeys is None else nkeys
        ss = scores(pl.multiple_of(kb * bk, bk), nk, nkeys is not None)
        out = []
        for i in range(nmap):
            m_new = jnp.maximum(ms[i], jnp.max(ss[i], axis=0, keepdims=True))
            alpha = jnp.exp2(ms[i] - m_new)
            p = jnp.exp2(ss[i] - m_new).astype(BF16)
            a_scr[i] = alpha * a_scr[i] + _dot(vt_scr[i // 2, kb, :, :nk], p)
            out.append(m_new)
        return tuple(out)

    def frozen(kb, carry, nkeys=None):
        ms, gs = carry
        nk = bk if nkeys is None else nkeys
        ss = scores(pl.multiple_of(kb * bk, bk), nk, False)
        out = []
        for i in range(nmap):
            p = jnp.exp2(ss[i] - ms[i]).astype(BF16)
            a_scr[i] += _dot(vt_scr[i // 2, kb, :, :nk], p)
            out.append(jnp.maximum(gs[i], jnp.max(ss[i], axis=0, keepdims=True)))
        return ms, tuple(out)

    neg = jnp.full((1, bq), NEG_BIG, F32)
    n_full = (qi * bq) // bk
    per = bk // bq
    def run_online():
        a_scr[...] = jnp.zeros(a_scr.shape, F32)
        ms = lax.fori_loop(0, n_full, online, (neg,) * nmap)
        lax.switch(qi % per, [functools.partial(online, n_full, nkeys=(r + 1) * bq) for r in range(per)], ms)

    def diagonal_first(r):
        start = pl.multiple_of(n_full * bk + r * bq, bq)
        ms = []
        for i, s in enumerate(scores(start, bq, True)):
            m = jnp.max(s, axis=0, keepdims=True)
            a_scr[i] = _dot(vt_scr[i // 2, n_full, :, r * bq:(r + 1) * bq], jnp.exp2(s - m).astype(BF16))
            ms.append(m)
        carry = (tuple(ms), tuple(ms))
        return frozen(n_full, carry, nkeys=r * bq) if r else carry

    def run_frozen():
        carry = lax.switch(qi % per, [functools.partial(diagonal_first, r) for r in range(per)])
        ms, gs = lax.fori_loop(0, n_full, frozen, carry)
        rise = gs[0] - ms[0]
        for i in range(1, nmap):
            rise = jnp.maximum(rise, gs[i] - ms[i])
        return jnp.max(rise) <= FROZEN_MAX_RISE

    @pl.when(jnp.logical_not(run_frozen()))
    def _():
        run_online()

    lam = (jnp.exp(jnp.sum(lq1_ref[...] * lk1_ref[...], axis=-1, keepdims=True))
           - jnp.exp(jnp.sum(lq2_ref[...] * lk2_ref[...], axis=-1, keepdims=True)) + lam_init)
    for h in range(nh):
        a1, a2 = a_scr[2 * h], a_scr[2 * h + 1]
        ot = a1[:hd, :] / a1[hd:hd + 1, :] - lam * (a2[:hd, :] / a2[hd:hd + 1, :])
        o = _rms(ot.T, g_ref[...]) * (1.0 - lam_init)
        o_ref[:, h * hd:(h + 1) * hd] = o.astype(o_ref.dtype)


def _diff_attn_pairs(z, lq1, lk1, lq2, lk2, subln_g, batch, seq, lam_init, *, bq=256, bk=1024, nh=4):
    nq = seq // bq
    hd = 2 * DA_HEAD_DIM
    w = nh * hd
    vec = lambda a: a.reshape(1, -1).astype(F32)
    small = lambda n: pl.BlockSpec((1, n), lambda b, h, i: (0, 0))
    return pl.pallas_call(
        functools.partial(_diff_head_pair_kernel, bq=bq, bk=bk, nh=nh, lam_init=lam_init),
        out_shape=jax.ShapeDtypeStruct((batch * seq, DA_WIDTH), BF16),
        grid=(batch, DA_HEADS // nh, nq),
        in_specs=[
            pl.BlockSpec((bq, w), lambda b, h, i: (b * nq + i, COL_QA // w + h)),
            pl.BlockSpec((seq, w), lambda b, h, i: (b, COL_KA // w + h), pipeline_mode=pl.Buffered(1)),
            pl.BlockSpec((seq, w), lambda b, h, i: (b, COL_VA // w + h), pipeline_mode=pl.Buffered(1)),
            small(DA_HEAD_DIM), small(DA_HEAD_DIM), small(DA_HEAD_DIM), small(DA_HEAD_DIM),
            small(hd),
        ],
        out_specs=pl.BlockSpec((bq, w), lambda b, h, i: (b * nq + i, h)),
        scratch_shapes=[
            pltpu.VMEM((nh, seq // bk, hd + BF16_ROWS, bk), BF16),
            pltpu.VMEM((2 * nh, hd + BF16_ROWS, bq), F32),
        ],
        compiler_params=_params(("arbitrary", "arbitrary", "arbitrary")),
        name="diff_attn",
    )(z, z, z, vec(lq1), vec(lk1), vec(lq2), vec(lk2), vec(subln_g))


def _sb_attn_kernel(q_ref, k_ref, v_ref, u_ref, o_ref, vt_scr, a_scr, *, bq, nh):
    qi = pl.program_id(2)
    hd = SB_HEAD_DIM
    bc = bq

    @pl.when(qi == 0)
    def _():
        for h in range(nh):
            _transpose_values(v_ref.at[:, h * hd:(h + 1) * hd], vt_scr.at[h], bc)

    qts = [q_ref[:, h * hd:(h + 1) * hd].astype(F32).T.astype(BF16) for h in range(nh)]
    a_scr[...] = jnp.zeros(a_scr.shape, F32)
    u = u_ref[...]

    def pair(sb0, cs, bound):
        start = pl.multiple_of(sb0 * bc, bc)
        k = k_ref[pl.ds(start, 2 * bc), :]
        zs = [_dot(k[:, h * hd:(h + 1) * hd], qts[h]) for h in range(nh)]
        if bound is not None:
            valid = start + lax.broadcasted_iota(jnp.int32, (2 * bc, bq), 0) < bound
        out = []
        for h in range(nh):
            z = zs[h]
            sp = jnp.log2(1.0 + jnp.exp2(-jnp.abs(z)))
            log_beta = jnp.minimum(z, 0.0) - sp
            log_1m = log_beta - z
            if bound is not None:
                log_1m = jnp.where(valid, log_1m, 0.0)
            l16 = log_1m.astype(BF16)
            c = cs[h]
            tails = [None, None]
            for sb in (1, 0):
                blk = l16[sb * bc:(sb + 1) * bc, :]
                t = _dot(u, blk) + c
                tails[sb] = t
                c = t[0:1, :] + blk[0:1, :].astype(F32)
            a = jnp.exp2(log_beta + jnp.concatenate(tails, axis=0))
            if bound is not None:
                a = jnp.where(valid, a, 0.0)
            a16 = a.astype(BF16)
            a_scr[h] += _dot(vt_scr[h, sb0], a16[:bc, :]) + _dot(vt_scr[h, sb0 + 1], a16[bc:, :])
            out.append(c)
        return tuple(out)

    def live(cs):
        m = cs[0]
        for c in cs[1:]:
            m = jnp.maximum(m, c)
        return jnp.max(m) > SB_ZERO_LOG2

    first = jnp.maximum(qi - 1, 0)
    qry = qi * bq + lax.broadcasted_iota(jnp.int32, (1, bq), 1)
    cs = pair(first, (jnp.zeros((1, bq), F32),) * nh, qry)

    n_pairs = first // 2

    def more(state):
        t, cc = state
        return jnp.logical_and(t < n_pairs, live(cc))

    def walk(state):
        t, cc = state
        return t + 1, pair(first - 2 - 2 * t, cc, None)

    t_end, cs = lax.while_loop(more, walk, (jnp.int32(0), cs))

    @pl.when(jnp.logical_and(jnp.logical_and(first % 2 == 1, t_end == n_pairs), live(cs)))
    def _():
        pair(0, cs, jnp.full((1, bq), bc, jnp.int32))

    for h in range(nh):
        o_ref[:, h * hd:(h + 1) * hd] = a_scr[h].T.astype(o_ref.dtype)


def _sb_attn(z, batch, seq, *, bq=256, nh=4):
    nq = seq // bq
    hd = SB_HEAD_DIM
    w = nh * hd
    r = lax.broadcasted_iota(jnp.int32, (bq, bq), 0)
    c = lax.broadcasted_iota(jnp.int32, (bq, bq), 1)
    u = (c > r).astype(BF16)
    return pl.pallas_call(
        functools.partial(_sb_attn_kernel, bq=bq, nh=nh),
        out_shape=jax.ShapeDtypeStruct((batch * seq, SB_WIDTH), BF16),
        grid=(batch, SB_HEADS // nh, nq),
        in_specs=[
            pl.BlockSpec((bq, w), lambda b, h, i: (b * nq + i, COL_QB // w + h)),
            pl.BlockSpec((seq, w), lambda b, h, i: (b, COL_KB // w + h), pipeline_mode=pl.Buffered(1)),
            pl.BlockSpec((seq, w), lambda b, h, i: (b, COL_VB // w + h), pipeline_mode=pl.Buffered(1)),
            pl.BlockSpec((bq, bq), lambda b, h, i: (0, 0)),
        ],
        out_specs=pl.BlockSpec((bq, w), lambda b, h, i: (b * nq + i, h)),
        scratch_shapes=[pltpu.VMEM((nh, seq // bq, hd, bq), BF16), pltpu.VMEM((nh, hd, bq), F32)],
        compiler_params=_params(("arbitrary", "arbitrary", "arbitrary")),
        name="sb_attn",
    )(z, z, z, u)


def _sigmoid(x):
    return 1.0 / (1.0 + jnp.exp(-x))


def _merge_out_kernel(oa_ref, ob_ref, ga_ref, gb_ref, wa_ref, wb_ref, wo_ref, x_ref, o_ref):
    ya = _dot(oa_ref[...], wa_ref[...])
    yb = _dot(ob_ref[...], wb_ref[...])
    ga = _sigmoid(ga_ref[...].astype(F32))
    gb = _sigmoid(gb_ref[...].astype(F32))
    merged = (ga * ya + gb * yb).astype(BF16)
    o_ref[...] = x_ref[...] + _dot(merged, wo_ref[...])


def _merge_out(oa, ob, z, wa, wb, wo, x2d, *, bm=512):
    m, d = x2d.shape
    resident = lambda shape: pl.BlockSpec(shape, lambda i: (0, 0), pipeline_mode=pl.Buffered(1))
    return pl.pallas_call(
        _merge_out_kernel,
        out_shape=jax.ShapeDtypeStruct((m, d), F32),
        grid=(m // bm,),
        in_specs=[
            pl.BlockSpec((bm, DA_WIDTH), lambda i: (i, 0)),
            pl.BlockSpec((bm, SB_WIDTH), lambda i: (i, 0)),
            pl.BlockSpec((bm, d), lambda i: (i, COL_GA // d)),
            pl.BlockSpec((bm, d), lambda i: (i, COL_GB // d)),
            resident((DA_WIDTH, d)), resident((SB_WIDTH, d)), resident((d, d)),
            pl.BlockSpec((bm, d), lambda i: (i, 0)),
        ],
        out_specs=pl.BlockSpec((bm, d), lambda i: (i, 0)),
        compiler_params=_params(("parallel",), VMEM_LIMIT_BIG),
        name="merge_out",
    )(oa, ob, z, z, wa, wb, wo, x2d)


def _mem_kv_kernel(x_ref, g_ref, w_ref, o_ref):
    h = _rms(x_ref[...], g_ref[...]).astype(BF16)
    o_ref[...] = _dot(h, w_ref[...]).astype(o_ref.dtype)


def _mem_kv(mem2d, g, w, *, bm=256):
    m, d = mem2d.shape
    n = w.shape[1]
    return pl.pallas_call(
        _mem_kv_kernel,
        out_shape=jax.ShapeDtypeStruct((m, n), BF16),
        grid=(m // bm,),
        in_specs=[
            pl.BlockSpec((bm, d), lambda i: (i, 0)),
            pl.BlockSpec((1, d), lambda i: (0, 0)),
            pl.BlockSpec((d, n), lambda i: (0, 0)),
        ],
        out_specs=pl.BlockSpec((bm, n), lambda i: (i, 0)),
        compiler_params=_params(("parallel",)),
        name="mem_kv",
    )(mem2d, g.reshape(1, d), w)


def _xattn_kernel(x_ref, g_ref, wq_ref, kv_ref, wo_ref, o_ref, oh_scr):
    x = x_ref[...]
    h = _rms(x, g_ref[...]).astype(BF16)
    q = (_dot(h, wq_ref[...]) * (XA_HEAD_DIM ** -0.5)).astype(BF16)
    for hh in range(XA_HEADS):
        lo = hh * XA_HEAD_DIM
        k = kv_ref[:, lo:lo + XA_HEAD_DIM]
        v = kv_ref[:, XA_WIDTH + lo:XA_WIDTH + lo + XA_HEAD_DIM]
        s = _dot_nt(q[:, lo:lo + XA_HEAD_DIM], k)
        p = jnp.exp(s - jnp.max(s, axis=-1, keepdims=True))
        p = p / jnp.sum(p, axis=-1, keepdims=True)
        oh_scr[:, lo:lo + XA_HEAD_DIM] = _dot(p.astype(BF16), v).astype(BF16)
    o_ref[...] = x + _dot(oh_scr[...], wo_ref[...])


def _xattn(x2d, g, wq, kv, wo, seq, *, bm=1024):
    m, d = x2d.shape
    nseq = seq // bm
    return pl.pallas_call(
        _xattn_kernel,
        out_shape=jax.ShapeDtypeStruct((m, d), F32),
        grid=(m // bm,),
        in_specs=[
            pl.BlockSpec((bm, d), lambda i: (i, 0)),
            pl.BlockSpec((1, d), lambda i: (0, 0)),
            pl.BlockSpec((d, XA_WIDTH), lambda i: (0, 0)),
            pl.BlockSpec((N_MEM, 2 * XA_WIDTH), lambda i: (i // nseq, 0)),
            pl.BlockSpec((XA_WIDTH, d), lambda i: (0, 0)),
        ],
        out_specs=pl.BlockSpec((bm, d), lambda i: (i, 0)),
        scratch_shapes=[pltpu.VMEM((bm, XA_WIDTH), BF16)],
        compiler_params=_params(("parallel",)),
        name="xattn",
    )(x2d, g.reshape(1, d), wq, kv, wo)


def _ffn_kernel(x_ref, halo_ref, g_ref, wg_ref, wv_ref, cwg_ref, cwv_ref, cbg_ref, cbv_ref,
                wd_ref, fg_ref, o_ref, h_scr, *, nseq, final):
    i = pl.program_id(0)
    f = pl.program_id(1)
    pad = SUBLANES

    @pl.when(f == 0)
    def _():
        g = g_ref[...]
        h_scr[pad:, :] = _rms(x_ref[...], g).astype(BF16)
        keep = (i % nseq != 0).astype(F32)
        h_scr[:pad, :] = (_rms(halo_ref[...], g) * keep).astype(BF16)
        o_ref[...] = jnp.zeros(o_ref.shape, F32)

    h = h_scr[...]

    def conv(w_ref, cw_ref, cb_ref):
        u = _dot(h, w_ref[...])
        cw = cw_ref[...]
        y = (cw[2:3, :] * u + cw[1:2, :] * pltpu.roll(u, 1, 0) + cw[0:1, :] * pltpu.roll(u, 2, 0))
        return y[pad:, :] + cb_ref[...]

    gate = conv(wg_ref, cwg_ref, cbg_ref)
    val = conv(wv_ref, cwv_ref, cbv_ref)
    act = (gate * _sigmoid(gate) * val).astype(BF16)
    o_ref[...] += _dot(act, wd_ref[...])

    @pl.when(f == pl.num_programs(1) - 1)
    def _():
        y = x_ref[...] + o_ref[...]
        o_ref[...] = _rms(y, fg_ref[...]) if final else y


def _ffn(x2d, g, w_up, conv_w, conv_b, w_down, final_g, seq, *, final, bm=1024, bf=512):
    m, d = x2d.shape
    nf = D_FF // bf
    nseq = seq // bm
    hb = bm // SUBLANES
    return pl.pallas_call(
        functools.partial(_ffn_kernel, nseq=nseq, final=final),
        out_shape=jax.ShapeDtypeStruct((m, d), F32),
        grid=(m // bm, nf),
        in_specs=[
            pl.BlockSpec((bm, d), lambda i, f: (i, 0), pipeline_mode=pl.Buffered(1)),
            pl.BlockSpec((SUBLANES, d), lambda i, f: (jnp.maximum(i * hb - 1, 0), 0)),
            pl.BlockSpec((1, d), lambda i, f: (0, 0)),
            pl.BlockSpec((d, bf), lambda i, f: (0, f)),
            pl.BlockSpec((d, bf), lambda i, f: (0, nf + f)),
            pl.BlockSpec((CONV_WIDTH, bf), lambda i, f: (0, f)),
            pl.BlockSpec((CONV_WIDTH, bf), lambda i, f: (0, nf + f)),
            pl.BlockSpec((1, bf), lambda i, f: (0, f)),
            pl.BlockSpec((1, bf), lambda i, f: (0, nf + f)),
            pl.BlockSpec((bf, d), lambda i, f: (f, 0)),
            pl.BlockSpec((1, d), lambda i, f: (0, 0)),
        ],
        out_specs=pl.BlockSpec((bm, d), lambda i, f: (i, 0)),
        scratch_shapes=[pltpu.VMEM((bm + SUBLANES, d), BF16)],
        compiler_params=_params(("parallel", "arbitrary"), VMEM_LIMIT_BIG),
        name="ffn",
    )(x2d, x2d, g.reshape(1, d), w_up, w_up, conv_w, conv_w,
      conv_b.reshape(1, -1), conv_b.reshape(1, -1), w_down, final_g.reshape(1, d))


def kernel(x, mem, norm_mix_g, w_in, lambda_q1, lambda_k1, lambda_q2, lambda_k2, da_subln_g,
           w_proj_a, w_proj_b, w_out, norm_x_g, norm_mem_g, w_xq, w_xkv, w_xo, norm_ffn_g,
           w_up, conv_w, conv_b, w_down, final_norm_g):
    batch, seq, d = x.shape
    depth = w_in.shape[0]
    bf = lambda a: a.astype(BF16)
    x2d = x.reshape(batch * seq, d)
    mem2d = mem.reshape(batch * mem.shape[1], d)
    for l in range(depth):
        lam_init = 0.8 - 0.6 * math.exp(-0.3 * l)
        z = _in_proj(x2d, norm_mix_g[l], bf(w_in[l]), seq)
        oa = _diff_attn_pairs(z, lambda_q1[l], lambda_k1[l], lambda_q2[l], lambda_k2[l],
                        da_subln_g[l], batch, seq, lam_init)
        ob = _sb_attn(z, batch, seq)
        x2d = _merge_out(oa, ob, z, bf(w_proj_a[l]), bf(w_proj_b[l]), bf(w_out[l]), x2d)
        kv = _mem_kv(mem2d, norm_mem_g[l], bf(w_xkv[l]))
        x2d = _xattn(x2d, norm_x_g[l], bf(w_xq[l]), kv, bf(w_xo[l]), seq)
        x2d = _ffn(x2d, norm_ffn_g[l], bf(w_up[l]), conv_w[l], conv_b[l], bf(w_down[l]),
                   final_norm_g, seq, final=(l == depth - 1))
    return x2d.reshape(batch, seq, d)
```

```python
import functools
import math

import jax
import jax.numpy as jnp
from jax import lax
from jax.experimental import pallas as pl
from jax.experimental.pallas import tpu as pltpu

F32 = jnp.float32
BF16 = jnp.bfloat16

D_MODEL = 2048
N_MEM = 256
DA_HEAD_DIM = 64
DA_WIDTH = D_MODEL // 2
DA_HEADS = DA_WIDTH // (2 * DA_HEAD_DIM)
DA_ROT_DIM = DA_HEAD_DIM // 4
SB_HEAD_DIM = 128
SB_WIDTH = D_MODEL // 2
SB_HEADS = SB_WIDTH // SB_HEAD_DIM
XA_HEADS = 4
XA_HEAD_DIM = 128
XA_WIDTH = XA_HEADS * XA_HEAD_DIM
D_FF = 256 * ((8 * D_MODEL // 3 + 255) // 256)
CONV_WIDTH = 3
ROPE_THETA = 500000.0
EPS = 1e-6
N_IN = 3 * DA_WIDTH + 3 * SB_WIDTH + 2 * D_MODEL

LANES = 128
SUBLANES = 8
BF16_ROWS = 16
NEG_BIG = -1e30
LOG2E = math.log2(math.e)
SB_ZERO_LOG2 = -160.0
TRANSPOSE_KEYS = 1024
FROZEN_MAX_RISE = 64.0
VMEM_LIMIT = 48 * 1024 * 1024
VMEM_LIMIT_BIG = 60 * 1024 * 1024

COL_QA, COL_KA, COL_VA = 0, DA_WIDTH, 2 * DA_WIDTH
COL_QB, COL_KB, COL_VB = 3 * DA_WIDTH, 3 * DA_WIDTH + SB_WIDTH, 3 * DA_WIDTH + 2 * SB_WIDTH
COL_GA = 3 * DA_WIDTH + 3 * SB_WIDTH
COL_GB = COL_GA + D_MODEL


def _params(sem, vmem_limit=VMEM_LIMIT):
    return pltpu.CompilerParams(dimension_semantics=sem, vmem_limit_bytes=vmem_limit)


def _rms(x, g):
    return x * lax.rsqrt(jnp.mean(x * x, axis=-1, keepdims=True) + EPS) * g


def _dot(a, b):
    return jnp.dot(a, b, preferred_element_type=F32)


def _dot_nt(a, b):
    return lax.dot_general(a, b, (((1,), (1,)), ((), ())), preferred_element_type=F32)


def _in_proj_kernel(x_ref, g_ref, w_ref, scale_ref, cos_ref, sp_ref, sm_ref, o_ref, h_scr, *, bn):
    j = pl.program_id(1)

    @pl.when(j == 0)
    def _():
        h_scr[...] = _rms(x_ref[...], g_ref[...]).astype(BF16)

    acc = _dot(h_scr[...], w_ref[...]) * scale_ref[...]
    col = j * bn

    @pl.when(col < COL_VA)
    def _():
        cos, sp, sm = cos_ref[...], sp_ref[...], sm_ref[...]
        for c in range(bn // LANES):
            a = acc[:, c * LANES:(c + 1) * LANES]
            r = a * cos + pltpu.roll(a, DA_ROT_DIM // 2, 1) * sp \
                + pltpu.roll(a, LANES - DA_ROT_DIM // 2, 1) * sm
            o_ref[:, c * LANES:(c + 1) * LANES] = r.astype(o_ref.dtype)

    @pl.when(col >= COL_VA)
    def _():
        o_ref[...] = acc.astype(o_ref.dtype)


def _rope_lane_tables(seq):
    half = DA_ROT_DIM // 2
    inv = ROPE_THETA ** (-jnp.arange(0, DA_ROT_DIM, 2, dtype=F32) / DA_ROT_DIM)
    ang = jnp.arange(seq, dtype=F32)[:, None] * inv[None, :]
    cos, sin = jnp.cos(ang), jnp.sin(ang)
    lane = jnp.arange(LANES) % DA_HEAD_DIM
    idx = lane % half
    lo = (lane < half)[None, :]
    hi = jnp.logical_and(lane >= half, lane < 2 * half)[None, :]
    cos_t = jnp.where(jnp.logical_or(lo, hi), cos[:, idx], 1.0)
    sp_t = jnp.where(hi, sin[:, idx], 0.0)
    sm_t = jnp.where(lo, -sin[:, idx], 0.0)
    return cos_t.astype(F32), sp_t.astype(F32), sm_t.astype(F32)


def _in_proj(x2d, g, w, seq, *, bm=1024, bn=2048):
    m, d = x2d.shape
    n = w.shape[1]
    cos_t, sp_t, sm_t = _rope_lane_tables(seq)
    col = jnp.arange(n)
    scale = jnp.where(col < COL_KA, LOG2E * DA_HEAD_DIM ** -0.5,
                      jnp.where((col >= COL_QB) & (col < COL_KB), LOG2E * SB_HEAD_DIM ** -0.5, 1.0))
    scale = scale.astype(F32).reshape(1, n)
    nseq = seq // bm
    tab_spec = pl.BlockSpec((bm, LANES), lambda i, j: (i % nseq, 0))
    return pl.pallas_call(
        functools.partial(_in_proj_kernel, bn=bn),
        out_shape=jax.ShapeDtypeStruct((m, n), BF16),
        grid=(m // bm, n // bn),
        in_specs=[
            pl.BlockSpec((bm, d), lambda i, j: (i, 0)),
            pl.BlockSpec((1, d), lambda i, j: (0, 0)),
            pl.BlockSpec((d, bn), lambda i, j: (0, j)),
            pl.BlockSpec((1, bn), lambda i, j: (0, j)),
            tab_spec, tab_spec, tab_spec,
        ],
        out_specs=pl.BlockSpec((bm, bn), lambda i, j: (i, j)),
        scratch_shapes=[pltpu.VMEM((bm, d), BF16)],
        compiler_params=_params(("parallel", "arbitrary"), VMEM_LIMIT_BIG),
        name="in_proj",
    )(x2d, g.reshape(1, d), w, scale, cos_t, sp_t, sm_t)


def _transpose_values(v_ref, vt_scr, bk):
    hd = v_ref.shape[1]
    extra = vt_scr.shape[1] - hd
    per = max(1, TRANSPOSE_KEYS // bk)
    chunk = per * bk

    def body(c, carry):
        start = pl.multiple_of(c * chunk, chunk)
        vt = v_ref[pl.ds(start, chunk), :].astype(F32).T.astype(BF16)
        for s in range(per):
            vt_scr[c * per + s, :hd, :] = vt[:, s * bk:(s + 1) * bk]
            if extra:
                vt_scr[c * per + s, hd:, :] = jnp.ones((extra, bk), BF16)
        return carry
    lax.fori_loop(0, vt_scr.shape[0] // per, body, 0)


def _diff_head_pair_kernel(q_ref, k_ref, v_ref, lq1_ref, lk1_ref, lq2_ref, lk2_ref, g_ref, o_ref,
                           vt_scr, a_scr, *, bq, bk, nh, lam_init):
    qi = pl.program_id(2)
    hd = 2 * DA_HEAD_DIM
    nmap = 2 * nh

    @pl.when(qi == 0)
    def _():
        for h in range(nh):
            _transpose_values(v_ref.at[:, h * hd:(h + 1) * hd], vt_scr.at[h], bk)

    qts = []
    for h in range(nh):
        qt = q_ref[:, h * hd:(h + 1) * hd].astype(F32).T
        feat = lax.broadcasted_iota(jnp.int32, qt.shape, 0)
        qts.append(jnp.where(feat < DA_HEAD_DIM, qt, 0.0).astype(BF16))
        qts.append(jnp.where(feat >= DA_HEAD_DIM, qt, 0.0).astype(BF16))
    def scores(start, nk, causal):
        k = k_ref[pl.ds(start, nk), :]
        ss = [_dot(k[:, (i // 2) * hd:(i // 2 + 1) * hd], qts[i]) for i in range(nmap)]
        if causal:
            key = start + lax.broadcasted_iota(jnp.int32, (nk, bq), 0)
            qry = qi * bq + lax.broadcasted_iota(jnp.int32, (nk, bq), 1)
            keep = key <= qry
            ss = [jnp.where(keep, s, NEG_BIG) for s in ss]
        return ss


    def online(kb, ms, nkeys=None):
        nk = bk if nkeys is None else nkeys
        ss = scores(pl.multiple_of(kb * bk, bk), nk, nkeys is not None)
        out = []
        for i in range(nmap):
            m_new = jnp.maximum(ms[i], jnp.max(ss[i], axis=0, keepdims=True))
            alpha = jnp.exp2(ms[i] - m_new)
            p = jnp.exp2(ss[i] - m_new).astype(BF16)
            a_scr[i] = alpha * a_scr[i] + _dot(vt_scr[i // 2, kb, :, :nk], p)
            out.append(m_new)
        return tuple(out)

    def frozen(kb, carry, nkeys=None):
        ms, gs = carry
        nk = bk if nkeys is None else nkeys
        ss = scores(pl.multiple_of(kb * bk, bk), nk, False)
        out = []
        for i in range(nmap):
            p = jnp.exp2(ss[i] - ms[i]).astype(BF16)
            a_scr[i] += _dot(vt_scr[i // 2, kb, :, :nk], p)
            out.append(jnp.maximum(gs[i], jnp.max(ss[i], axis=0, keepdims=True)))
        return ms, tuple(out)

    neg = jnp.full((1, bq), NEG_BIG, F32)
    n_full = (qi * bq) // bk
    per = bk // bq
    def run_online():
        a_scr[...] = jnp.zeros(a_scr.shape, F32)
        ms = lax.fori_loop(0, n_full, online, (neg,) * nmap)
        lax.switch(qi % per, [functools.partial(online, n_full, nkeys=(r + 1) * bq) for r in range(per)], ms)

    def diagonal_first(r):
        start = pl.multiple_of(n_full * bk + r * bq, bq)
        ms = []
        for i, s in enumerate(scores(start, bq, True)):
            m = jnp.max(s, axis=0, keepdims=True)
            a_scr[i] = _dot(vt_scr[i // 2, n_full, :, r * bq:(r + 1) * bq], jnp.exp2(s - m).astype(BF16))
            ms.append(m)
        carry = (tuple(ms), tuple(ms))
        return frozen(n_full, carry, nkeys=r * bq) if r else carry

    def run_frozen():
        carry = lax.switch(qi % per, [functools.partial(diagonal_first, r) for r in range(per)])
        ms, gs = lax.fori_loop(0, n_full, frozen, carry)
        rise = gs[0] - ms[0]
        for i in range(1, nmap):
            rise = jnp.maximum(rise, gs[i] - ms[i])
        return jnp.max(rise) <= FROZEN_MAX_RISE

    @pl.when(jnp.logical_not(run_frozen()))
    def _():
        run_online()

    lam = (jnp.exp(jnp.sum(lq1_ref[...] * lk1_ref[...], axis=-1, keepdims=True))
           - jnp.exp(jnp.sum(lq2_ref[...] * lk2_ref[...], axis=-1, keepdims=True)) + lam_init)
    for h in range(nh):
        a1, a2 = a_scr[2 * h], a_scr[2 * h + 1]
        ot = a1[:hd, :] / a1[hd:hd + 1, :] - lam * (a2[:hd, :] / a2[hd:hd + 1, :])
        o = _rms(ot.T, g_ref[...]) * (1.0 - lam_init)
        o_ref[:, h * hd:(h + 1) * hd] = o.astype(o_ref.dtype)


def _diff_attn_pairs(z, lq1, lk1, lq2, lk2, subln_g, batch, seq, lam_init, *, bq=256, bk=1024, nh=4):
    nq = seq // bq
    hd = 2 * DA_HEAD_DIM
    w = nh * hd
    vec = lambda a: a.reshape(1, -1).astype(F32)
    small = lambda n: pl.BlockSpec((1, n), lambda b, h, i: (0, 0))
    return pl.pallas_call(
        functools.partial(_diff_head_pair_kernel, bq=bq, bk=bk, nh=nh, lam_init=lam_init),
        out_shape=jax.ShapeDtypeStruct((batch * seq, DA_WIDTH), BF16),
        grid=(batch, DA_HEADS // nh, nq),
        in_specs=[
            pl.BlockSpec((bq, w), lambda b, h, i: (b * nq + i, COL_QA // w + h)),
            pl.BlockSpec((seq, w), lambda b, h, i: (b, COL_KA // w + h), pipeline_mode=pl.Buffered(1)),
            pl.BlockSpec((seq, w), lambda b, h, i: (b, COL_VA // w + h), pipeline_mode=pl.Buffered(1)),
            small(DA_HEAD_DIM), small(DA_HEAD_DIM), small(DA_HEAD_DIM), small(DA_HEAD_DIM),
            small(hd),
        ],
        out_specs=pl.BlockSpec((bq, w), lambda b, h, i: (b * nq + i, h)),
        scratch_shapes=[
            pltpu.VMEM((nh, seq // bk, hd + BF16_ROWS, bk), BF16),
            pltpu.VMEM((2 * nh, hd + BF16_ROWS, bq), F32),
        ],
        compiler_params=_params(("arbitrary", "arbitrary", "arbitrary")),
        name="diff_attn",
    )(z, z, z, vec(lq1), vec(lk1), vec(lq2), vec(lk2), vec(subln_g))


def _sb_attn_kernel(q_ref, k_ref, v_ref, u_ref, o_ref, vt_scr, a_scr, *, bq, nh):
    qi = pl.program_id(2)
    hd = SB_HEAD_DIM
    bc = bq

    @pl.when(qi == 0)
    def _():
        for h in range(nh):
            _transpose_values(v_ref.at[:, h * hd:(h + 1) * hd], vt_scr.at[h], bc)

    qts = [q_ref[:, h * hd:(h + 1) * hd].astype(F32).T.astype(BF16) for h in range(nh)]
    a_scr[...] = jnp.zeros(a_scr.shape, F32)
    u = u_ref[...]

    def pair(sb0, cs, bound):
        start = pl.multiple_of(sb0 * bc, bc)
        k = k_ref[pl.ds(start, 2 * bc), :]
        zs = [_dot(k[:, h * hd:(h + 1) * hd], qts[h]) for h in range(nh)]
        if bound is not None:
            valid = start + lax.broadcasted_iota(jnp.int32, (2 * bc, bq), 0) < bound
        out = []
        for h in range(nh):
            z = zs[h]
            sp = jnp.log2(1.0 + jnp.exp2(-jnp.abs(z)))
            log_beta = jnp.minimum(z, 0.0) - sp
            log_1m = log_beta - z
            if bound is not None:
                log_1m = jnp.where(valid, log_1m, 0.0)
            l16 = log_1m.astype(BF16)
            c = cs[h]
            tails = [None, None]
            for sb in (1, 0):
                blk = l16[sb * bc:(sb + 1) * bc, :]
                t = _dot(u, blk) + c
                tails[sb] = t
                c = t[0:1, :] + blk[0:1, :].astype(F32)
            a = jnp.exp2(log_beta + jnp.concatenate(tails, axis=0))
            if bound is not None:
                a = jnp.where(valid, a, 0.0)
            a16 = a.astype(BF16)
            a_scr[h] += _dot(vt_scr[h, sb0], a16[:bc, :]) + _dot(vt_scr[h, sb0 + 1], a16[bc:, :])
            out.append(c)
        return tuple(out)

    def live(cs):
        m = cs[0]
        for c in cs[1:]:
            m = jnp.maximum(m, c)
        return jnp.max(m) > SB_ZERO_LOG2

    first = jnp.maximum(qi - 1, 0)
    qry = qi * bq + lax.broadcasted_iota(jnp.int32, (1, bq), 1)
    cs = pair(first, (jnp.zeros((1, bq), F32),) * nh, qry)

    n_pairs = first // 2

    def more(state):
        t, cc = state
        return jnp.logical_and(t < n_pairs, live(cc))

    def walk(state):
        t, cc = state
        return t + 1, pair(first - 2 - 2 * t, cc, None)

    t_end, cs = lax.while_loop(more, walk, (jnp.int32(0), cs))

    @pl.when(jnp.logical_and(jnp.logical_and(first % 2 == 1, t_end == n_pairs), live(cs)))
    def _():
        pair(0, cs, jnp.full((1, bq), bc, jnp.int32))

    for h in range(nh):
        o_ref[:, h * hd:(h + 1) * hd] = a_scr[h].T.astype(o_ref.dtype)


def _sb_attn(z, batch, seq, *, bq=256, nh=4):
    nq = seq // bq
    hd = SB_HEAD_DIM
    w = nh * hd
    r = lax.broadcasted_iota(jnp.int32, (bq, bq), 0)
    c = lax.broadcasted_iota(jnp.int32, (bq, bq), 1)
    u = (c > r).astype(BF16)
    return pl.pallas_call(
        functools.partial(_sb_attn_kernel, bq=bq, nh=nh),
        out_shape=jax.ShapeDtypeStruct((batch * seq, SB_WIDTH), BF16),
        grid=(batch, SB_HEADS // nh, nq),
        in_specs=[
            pl.BlockSpec((bq, w), lambda b, h, i: (b * nq + i, COL_QB // w + h)),
            pl.BlockSpec((seq, w), lambda b, h, i: (b, COL_KB // w + h), pipeline_mode=pl.Buffered(1)),
            pl.BlockSpec((seq, w), lambda b, h, i: (b, COL_VB // w + h), pipeline_mode=pl.Buffered(1)),
            pl.BlockSpec((bq, bq), lambda b, h, i: (0, 0)),
        ],
        out_specs=pl.BlockSpec((bq, w), lambda b, h, i: (b * nq + i, h)),
        scratch_shapes=[pltpu.VMEM((nh, seq // bq, hd, bq), BF16), pltpu.VMEM((nh, hd, bq), F32)],
        compiler_params=_params(("arbitrary", "arbitrary", "arbitrary")),
        name="sb_attn",
    )(z, z, z, u)


def _sigmoid(x):
    return 1.0 / (1.0 + jnp.exp(-x))


def _merge_out_kernel(oa_ref, ob_ref, ga_ref, gb_ref, wa_ref, wb_ref, wo_ref, x_ref, o_ref):
    ya = _dot(oa_ref[...], wa_ref[...])
    yb = _dot(ob_ref[...], wb_ref[...])
    ga = _sigmoid(ga_ref[...].astype(F32))
    gb = _sigmoid(gb_ref[...].astype(F32))
    merged = (ga * ya + gb * yb).astype(BF16)
    o_ref[...] = x_ref[...] + _dot(merged, wo_ref[...])


def _merge_out(oa, ob, z, wa, wb, wo, x2d, *, bm=512):
    m, d = x2d.shape
    resident = lambda shape: pl.BlockSpec(shape, lambda i: (0, 0), pipeline_mode=pl.Buffered(1))
    return pl.pallas_call(
        _merge_out_kernel,
        out_shape=jax.ShapeDtypeStruct((m, d), F32),
        grid=(m // bm,),
        in_specs=[
            pl.BlockSpec((bm, DA_WIDTH), lambda i: (i, 0)),
            pl.BlockSpec((bm, SB_WIDTH), lambda i: (i, 0)),
            pl.BlockSpec((bm, d), lambda i: (i, COL_GA // d)),
            pl.BlockSpec((bm, d), lambda i: (i, COL_GB // d)),
            resident((DA_WIDTH, d)), resident((SB_WIDTH, d)), resident((d, d)),
            pl.BlockSpec((bm, d), lambda i: (i, 0)),
        ],
        out_specs=pl.BlockSpec((bm, d), lambda i: (i, 0)),
        compiler_params=_params(("parallel",), VMEM_LIMIT_BIG),
        name="merge_out",
    )(oa, ob, z, z, wa, wb, wo, x2d)


def _mem_kv_kernel(x_ref, g_ref, w_ref, o_ref):
    h = _rms(x_ref[...], g_ref[...]).astype(BF16)
    o_ref[...] = _dot(h, w_ref[...]).astype(o_ref.dtype)


def _mem_kv(mem2d, g, w, *, bm=256):
    m, d = mem2d.shape
    n = w.shape[1]
    return pl.pallas_call(
        _mem_kv_kernel,
        out_shape=jax.ShapeDtypeStruct((m, n), BF16),
        grid=(m // bm,),
        in_specs=[
            pl.BlockSpec((bm, d), lambda i: (i, 0)),
            pl.BlockSpec((1, d), lambda i: (0, 0)),
            pl.BlockSpec((d, n), lambda i: (0, 0)),
        ],
        out_specs=pl.BlockSpec((bm, n), lambda i: (i, 0)),
        compiler_params=_params(("parallel",)),
        name="mem_kv",
    )(mem2d, g.reshape(1, d), w)


def _xattn_kernel(x_ref, g_ref, wq_ref, kv_ref, wo_ref, o_ref, oh_scr):
    x = x_ref[...]
    h = _rms(x, g_ref[...]).astype(BF16)
    q = (_dot(h, wq_ref[...]) * (XA_HEAD_DIM ** -0.5)).astype(BF16)
    for hh in range(XA_HEADS):
        lo = hh * XA_HEAD_DIM
        k = kv_ref[:, lo:lo + XA_HEAD_DIM]
        v = kv_ref[:, XA_WIDTH + lo:XA_WIDTH + lo + XA_HEAD_DIM]
        s = _dot_nt(q[:, lo:lo + XA_HEAD_DIM], k)
        p = jnp.exp(s - jnp.max(s, axis=-1, keepdims=True))
        p = p / jnp.sum(p, axis=-1, keepdims=True)
        oh_scr[:, lo:lo + XA_HEAD_DIM] = _dot(p.astype(BF16), v).astype(BF16)
    o_ref[...] = x + _dot(oh_scr[...], wo_ref[...])


def _xattn(x2d, g, wq, kv, wo, seq, *, bm=1024):
    m, d = x2d.shape
    nseq = seq // bm
    return pl.pallas_call(
        _xattn_kernel,
        out_shape=jax.ShapeDtypeStruct((m, d), F32),
        grid=(m // bm,),
        in_specs=[
            pl.BlockSpec((bm, d), lambda i: (i, 0)),
            pl.BlockSpec((1, d), lambda i: (0, 0)),
            pl.BlockSpec((d, XA_WIDTH), lambda i: (0, 0)),
            pl.BlockSpec((N_MEM, 2 * XA_WIDTH), lambda i: (i // nseq, 0)),
            pl.BlockSpec((XA_WIDTH, d), lambda i: (0, 0)),
        ],
        out_specs=pl.BlockSpec((bm, d), lambda i: (i, 0)),
        scratch_shapes=[pltpu.VMEM((bm, XA_WIDTH), BF16)],
        compiler_params=_params(("parallel",)),
        name="xattn",
    )(x2d, g.reshape(1, d), wq, kv, wo)


def _ffn_kernel(x_ref, halo_ref, g_ref, wg_ref, wv_ref, cwg_ref, cwv_ref, cbg_ref, cbv_ref,
                wd_ref, fg_ref, o_ref, h_scr, *, nseq, final):
    i = pl.program_id(0)
    f = pl.program_id(1)
    pad = SUBLANES

    @pl.when(f == 0)
    def _():
        g = g_ref[...]
        h_scr[pad:, :] = _rms(x_ref[...], g).astype(BF16)
        keep = (i % nseq != 0).astype(F32)
        h_scr[:pad, :] = (_rms(halo_ref[...], g) * keep).astype(BF16)
        o_ref[...] = jnp.zeros(o_ref.shape, F32)

    h = h_scr[...]

    def conv(w_ref, cw_ref, cb_ref):
        u = _dot(h, w_ref[...])
        cw = cw_ref[...]
        y = (cw[2:3, :] * u + cw[1:2, :] * pltpu.roll(u, 1, 0) + cw[0:1, :] * pltpu.roll(u, 2, 0))
        return y[pad:, :] + cb_ref[...]

    gate = conv(wg_ref, cwg_ref, cbg_ref)
    val = conv(wv_ref, cwv_ref, cbv_ref)
    act = (gate * _sigmoid(gate) * val).astype(BF16)
    o_ref[...] += _dot(act, wd_ref[...])

    @pl.when(f == pl.num_programs(1) - 1)
    def _():
        y = x_ref[...] + o_ref[...]
        o_ref[...] = _rms(y, fg_ref[...]) if final else y


def _ffn(x2d, g, w_up, conv_w, conv_b, w_down, final_g, seq, *, final, bm=1024, bf=512):
    m, d = x2d.shape
    nf = D_FF // bf
    nseq = seq // bm
    hb = bm // SUBLANES
    return pl.pallas_call(
        functools.partial(_ffn_kernel, nseq=nseq, final=final),
        out_shape=jax.ShapeDtypeStruct((m, d), F32),
        grid=(m // bm, nf),
        in_specs=[
            pl.BlockSpec((bm, d), lambda i, f: (i, 0), pipeline_mode=pl.Buffered(1)),
            pl.BlockSpec((SUBLANES, d), lambda i, f: (jnp.maximum(i * hb - 1, 0), 0)),
            pl.BlockSpec((1, d), lambda i, f: (0, 0)),
            pl.BlockSpec((d, bf), lambda i, f: (0, f)),
            pl.BlockSpec((d, bf), lambda i, f: (0, nf + f)),
            pl.BlockSpec((CONV_WIDTH, bf), lambda i, f: (0, f)),
            pl.BlockSpec((CONV_WIDTH, bf), lambda i, f: (0, nf + f)),
            pl.BlockSpec((1, bf), lambda i, f: (0, f)),
            pl.BlockSpec((1, bf), lambda i, f: (0, nf + f)),
            pl.BlockSpec((bf, d), lambda i, f: (f, 0)),
            pl.BlockSpec((1, d), lambda i, f: (0, 0)),
        ],
        out_specs=pl.BlockSpec((bm, d), lambda i, f: (i, 0)),
        scratch_shapes=[pltpu.VMEM((bm + SUBLANES, d), BF16)],
        compiler_params=_params(("parallel", "arbitrary"), VMEM_LIMIT_BIG),
        name="ffn",
    )(x2d, x2d, g.reshape(1, d), w_up, w_up, conv_w, conv_w,
      conv_b.reshape(1, -1), conv_b.reshape(1, -1), w_down, final_g.reshape(1, d))


def kernel(x, mem, norm_mix_g, w_in, lambda_q1, lambda_k1, lambda_q2, lambda_k2, da_subln_g,
           w_proj_a, w_proj_b, w_out, norm_x_g, norm_mem_g, w_xq, w_xkv, w_xo, norm_ffn_g,
           w_up, conv_w, conv_b, w_down, final_norm_g):
    batch, seq, d = x.shape
    depth = w_in.shape[0]
    bf = lambda a: a.astype(BF16)
    x2d = x.reshape(batch * seq, d)
    mem2d = mem.reshape(batch * mem.shape[1], d)
    for l in range(depth):
        lam_init = 0.8 - 0.6 * math.exp(-0.3 * l)
        z = _in_proj(x2d, norm_mix_g[l], bf(w_in[l]), seq)
        oa = _diff_attn_pairs(z, lambda_q1[l], lambda_k1[l], lambda_q2[l], lambda_k2[l],
                        da_subln_g[l], batch, seq, lam_init)
        ob = _sb_attn(z, batch, seq)
        x2d = _merge_out(oa, ob, z, bf(w_proj_a[l]), bf(w_proj_b[l]), bf(w_out[l]), x2d)
        kv = _mem_kv(mem2d, norm_mem_g[l], bf(w_xkv[l]))
        x2d = _xattn(x2d, norm_x_g[l], bf(w_xq[l]), kv, bf(w_xo[l]), seq)
        x2d = _ffn(x2d, norm_ffn_g[l], bf(w_up[l]), conv_w[l], conv_b[l], bf(w_down[l]),
                   final_norm_g, seq, final=(l == depth - 1))
    return x2d.reshape(batch, seq, d)
```

```python
import functools
import math

import jax
import jax.numpy as jnp
from jax import lax
from jax.experimental import pallas as pl
from jax.experimental.pallas import tpu as pltpu

F32 = jnp.float32
BF16 = jnp.bfloat16

D_MODEL = 2048
N_MEM = 256
DA_HEAD_DIM = 64
DA_WIDTH = D_MODEL // 2
DA_HEADS = DA_WIDTH // (2 * DA_HEAD_DIM)
DA_ROT_DIM = DA_HEAD_DIM // 4
SB_HEAD_DIM = 128
SB_WIDTH = D_MODEL // 2
SB_HEADS = SB_WIDTH // SB_HEAD_DIM
XA_HEADS = 4
XA_HEAD_DIM = 128
XA_WIDTH = XA_HEADS * XA_HEAD_DIM
D_FF = 256 * ((8 * D_MODEL // 3 + 255) // 256)
CONV_WIDTH = 3
ROPE_THETA = 500000.0
EPS = 1e-6
N_IN = 3 * DA_WIDTH + 3 * SB_WIDTH + 2 * D_MODEL

LANES = 128
SUBLANES = 8
BF16_ROWS = 16
NEG_BIG = -1e30
LOG2E = math.log2(math.e)
SB_ZERO_LOG2 = -160.0
TRANSPOSE_KEYS = 1024
FROZEN_MAX_RISE = 64.0
VMEM_LIMIT = 48 * 1024 * 1024
VMEM_LIMIT_BIG = 60 * 1024 * 1024

COL_QA, COL_KA, COL_VA = 0, DA_WIDTH, 2 * DA_WIDTH
COL_QB, COL_KB, COL_VB = 3 * DA_WIDTH, 3 * DA_WIDTH + SB_WIDTH, 3 * DA_WIDTH + 2 * SB_WIDTH
COL_GA = 3 * DA_WIDTH + 3 * SB_WIDTH
COL_GB = COL_GA + D_MODEL


def _params(sem, vmem_limit=VMEM_LIMIT):
    return pltpu.CompilerParams(dimension_semantics=sem, vmem_limit_bytes=vmem_limit)


def _rms(x, g):
    return x * lax.rsqrt(jnp.mean(x * x, axis=-1, keepdims=True) + EPS) * g


def _dot(a, b):
    return jnp.dot(a, b, preferred_element_type=F32)


def _dot_nt(a, b):
    return lax.dot_general(a, b, (((1,), (1,)), ((), ())), preferred_element_type=F32)


def _in_proj_kernel(x_ref, g_ref, w_ref, scale_ref, cos_ref, sp_ref, sm_ref, o_ref, h_scr, *, bn):
    j = pl.program_id(1)

    @pl.when(j == 0)
    def _():
        h_scr[...] = _rms(x_ref[...], g_ref[...]).astype(BF16)

    acc = _dot(h_scr[...], w_ref[...]) * scale_ref[...]
    col = j * bn

    @pl.when(col < COL_VA)
    def _():
        cos, sp, sm = cos_ref[...], sp_ref[...], sm_ref[...]
        for c in range(bn // LANES):
            a = acc[:, c * LANES:(c + 1) * LANES]
            r = a * cos + pltpu.roll(a, DA_ROT_DIM // 2, 1) * sp \
                + pltpu.roll(a, LANES - DA_ROT_DIM // 2, 1) * sm
            o_ref[:, c * LANES:(c + 1) * LANES] = r.astype(o_ref.dtype)

    @pl.when(col >= COL_VA)
    def _():
        o_ref[...] = acc.astype(o_ref.dtype)


def _rope_lane_tables(seq):
    half = DA_ROT_DIM // 2
    inv = ROPE_THETA ** (-jnp.arange(0, DA_ROT_DIM, 2, dtype=F32) / DA_ROT_DIM)
    ang = jnp.arange(seq, dtype=F32)[:, None] * inv[None, :]
    cos, sin = jnp.cos(ang), jnp.sin(ang)
    lane = jnp.arange(LANES) % DA_HEAD_DIM
    idx = lane % half
    lo = (lane < half)[None, :]
    hi = jnp.logical_and(lane >= half, lane < 2 * half)[None, :]
    cos_t = jnp.where(jnp.logical_or(lo, hi), cos[:, idx], 1.0)
    sp_t = jnp.where(hi, sin[:, idx], 0.0)
    sm_t = jnp.where(lo, -sin[:, idx], 0.0)
    return cos_t.astype(F32), sp_t.astype(F32), sm_t.astype(F32)


def _in_proj(x2d, g, w, seq, *, bm=1024, bn=2048):
    m, d = x2d.shape
    n = w.shape[1]
    cos_t, sp_t, sm_t = _rope_lane_tables(seq)
    col = jnp.arange(n)
    scale = jnp.where(col < COL_KA, LOG2E * DA_HEAD_DIM ** -0.5,
                      jnp.where((col >= COL_QB) & (col < COL_KB), LOG2E * SB_HEAD_DIM ** -0.5, 1.0))
    scale = scale.astype(F32).reshape(1, n)
    nseq = seq // bm
    tab_spec = pl.BlockSpec((bm, LANES), lambda i, j: (i % nseq, 0))
    return pl.pallas_call(
        functools.partial(_in_proj_kernel, bn=bn),
        out_shape=jax.ShapeDtypeStruct((m, n), BF16),
        grid=(m // bm, n // bn),
        in_specs=[
            pl.BlockSpec((bm, d), lambda i, j: (i, 0)),
            pl.BlockSpec((1, d), lambda i, j: (0, 0)),
            pl.BlockSpec((d, bn), lambda i, j: (0, j)),
            pl.BlockSpec((1, bn), lambda i, j: (0, j)),
            tab_spec, tab_spec, tab_spec,
        ],
        out_specs=pl.BlockSpec((bm, bn), lambda i, j: (i, j)),
        scratch_shapes=[pltpu.VMEM((bm, d), BF16)],
        compiler_params=_params(("parallel", "arbitrary"), VMEM_LIMIT_BIG),
        name="in_proj",
    )(x2d, g.reshape(1, d), w, scale, cos_t, sp_t, sm_t)


def _transpose_values(v_ref, vt_scr, bk):
    hd = v_ref.shape[1]
    extra = vt_scr.shape[1] - hd
    per = max(1, TRANSPOSE_KEYS // bk)
    chunk = per * bk

    def body(c, carry):
        start = pl.multiple_of(c * chunk, chunk)
        vt = v_ref[pl.ds(start, chunk), :].astype(F32).T.astype(BF16)
        for s in range(per):
            vt_scr[c * per + s, :hd, :] = vt[:, s * bk:(s + 1) * bk]
            if extra:
                vt_scr[c * per + s, hd:, :] = jnp.ones((extra, bk), BF16)
        return carry
    lax.fori_loop(0, vt_scr.shape[0] // per, body, 0)


def _diff_head_pair_kernel(q_ref, k_ref, v_ref, lq1_ref, lk1_ref, lq2_ref, lk2_ref, g_ref, o_ref,
                           vt_scr, a_scr, *, bq, bk, nh, lam_init):
    qi = pl.program_id(2)
    hd = 2 * DA_HEAD_DIM
    nmap = 2 * nh

    @pl.when(qi == 0)
    def _():
        for h in range(nh):
            _transpose_values(v_ref.at[:, h * hd:(h + 1) * hd], vt_scr.at[h], bk)

    qts = []
    for h in range(nh):
        qt = q_ref[:, h * hd:(h + 1) * hd].astype(F32).T
        feat = lax.broadcasted_iota(jnp.int32, qt.shape, 0)
        qts.append(jnp.where(feat < DA_HEAD_DIM, qt, 0.0).astype(BF16))
        qts.append(jnp.where(feat >= DA_HEAD_DIM, qt, 0.0).astype(BF16))
    def scores(start, nk, causal):
        k = k_ref[pl.ds(start, nk), :]
        ss = [_dot(k[:, (i // 2) * hd:(i // 2 + 1) * hd], qts[i]) for i in range(nmap)]
        if causal:
            key = start + lax.broadcasted_iota(jnp.int32, (nk, bq), 0)
            qry = qi * bq + lax.broadcasted_iota(jnp.int32, (nk, bq), 1)
            keep = key <= qry
            ss = [jnp.where(keep, s, NEG_BIG) for s in ss]
        return ss


    def online(kb, ms, nkeys=None):
        nk = bk if nkeys is None else nkeys
        ss = scores(pl.multiple_of(kb * bk, bk), nk, nkeys is not None)
        out = []
        for i in range(nmap):
            m_new = jnp.maximum(ms[i], jnp.max(ss[i], axis=0, keepdims=True))
            alpha = jnp.exp2(ms[i] - m_new)
            p = jnp.exp2(ss[i] - m_new).astype(BF16)
            a_scr[i] = alpha * a_scr[i] + _dot(vt_scr[i // 2, kb, :, :nk], p)
            out.append(m_new)
        return tuple(out)

    def frozen(kb, carry, nkeys=None):
        ms, gs = carry
        nk = bk if nkeys is None else nkeys
        ss = scores(pl.multiple_of(kb * bk, bk), nk, False)
        out = []
        for i in range(nmap):
            p = jnp.exp2(ss[i] - ms[i]).astype(BF16)
            a_scr[i] += _dot(vt_scr[i // 2, kb, :, :nk], p)
            out.append(jnp.maximum(gs[i], jnp.max(ss[i], axis=0, keepdims=True)))
        return ms, tuple(out)

    neg = jnp.full((1, bq), NEG_BIG, F32)
    n_full = (qi * bq) // bk
    per = bk // bq
    def run_online():
        a_scr[...] = jnp.zeros(a_scr.shape, F32)
        ms = lax.fori_loop(0, n_full, online, (neg,) * nmap)
        lax.switch(qi % per, [functools.partial(online, n_full, nkeys=(r + 1) * bq) for r in range(per)], ms)

    def diagonal_first(r):
        start = pl.multiple_of(n_full * bk + r * bq, bq)
        ms = []
        for i, s in enumerate(scores(start, bq, True)):
            m = jnp.max(s, axis=0, keepdims=True)
            a_scr[i] = _dot(vt_scr[i // 2, n_full, :, r * bq:(r + 1) * bq], jnp.exp2(s - m).astype(BF16))
            ms.append(m)
        carry = (tuple(ms), tuple(ms))
        return frozen(n_full, carry, nkeys=r * bq) if r else carry

    def run_frozen():
        carry = lax.switch(qi % per, [functools.partial(diagonal_first, r) for r in range(per)])
        ms, gs = lax.fori_loop(0, n_full, frozen, carry)
        rise = gs[0] - ms[0]
        for i in range(1, nmap):
            rise = jnp.maximum(rise, gs[i] - ms[i])
        return jnp.max(rise) <= FROZEN_MAX_RISE

    @pl.when(jnp.logical_not(run_frozen()))
    def _():
        run_online()

    lam = (jnp.exp(jnp.sum(lq1_ref[...] * lk1_ref[...], axis=-1, keepdims=True))
           - jnp.exp(jnp.sum(lq2_ref[...] * lk2_ref[...], axis=-1, keepdims=True)) + lam_init)
    for h in range(nh):
        a1, a2 = a_scr[2 * h], a_scr[2 * h + 1]
        ot = a1[:hd, :] / a1[hd:hd + 1, :] - lam * (a2[:hd, :] / a2[hd:hd + 1, :])
        o = _rms(ot.T, g_ref[...]) * (1.0 - lam_init)
        o_ref[:, h * hd:(h + 1) * hd] = o.astype(o_ref.dtype)


def _diff_attn_pairs(z, lq1, lk1, lq2, lk2, subln_g, batch, seq, lam_init, *, bq=256, bk=1024, nh=4):
    nq = seq // bq
    hd = 2 * DA_HEAD_DIM
    w = nh * hd
    vec = lambda a: a.reshape(1, -1).astype(F32)
    small = lambda n: pl.BlockSpec((1, n), lambda b, h, i: (0, 0))
    return pl.pallas_call(
        functools.partial(_diff_head_pair_kernel, bq=bq, bk=bk, nh=nh, lam_init=lam_init),
        out_shape=jax.ShapeDtypeStruct((batch * seq, DA_WIDTH), BF16),
        grid=(batch, DA_HEADS // nh, nq),
        in_specs=[
            pl.BlockSpec((bq, w), lambda b, h, i: (b * nq + i, COL_QA // w + h)),
            pl.BlockSpec((seq, w), lambda b, h, i: (b, COL_KA // w + h), pipeline_mode=pl.Buffered(1)),
            pl.BlockSpec((seq, w), lambda b, h, i: (b, COL_VA // w + h), pipeline_mode=pl.Buffered(1)),
            small(DA_HEAD_DIM), small(DA_HEAD_DIM), small(DA_HEAD_DIM), small(DA_HEAD_DIM),
            small(hd),
        ],
        out_specs=pl.BlockSpec((bq, w), lambda b, h, i: (b * nq + i, h)),
        scratch_shapes=[
            pltpu.VMEM((nh, seq // bk, hd + BF16_ROWS, bk), BF16),
            pltpu.VMEM((2 * nh, hd + BF16_ROWS, bq), F32),
        ],
        compiler_params=_params(("arbitrary", "arbitrary", "arbitrary")),
        name="diff_attn",
    )(z, z, z, vec(lq1), vec(lk1), vec(lq2), vec(lk2), vec(subln_g))


def _sb_attn_kernel(q_ref, k_ref, v_ref, u_ref, o_ref, vt_scr, a_scr, *, bq, nh):
    qi = pl.program_id(2)
    hd = SB_HEAD_DIM
    bc = bq

    @pl.when(qi == 0)
    def _():
        for h in range(nh):
            _transpose_values(v_ref.at[:, h * hd:(h + 1) * hd], vt_scr.at[h], bc)

    qts = [q_ref[:, h * hd:(h + 1) * hd].astype(F32).T.astype(BF16) for h in range(nh)]
    a_scr[...] = jnp.zeros(a_scr.shape, F32)
    u = u_ref[...]

    def pair(sb0, cs, bound):
        start = pl.multiple_of(sb0 * bc, bc)
        k = k_ref[pl.ds(start, 2 * bc), :]
        zs = [_dot(k[:, h * hd:(h + 1) * hd], qts[h]) for h in range(nh)]
        if bound is not None:
            valid = start + lax.broadcasted_iota(jnp.int32, (2 * bc, bq), 0) < bound
        out = []
        for h in range(nh):
            z = zs[h]
            sp = jnp.log2(1.0 + jnp.exp2(-jnp.abs(z)))
            log_beta = jnp.minimum(z, 0.0) - sp
            log_1m = log_beta - z
            if bound is not None:
                log_1m = jnp.where(valid, log_1m, 0.0)
            l16 = log_1m.astype(BF16)
            c = cs[h]
            tails = [None, None]
            for sb in (1, 0):
                blk = l16[sb * bc:(sb + 1) * bc, :]
                t = _dot(u, blk) + c
                tails[sb] = t
                c = t[0:1, :] + blk[0:1, :].astype(F32)
            a = jnp.exp2(log_beta + jnp.concatenate(tails, axis=0))
            if bound is not None:
                a = jnp.where(valid, a, 0.0)
            a16 = a.astype(BF16)
            a_scr[h] += _dot(vt_scr[h, sb0], a16[:bc, :]) + _dot(vt_scr[h, sb0 + 1], a16[bc:, :])
            out.append(c)
        return tuple(out)

    def live(cs):
        m = cs[0]
        for c in cs[1:]:
            m = jnp.maximum(m, c)
        return jnp.max(m) > SB_ZERO_LOG2

    first = jnp.maximum(qi - 1, 0)
    qry = qi * bq + lax.broadcasted_iota(jnp.int32, (1, bq), 1)
    cs = pair(first, (jnp.zeros((1, bq), F32),) * nh, qry)

    n_pairs = first // 2

    def more(state):
        t, cc = state
        return jnp.logical_and(t < n_pairs, live(cc))

    def walk(state):
        t, cc = state
        return t + 1, pair(first - 2 - 2 * t, cc, None)

    t_end, cs = lax.while_loop(more, walk, (jnp.int32(0), cs))

    @pl.when(jnp.logical_and(jnp.logical_and(first % 2 == 1, t_end == n_pairs), live(cs)))
    def _():
        pair(0, cs, jnp.full((1, bq), bc, jnp.int32))

    for h in range(nh):
        o_ref[:, h * hd:(h + 1) * hd] = a_scr[h].T.astype(o_ref.dtype)


def _sb_attn(z, batch, seq, *, bq=256, nh=4):
    nq = seq // bq
    hd = SB_HEAD_DIM
    w = nh * hd
    r = lax.broadcasted_iota(jnp.int32, (bq, bq), 0)
    c = lax.broadcasted_iota(jnp.int32, (bq, bq), 1)
    u = (c > r).astype(BF16)
    return pl.pallas_call(
        functools.partial(_sb_attn_kernel, bq=bq, nh=nh),
        out_shape=jax.ShapeDtypeStruct((batch * seq, SB_WIDTH), BF16),
        grid=(batch, SB_HEADS // nh, nq),
        in_specs=[
            pl.BlockSpec((bq, w), lambda b, h, i: (b * nq + i, COL_QB // w + h)),
            pl.BlockSpec((seq, w), lambda b, h, i: (b, COL_KB // w + h), pipeline_mode=pl.Buffered(1)),
            pl.BlockSpec((seq, w), lambda b, h, i: (b, COL_VB // w + h), pipeline_mode=pl.Buffered(1)),
            pl.BlockSpec((bq, bq), lambda b, h, i: (0, 0)),
        ],
        out_specs=pl.BlockSpec((bq, w), lambda b, h, i: (b * nq + i, h)),
        scratch_shapes=[pltpu.VMEM((nh, seq // bq, hd, bq), BF16), pltpu.VMEM((nh, hd, bq), F32)],
        compiler_params=_params(("arbitrary", "arbitrary", "arbitrary")),
        name="sb_attn",
    )(z, z, z, u)


def _sigmoid(x):
    return 1.0 / (1.0 + jnp.exp(-x))


def _merge_out_kernel(oa_ref, ob_ref, ga_ref, gb_ref, wa_ref, wb_ref, wo_ref, x_ref, o_ref):
    ya = _dot(oa_ref[...], wa_ref[...])
    yb = _dot(ob_ref[...], wb_ref[...])
    ga = _sigmoid(ga_ref[...].astype(F32))
    gb = _sigmoid(gb_ref[...].astype(F32))
    merged = (ga * ya + gb * yb).astype(BF16)
    o_ref[...] = x_ref[...] + _dot(merged, wo_ref[...])


def _merge_out(oa, ob, z, wa, wb, wo, x2d, *, bm=512):
    m, d = x2d.shape
    resident = lambda shape: pl.BlockSpec(shape, lambda i: (0, 0), pipeline_mode=pl.Buffered(1))
    return pl.pallas_call(
        _merge_out_kernel,
        out_shape=jax.ShapeDtypeStruct((m, d), F32),
        grid=(m // bm,),
        in_specs=[
            pl.BlockSpec((bm, DA_WIDTH), lambda i: (i, 0)),
            pl.BlockSpec((bm, SB_WIDTH), lambda i: (i, 0)),
            pl.BlockSpec((bm, d), lambda i: (i, COL_GA // d)),
            pl.BlockSpec((bm, d), lambda i: (i, COL_GB // d)),
            resident((DA_WIDTH, d)), resident((SB_WIDTH, d)), resident((d, d)),
            pl.BlockSpec((bm, d), lambda i: (i, 0)),
        ],
        out_specs=pl.BlockSpec((bm, d), lambda i: (i, 0)),
        compiler_params=_params(("parallel",), VMEM_LIMIT_BIG),
        name="merge_out",
    )(oa, ob, z, z, wa, wb, wo, x2d)


def _mem_kv_kernel(x_ref, g_ref, w_ref, o_ref):
    h = _rms(x_ref[...], g_ref[...]).astype(BF16)
    o_ref[...] = _dot(h, w_ref[...]).astype(o_ref.dtype)


def _mem_kv(mem2d, g, w, *, bm=256):
    m, d = mem2d.shape
    n = w.shape[1]
    return pl.pallas_call(
        _mem_kv_kernel,
        out_shape=jax.ShapeDtypeStruct((m, n), BF16),
        grid=(m // bm,),
        in_specs=[
            pl.BlockSpec((bm, d), lambda i: (i, 0)),
            pl.BlockSpec((1, d), lambda i: (0, 0)),
            pl.BlockSpec((d, n), lambda i: (0, 0)),
        ],
        out_specs=pl.BlockSpec((bm, n), lambda i: (i, 0)),
        compiler_params=_params(("parallel",)),
        name="mem_kv",
    )(mem2d, g.reshape(1, d), w)


def _xattn_kernel(x_ref, g_ref, wq_ref, kv_ref, wo_ref, o_ref, oh_scr):
    x = x_ref[...]
    h = _rms(x, g_ref[...]).astype(BF16)
    q = (_dot(h, wq_ref[...]) * (XA_HEAD_DIM ** -0.5)).astype(BF16)
    for hh in range(XA_HEADS):
        lo = hh * XA_HEAD_DIM
        k = kv_ref[:, lo:lo + XA_HEAD_DIM]
        v = kv_ref[:, XA_WIDTH + lo:XA_WIDTH + lo + XA_HEAD_DIM]
        s = _dot_nt(q[:, lo:lo + XA_HEAD_DIM], k)
        p = jnp.exp(s - jnp.max(s, axis=-1, keepdims=True))
        p = p / jnp.sum(p, axis=-1, keepdims=True)
        oh_scr[:, lo:lo + XA_HEAD_DIM] = _dot(p.astype(BF16), v).astype(BF16)
    o_ref[...] = x + _dot(oh_scr[...], wo_ref[...])


def _xattn(x2d, g, wq, kv, wo, seq, *, bm=1024):
    m, d = x2d.shape
    nseq = seq // bm
    return pl.pallas_call(
        _xattn_kernel,
        out_shape=jax.ShapeDtypeStruct((m, d), F32),
        grid=(m // bm,),
        in_specs=[
            pl.BlockSpec((bm, d), lambda i: (i, 0)),
            pl.BlockSpec((1, d), lambda i: (0, 0)),
            pl.BlockSpec((d, XA_WIDTH), lambda i: (0, 0)),
            pl.BlockSpec((N_MEM, 2 * XA_WIDTH), lambda i: (i // nseq, 0)),
            pl.BlockSpec((XA_WIDTH, d), lambda i: (0, 0)),
        ],
        out_specs=pl.BlockSpec((bm, d), lambda i: (i, 0)),
        scratch_shapes=[pltpu.VMEM((bm, XA_WIDTH), BF16)],
        compiler_params=_params(("parallel",)),
        name="xattn",
    )(x2d, g.reshape(1, d), wq, kv, wo)


def _ffn_kernel(x_ref, halo_ref, g_ref, wg_ref, wv_ref, cwg_ref, cwv_ref, cbg_ref, cbv_ref,
                wd_ref, fg_ref, o_ref, h_scr, *, nseq, final):
    i = pl.program_id(0)
    f = pl.program_id(1)
    pad = SUBLANES

    @pl.when(f == 0)
    def _():
        g = g_ref[...]
        h_scr[pad:, :] = _rms(x_ref[...], g).astype(BF16)
        keep = (i % nseq != 0).astype(F32)
        h_scr[:pad, :] = (_rms(halo_ref[...], g) * keep).astype(BF16)
        o_ref[...] = jnp.zeros(o_ref.shape, F32)

    h = h_scr[...]

    def conv(w_ref, cw_ref, cb_ref):
        u = _dot(h, w_ref[...])
        cw = cw_ref[...]
        y = (cw[2:3, :] * u + cw[1:2, :] * pltpu.roll(u, 1, 0) + cw[0:1, :] * pltpu.roll(u, 2, 0))
        return y[pad:, :] + cb_ref[...]

    gate = conv(wg_ref, cwg_ref, cbg_ref)
    val = conv(wv_ref, cwv_ref, cbv_ref)
    act = (gate * _sigmoid(gate) * val).astype(BF16)
    o_ref[...] += _dot(act, wd_ref[...])

    @pl.when(f == pl.num_programs(1) - 1)
    def _():
        y = x_ref[...] + o_ref[...]
        o_ref[...] = _rms(y, fg_ref[...]) if final else y


def _ffn(x2d, g, w_up, conv_w, conv_b, w_down, final_g, seq, *, final, bm=1024, bf=512):
    m, d = x2d.shape
    nf = D_FF // bf
    nseq = seq // bm
    hb = bm // SUBLANES
    return pl.pallas_call(
        functools.partial(_ffn_kernel, nseq=nseq, final=final),
        out_shape=jax.ShapeDtypeStruct((m, d), F32),
        grid=(m // bm, nf),
        in_specs=[
            pl.BlockSpec((bm, d), lambda i, f: (i, 0), pipeline_mode=pl.Buffered(1)),
            pl.BlockSpec((SUBLANES, d), lambda i, f: (jnp.maximum(i * hb - 1, 0), 0)),
            pl.BlockSpec((1, d), lambda i, f: (0, 0)),
            pl.BlockSpec((d, bf), lambda i, f: (0, f)),
            pl.BlockSpec((d, bf), lambda i, f: (0, nf + f)),
            pl.BlockSpec((CONV_WIDTH, bf), lambda i, f: (0, f)),
            pl.BlockSpec((CONV_WIDTH, bf), lambda i, f: (0, nf + f)),
            pl.BlockSpec((1, bf), lambda i, f: (0, f)),
            pl.BlockSpec((1, bf), lambda i, f: (0, nf + f)),
            pl.BlockSpec((bf, d), lambda i, f: (f, 0)),
            pl.BlockSpec((1, d), lambda i, f: (0, 0)),
        ],
        out_specs=pl.BlockSpec((bm, d), lambda i, f: (i, 0)),
        scratch_shapes=[pltpu.VMEM((bm + SUBLANES, d), BF16)],
        compiler_params=_params(("parallel", "arbitrary"), VMEM_LIMIT_BIG),
        name="ffn",
    )(x2d, x2d, g.reshape(1, d), w_up, w_up, conv_w, conv_w,
      conv_b.reshape(1, -1), conv_b.reshape(1, -1), w_down, final_g.reshape(1, d))


def kernel(x, mem, norm_mix_g, w_in, lambda_q1, lambda_k1, lambda_q2, lambda_k2, da_subln_g,
           w_proj_a, w_proj_b, w_out, norm_x_g, norm_mem_g, w_xq, w_xkv, w_xo, norm_ffn_g,
           w_up, conv_w, conv_b, w_down, final_norm_g):
    batch, seq, d = x.shape
    depth = w_in.shape[0]
    assert d == D_MODEL and mem.shape[1:] == (N_MEM, D_MODEL) and w_in.shape[1:] == (D_MODEL, N_IN)
    assert seq % TRANSPOSE_KEYS == 0, "token tiles and attention key blocks are 1024 long"
    bf = lambda a: a.astype(BF16)
    x2d = x.reshape(batch * seq, d)
    mem2d = mem.reshape(batch * mem.shape[1], d)
    for l in range(depth):
        lam_init = 0.8 - 0.6 * math.exp(-0.3 * l)
        z = _in_proj(x2d, norm_mix_g[l], bf(w_in[l]), seq)
        oa = _diff_attn_pairs(z, lambda_q1[l], lambda_k1[l], lambda_q2[l], lambda_k2[l],
                              da_subln_g[l], batch, seq, lam_init)
        ob = _sb_attn(z, batch, seq)
        x2d = _merge_out(oa, ob, z, bf(w_proj_a[l]), bf(w_proj_b[l]), bf(w_out[l]), x2d)
        kv = _mem_kv(mem2d, norm_mem_g[l], bf(w_xkv[l]))
        x2d = _xattn(x2d, norm_x_g[l], bf(w_xq[l]), kv, bf(w_xo[l]), seq)
        x2d = _ffn(x2d, norm_ffn_g[l], bf(w_up[l]), conv_w[l], conv_b[l], bf(w_down[l]),
                   final_norm_g, seq, final=(l == depth - 1))
    return x2d.reshape(batch, seq, d)
```

```python
import functools
import math

import jax
import jax.numpy as jnp
from jax import lax
from jax.experimental import pallas as pl
from jax.experimental.pallas import tpu as pltpu

F32 = jnp.float32
BF16 = jnp.bfloat16

D_MODEL = 2048
N_MEM = 256
DA_HEAD_DIM = 64
DA_WIDTH = D_MODEL // 2
DA_HEADS = DA_WIDTH // (2 * DA_HEAD_DIM)
DA_ROT_DIM = DA_HEAD_DIM // 4
SB_HEAD_DIM = 128
SB_WIDTH = D_MODEL // 2
SB_HEADS = SB_WIDTH // SB_HEAD_DIM
XA_HEADS = 4
XA_HEAD_DIM = 128
XA_WIDTH = XA_HEADS * XA_HEAD_DIM
D_FF = 256 * ((8 * D_MODEL // 3 + 255) // 256)
CONV_WIDTH = 3
ROPE_THETA = 500000.0
EPS = 1e-6
N_IN = 3 * DA_WIDTH + 3 * SB_WIDTH + 2 * D_MODEL

LANES = 128
SUBLANES = 8
BF16_ROWS = 16
NEG_BIG = -1e30
LOG2E = math.log2(math.e)
SB_ZERO_LOG2 = -160.0
TRANSPOSE_KEYS = 1024
FROZEN_MAX_RISE = 64.0
VMEM_LIMIT = 48 * 1024 * 1024
VMEM_LIMIT_BIG = 60 * 1024 * 1024

COL_QA, COL_KA, COL_VA = 0, DA_WIDTH, 2 * DA_WIDTH
COL_QB, COL_KB, COL_VB = 3 * DA_WIDTH, 3 * DA_WIDTH + SB_WIDTH, 3 * DA_WIDTH + 2 * SB_WIDTH
COL_GA = 3 * DA_WIDTH + 3 * SB_WIDTH
COL_GB = COL_GA + D_MODEL


def _params(sem, vmem_limit=VMEM_LIMIT):
    return pltpu.CompilerParams(dimension_semantics=sem, vmem_limit_bytes=vmem_limit)


def _rms(x, g):
    return x * lax.rsqrt(jnp.mean(x * x, axis=-1, keepdims=True) + EPS) * g


def _dot(a, b):
    return jnp.dot(a, b, preferred_element_type=F32)


def _dot_nt(a, b):
    return lax.dot_general(a, b, (((1,), (1,)), ((), ())), preferred_element_type=F32)


def _in_proj_kernel(x_ref, g_ref, w_ref, scale_ref, cos_ref, sp_ref, sm_ref, o_ref, h_scr, *, bn):
    j = pl.program_id(1)

    @pl.when(j == 0)
    def _():
        h_scr[...] = _rms(x_ref[...], g_ref[...]).astype(BF16)

    acc = _dot(h_scr[...], w_ref[...]) * scale_ref[...]
    col = j * bn

    @pl.when(col < COL_VA)
    def _():
        cos, sp, sm = cos_ref[...], sp_ref[...], sm_ref[...]
        for c in range(bn // LANES):
            a = acc[:, c * LANES:(c + 1) * LANES]
            r = a * cos + pltpu.roll(a, DA_ROT_DIM // 2, 1) * sp \
                + pltpu.roll(a, LANES - DA_ROT_DIM // 2, 1) * sm
            o_ref[:, c * LANES:(c + 1) * LANES] = r.astype(o_ref.dtype)

    @pl.when(col >= COL_VA)
    def _():
        o_ref[...] = acc.astype(o_ref.dtype)


def _rope_lane_tables(seq):
    half = DA_ROT_DIM // 2
    inv = ROPE_THETA ** (-jnp.arange(0, DA_ROT_DIM, 2, dtype=F32) / DA_ROT_DIM)
    ang = jnp.arange(seq, dtype=F32)[:, None] * inv[None, :]
    cos, sin = jnp.cos(ang), jnp.sin(ang)
    lane = jnp.arange(LANES) % DA_HEAD_DIM
    idx = lane % half
    lo = (lane < half)[None, :]
    hi = jnp.logical_and(lane >= half, lane < 2 * half)[None, :]
    cos_t = jnp.where(jnp.logical_or(lo, hi), cos[:, idx], 1.0)
    sp_t = jnp.where(hi, sin[:, idx], 0.0)
    sm_t = jnp.where(lo, -sin[:, idx], 0.0)
    return cos_t.astype(F32), sp_t.astype(F32), sm_t.astype(F32)


def _in_proj(x2d, g, w, seq, *, bm=1024, bn=2048):
    m, d = x2d.shape
    n = w.shape[1]
    cos_t, sp_t, sm_t = _rope_lane_tables(seq)
    col = jnp.arange(n)
    scale = jnp.where(col < COL_KA, LOG2E * DA_HEAD_DIM ** -0.5,
                      jnp.where((col >= COL_QB) & (col < COL_KB), LOG2E * SB_HEAD_DIM ** -0.5, 1.0))
    scale = scale.astype(F32).reshape(1, n)
    nseq = seq // bm
    tab_spec = pl.BlockSpec((bm, LANES), lambda i, j: (i % nseq, 0))
    return pl.pallas_call(
        functools.partial(_in_proj_kernel, bn=bn),
        out_shape=jax.ShapeDtypeStruct((m, n), BF16),
        grid=(m // bm, n // bn),
        in_specs=[
            pl.BlockSpec((bm, d), lambda i, j: (i, 0)),
            pl.BlockSpec((1, d), lambda i, j: (0, 0)),
            pl.BlockSpec((d, bn), lambda i, j: (0, j)),
            pl.BlockSpec((1, bn), lambda i, j: (0, j)),
            tab_spec, tab_spec, tab_spec,
        ],
        out_specs=pl.BlockSpec((bm, bn), lambda i, j: (i, j)),
        scratch_shapes=[pltpu.VMEM((bm, d), BF16)],
        compiler_params=_params(("parallel", "arbitrary"), VMEM_LIMIT_BIG),
        name="in_proj",
    )(x2d, g.reshape(1, d), w, scale, cos_t, sp_t, sm_t)


def _transpose_values(v_ref, vt_scr, bk):
    hd = v_ref.shape[1]
    extra = vt_scr.shape[1] - hd
    per = max(1, TRANSPOSE_KEYS // bk)
    chunk = per * bk

    def body(c, carry):
        start = pl.multiple_of(c * chunk, chunk)
        vt = v_ref[pl.ds(start, chunk), :].astype(F32).T.astype(BF16)
        for s in range(per):
            vt_scr[c * per + s, :hd, :] = vt[:, s * bk:(s + 1) * bk]
            if extra:
                vt_scr[c * per + s, hd:, :] = jnp.ones((extra, bk), BF16)
        return carry
    lax.fori_loop(0, vt_scr.shape[0] // per, body, 0)


def _diff_head_pair_kernel(q_ref, k_ref, v_ref, lq1_ref, lk1_ref, lq2_ref, lk2_ref, g_ref, o_ref,
                           vt_scr, a_scr, *, bq, bk, nh, lam_init):
    qi = pl.program_id(2)
    hd = 2 * DA_HEAD_DIM
    nmap = 2 * nh

    @pl.when(qi == 0)
    def _():
        for h in range(nh):
            _transpose_values(v_ref.at[:, h * hd:(h + 1) * hd], vt_scr.at[h], bk)

    qts = []
    for h in range(nh):
        qt = q_ref[:, h * hd:(h + 1) * hd].astype(F32).T
        feat = lax.broadcasted_iota(jnp.int32, qt.shape, 0)
        qts.append(jnp.where(feat < DA_HEAD_DIM, qt, 0.0).astype(BF16))
        qts.append(jnp.where(feat >= DA_HEAD_DIM, qt, 0.0).astype(BF16))
    def scores(start, nk, causal):
        k = k_ref[pl.ds(start, nk), :]
        ss = [_dot(k[:, (i // 2) * hd:(i // 2 + 1) * hd], qts[i]) for i in range(nmap)]
        if causal:
            key = start + lax.broadcasted_iota(jnp.int32, (nk, bq), 0)
            qry = qi * bq + lax.broadcasted_iota(jnp.int32, (nk, bq), 1)
            keep = key <= qry
            ss = [jnp.where(keep, s, NEG_BIG) for s in ss]
        return ss


    def online(kb, ms, nkeys=None):
        nk = bk if nkeys is None else nkeys
        ss = scores(pl.multiple_of(kb * bk, bk), nk, nkeys is not None)
        out = []
        for i in range(nmap):
            m_new = jnp.maximum(ms[i], jnp.max(ss[i], axis=0, keepdims=True))
            alpha = jnp.exp2(ms[i] - m_new)
            p = jnp.exp2(ss[i] - m_new).astype(BF16)
            a_scr[i] = alpha * a_scr[i] + _dot(vt_scr[i // 2, kb, :, :nk], p)
            out.append(m_new)
        return tuple(out)

    def frozen(kb, carry, nkeys=None):
        ms, gs = carry
        nk = bk if nkeys is None else nkeys
        ss = scores(pl.multiple_of(kb * bk, bk), nk, False)
        out = []
        for i in range(nmap):
            p = jnp.exp2(ss[i] - ms[i]).astype(BF16)
            a_scr[i] += _dot(vt_scr[i // 2, kb, :, :nk], p)
            out.append(jnp.maximum(gs[i], jnp.max(ss[i], axis=0, keepdims=True)))
        return ms, tuple(out)

    neg = jnp.full((1, bq), NEG_BIG, F32)
    n_full = (qi * bq) // bk
    per = bk // bq
    def run_online():
        a_scr[...] = jnp.zeros(a_scr.shape, F32)
        ms = lax.fori_loop(0, n_full, online, (neg,) * nmap)
        lax.switch(qi % per, [functools.partial(online, n_full, nkeys=(r + 1) * bq) for r in range(per)], ms)

    def diagonal_first(r):
        start = pl.multiple_of(n_full * bk + r * bq, bq)
        ms = []
        for i, s in enumerate(scores(start, bq, True)):
            m = jnp.max(s, axis=0, keepdims=True)
            a_scr[i] = _dot(vt_scr[i // 2, n_full, :, r * bq:(r + 1) * bq], jnp.exp2(s - m).astype(BF16))
            ms.append(m)
        carry = (tuple(ms), tuple(ms))
        return frozen(n_full, carry, nkeys=r * bq) if r else carry

    def run_frozen():
        carry = lax.switch(qi % per, [functools.partial(diagonal_first, r) for r in range(per)])
        ms, gs = lax.fori_loop(0, n_full, frozen, carry)
        rise = gs[0] - ms[0]
        for i in range(1, nmap):
            rise = jnp.maximum(rise, gs[i] - ms[i])
        return jnp.max(rise) <= FROZEN_MAX_RISE

    @pl.when(jnp.logical_not(run_frozen()))
    def _():
        run_online()

    lam = (jnp.exp(jnp.sum(lq1_ref[...] * lk1_ref[...], axis=-1, keepdims=True))
           - jnp.exp(jnp.sum(lq2_ref[...] * lk2_ref[...], axis=-1, keepdims=True)) + lam_init)
    for h in range(nh):
        a1, a2 = a_scr[2 * h], a_scr[2 * h + 1]
        ot = a1[:hd, :] * (1.0 / a1[hd:hd + 1, :]) - (lam / a2[hd:hd + 1, :]) * a2[:hd, :]
        ot = ot * lax.rsqrt(jnp.mean(ot * ot, axis=0, keepdims=True) + EPS)
        o = ot.T * (g_ref[...] * (1.0 - lam_init))
        o_ref[:, h * hd:(h + 1) * hd] = o.astype(o_ref.dtype)


def _diff_attn_pairs(z, lq1, lk1, lq2, lk2, subln_g, batch, seq, lam_init, *, bq=256, bk=1024, nh=4):
    nq = seq // bq
    hd = 2 * DA_HEAD_DIM
    w = nh * hd
    vec = lambda a: a.reshape(1, -1).astype(F32)
    small = lambda n: pl.BlockSpec((1, n), lambda b, h, i: (0, 0))
    return pl.pallas_call(
        functools.partial(_diff_head_pair_kernel, bq=bq, bk=bk, nh=nh, lam_init=lam_init),
        out_shape=jax.ShapeDtypeStruct((batch * seq, DA_WIDTH), BF16),
        grid=(batch, DA_HEADS // nh, nq),
        in_specs=[
            pl.BlockSpec((bq, w), lambda b, h, i: (b * nq + i, COL_QA // w + h)),
            pl.BlockSpec((seq, w), lambda b, h, i: (b, COL_KA // w + h), pipeline_mode=pl.Buffered(1)),
            pl.BlockSpec((seq, w), lambda b, h, i: (b, COL_VA // w + h), pipeline_mode=pl.Buffered(1)),
            small(DA_HEAD_DIM), small(DA_HEAD_DIM), small(DA_HEAD_DIM), small(DA_HEAD_DIM),
            small(hd),
        ],
        out_specs=pl.BlockSpec((bq, w), lambda b, h, i: (b * nq + i, h)),
        scratch_shapes=[
            pltpu.VMEM((nh, seq // bk, hd + BF16_ROWS, bk), BF16),
            pltpu.VMEM((2 * nh, hd + BF16_ROWS, bq), F32),
        ],
        compiler_params=_params(("arbitrary", "arbitrary", "arbitrary")),
        name="diff_attn",
    )(z, z, z, vec(lq1), vec(lk1), vec(lq2), vec(lk2), vec(subln_g))


def _sb_attn_kernel(q_ref, k_ref, v_ref, u_ref, o_ref, vt_scr, a_scr, *, bq, nh):
    qi = pl.program_id(2)
    hd = SB_HEAD_DIM
    bc = bq

    @pl.when(qi == 0)
    def _():
        for h in range(nh):
            _transpose_values(v_ref.at[:, h * hd:(h + 1) * hd], vt_scr.at[h], bc)

    qts = [q_ref[:, h * hd:(h + 1) * hd].astype(F32).T.astype(BF16) for h in range(nh)]
    a_scr[...] = jnp.zeros(a_scr.shape, F32)
    u = u_ref[...]

    def pair(sb0, cs, bound):
        start = pl.multiple_of(sb0 * bc, bc)
        k = k_ref[pl.ds(start, 2 * bc), :]
        zs = [_dot(k[:, h * hd:(h + 1) * hd], qts[h]) for h in range(nh)]
        if bound is not None:
            valid = start + lax.broadcasted_iota(jnp.int32, (2 * bc, bq), 0) < bound
        out = []
        for h in range(nh):
            z = zs[h]
            sp = jnp.log2(1.0 + jnp.exp2(-jnp.abs(z)))
            log_beta = jnp.minimum(z, 0.0) - sp
            log_1m = log_beta - z
            if bound is not None:
                log_1m = jnp.where(valid, log_1m, 0.0)
            l16 = log_1m.astype(BF16)
            c = cs[h]
            tails = [None, None]
            for sb in (1, 0):
                blk = l16[sb * bc:(sb + 1) * bc, :]
                t = _dot(u, blk) + c
                tails[sb] = t
                c = t[0:1, :] + blk[0:1, :].astype(F32)
            a = jnp.exp2(log_beta + jnp.concatenate(tails, axis=0))
            if bound is not None:
                a = jnp.where(valid, a, 0.0)
            a16 = a.astype(BF16)
            a_scr[h] += _dot(vt_scr[h, sb0], a16[:bc, :]) + _dot(vt_scr[h, sb0 + 1], a16[bc:, :])
            out.append(c)
        return tuple(out)

    def live(cs):
        m = cs[0]
        for c in cs[1:]:
            m = jnp.maximum(m, c)
        return jnp.max(m) > SB_ZERO_LOG2

    first = jnp.maximum(qi - 1, 0)
    qry = qi * bq + lax.broadcasted_iota(jnp.int32, (1, bq), 1)
    cs = pair(first, (jnp.zeros((1, bq), F32),) * nh, qry)

    n_pairs = first // 2

    def more(state):
        t, cc = state
        return jnp.logical_and(t < n_pairs, live(cc))

    def walk(state):
        t, cc = state
        return t + 1, pair(first - 2 - 2 * t, cc, None)

    t_end, cs = lax.while_loop(more, walk, (jnp.int32(0), cs))

    @pl.when(jnp.logical_and(jnp.logical_and(first % 2 == 1, t_end == n_pairs), live(cs)))
    def _():
        pair(0, cs, jnp.full((1, bq), bc, jnp.int32))

    for h in range(nh):
        o_ref[:, h * hd:(h + 1) * hd] = a_scr[h].T.astype(o_ref.dtype)


def _sb_attn(z, batch, seq, *, bq=256, nh=4):
    nq = seq // bq
    hd = SB_HEAD_DIM
    w = nh * hd
    r = lax.broadcasted_iota(jnp.int32, (bq, bq), 0)
    c = lax.broadcasted_iota(jnp.int32, (bq, bq), 1)
    u = (c > r).astype(BF16)
    return pl.pallas_call(
        functools.partial(_sb_attn_kernel, bq=bq, nh=nh),
        out_shape=jax.ShapeDtypeStruct((batch * seq, SB_WIDTH), BF16),
        grid=(batch, SB_HEADS // nh, nq),
        in_specs=[
            pl.BlockSpec((bq, w), lambda b, h, i: (b * nq + i, COL_QB // w + h)),
            pl.BlockSpec((seq, w), lambda b, h, i: (b, COL_KB // w + h), pipeline_mode=pl.Buffered(1)),
            pl.BlockSpec((seq, w), lambda b, h, i: (b, COL_VB // w + h), pipeline_mode=pl.Buffered(1)),
            pl.BlockSpec((bq, bq), lambda b, h, i: (0, 0)),
        ],
        out_specs=pl.BlockSpec((bq, w), lambda b, h, i: (b * nq + i, h)),
        scratch_shapes=[pltpu.VMEM((nh, seq // bq, hd, bq), BF16), pltpu.VMEM((nh, hd, bq), F32)],
        compiler_params=_params(("arbitrary", "arbitrary", "arbitrary")),
        name="sb_attn",
    )(z, z, z, u)


def _sigmoid(x):
    return 1.0 / (1.0 + jnp.exp(-x))


def _merge_out_kernel(oa_ref, ob_ref, ga_ref, gb_ref, wa_ref, wb_ref, wo_ref, x_ref, o_ref):
    ya = _dot(oa_ref[...], wa_ref[...])
    yb = _dot(ob_ref[...], wb_ref[...])
    ga = _sigmoid(ga_ref[...].astype(F32))
    gb = _sigmoid(gb_ref[...].astype(F32))
    merged = (ga * ya + gb * yb).astype(BF16)
    o_ref[...] = x_ref[...] + _dot(merged, wo_ref[...])


def _merge_out(oa, ob, z, wa, wb, wo, x2d, *, bm=512):
    m, d = x2d.shape
    resident = lambda shape: pl.BlockSpec(shape, lambda i: (0, 0), pipeline_mode=pl.Buffered(1))
    return pl.pallas_call(
        _merge_out_kernel,
        out_shape=jax.ShapeDtypeStruct((m, d), F32),
        grid=(m // bm,),
        in_specs=[
            pl.BlockSpec((bm, DA_WIDTH), lambda i: (i, 0)),
            pl.BlockSpec((bm, SB_WIDTH), lambda i: (i, 0)),
            pl.BlockSpec((bm, d), lambda i: (i, COL_GA // d)),
            pl.BlockSpec((bm, d), lambda i: (i, COL_GB // d)),
            resident((DA_WIDTH, d)), resident((SB_WIDTH, d)), resident((d, d)),
            pl.BlockSpec((bm, d), lambda i: (i, 0)),
        ],
        out_specs=pl.BlockSpec((bm, d), lambda i: (i, 0)),
        compiler_params=_params(("parallel",), VMEM_LIMIT_BIG),
        name="merge_out",
    )(oa, ob, z, z, wa, wb, wo, x2d)


def _mem_kv_kernel(x_ref, g_ref, w_ref, o_ref):
    h = _rms(x_ref[...], g_ref[...]).astype(BF16)
    o_ref[...] = _dot(h, w_ref[...]).astype(o_ref.dtype)


def _mem_kv(mem2d, g, w, *, bm=256):
    m, d = mem2d.shape
    n = w.shape[1]
    return pl.pallas_call(
        _mem_kv_kernel,
        out_shape=jax.ShapeDtypeStruct((m, n), BF16),
        grid=(m // bm,),
        in_specs=[
            pl.BlockSpec((bm, d), lambda i: (i, 0)),
            pl.BlockSpec((1, d), lambda i: (0, 0)),
            pl.BlockSpec((d, n), lambda i: (0, 0)),
        ],
        out_specs=pl.BlockSpec((bm, n), lambda i: (i, 0)),
        compiler_params=_params(("parallel",)),
        name="mem_kv",
    )(mem2d, g.reshape(1, d), w)


def _xattn_kernel(x_ref, g_ref, wq_ref, kv_ref, wo_ref, o_ref, oh_scr):
    x = x_ref[...]
    h = _rms(x, g_ref[...]).astype(BF16)
    q = (_dot(h, wq_ref[...]) * (XA_HEAD_DIM ** -0.5)).astype(BF16)
    for hh in range(XA_HEADS):
        lo = hh * XA_HEAD_DIM
        k = kv_ref[:, lo:lo + XA_HEAD_DIM]
        v = kv_ref[:, XA_WIDTH + lo:XA_WIDTH + lo + XA_HEAD_DIM]
        s = _dot_nt(q[:, lo:lo + XA_HEAD_DIM], k)
        p = jnp.exp(s - jnp.max(s, axis=-1, keepdims=True))
        p = p / jnp.sum(p, axis=-1, keepdims=True)
        oh_scr[:, lo:lo + XA_HEAD_DIM] = _dot(p.astype(BF16), v).astype(BF16)
    o_ref[...] = x + _dot(oh_scr[...], wo_ref[...])


def _xattn(x2d, g, wq, kv, wo, seq, *, bm=1024):
    m, d = x2d.shape
    nseq = seq // bm
    return pl.pallas_call(
        _xattn_kernel,
        out_shape=jax.ShapeDtypeStruct((m, d), F32),
        grid=(m // bm,),
        in_specs=[
            pl.BlockSpec((bm, d), lambda i: (i, 0)),
            pl.BlockSpec((1, d), lambda i: (0, 0)),
            pl.BlockSpec((d, XA_WIDTH), lambda i: (0, 0)),
            pl.BlockSpec((N_MEM, 2 * XA_WIDTH), lambda i: (i // nseq, 0)),
            pl.BlockSpec((XA_WIDTH, d), lambda i: (0, 0)),
        ],
        out_specs=pl.BlockSpec((bm, d), lambda i: (i, 0)),
        scratch_shapes=[pltpu.VMEM((bm, XA_WIDTH), BF16)],
        compiler_params=_params(("parallel",)),
        name="xattn",
    )(x2d, g.reshape(1, d), wq, kv, wo)


def _ffn_kernel(x_ref, halo_ref, g_ref, wg_ref, wv_ref, cwg_ref, cwv_ref, cbg_ref, cbv_ref,
                wd_ref, fg_ref, o_ref, h_scr, *, nseq, final):
    i = pl.program_id(0)
    f = pl.program_id(1)
    pad = SUBLANES

    @pl.when(f == 0)
    def _():
        g = g_ref[...]
        h_scr[pad:, :] = _rms(x_ref[...], g).astype(BF16)
        keep = (i % nseq != 0).astype(F32)
        h_scr[:pad, :] = (_rms(halo_ref[...], g) * keep).astype(BF16)
        o_ref[...] = jnp.zeros(o_ref.shape, F32)

    h = h_scr[...]

    def conv(w_ref, cw_ref, cb_ref):
        u = _dot(h, w_ref[...])
        cw = cw_ref[...]
        y = (cw[2:3, :] * u + cw[1:2, :] * pltpu.roll(u, 1, 0) + cw[0:1, :] * pltpu.roll(u, 2, 0))
        return y[pad:, :] + cb_ref[...]

    gate = conv(wg_ref, cwg_ref, cbg_ref)
    val = conv(wv_ref, cwv_ref, cbv_ref)
    act = (gate * _sigmoid(gate) * val).astype(BF16)
    o_ref[...] += _dot(act, wd_ref[...])

    @pl.when(f == pl.num_programs(1) - 1)
    def _():
        y = x_ref[...] + o_ref[...]
        o_ref[...] = _rms(y, fg_ref[...]) if final else y


def _ffn(x2d, g, w_up, conv_w, conv_b, w_down, final_g, seq, *, final, bm=1024, bf=512):
    m, d = x2d.shape
    nf = D_FF // bf
    nseq = seq // bm
    hb = bm // SUBLANES
    return pl.pallas_call(
        functools.partial(_ffn_kernel, nseq=nseq, final=final),
        out_shape=jax.ShapeDtypeStruct((m, d), F32),
        grid=(m // bm, nf),
        in_specs=[
            pl.BlockSpec((bm, d), lambda i, f: (i, 0), pipeline_mode=pl.Buffered(1)),
            pl.BlockSpec((SUBLANES, d), lambda i, f: (jnp.maximum(i * hb - 1, 0), 0)),
            pl.BlockSpec((1, d), lambda i, f: (0, 0)),
            pl.BlockSpec((d, bf), lambda i, f: (0, f)),
            pl.BlockSpec((d, bf), lambda i, f: (0, nf + f)),
            pl.BlockSpec((CONV_WIDTH, bf), lambda i, f: (0, f)),
            pl.BlockSpec((CONV_WIDTH, bf), lambda i, f: (0, nf + f)),
            pl.BlockSpec((1, bf), lambda i, f: (0, f)),
            pl.BlockSpec((1, bf), lambda i, f: (0, nf + f)),
            pl.BlockSpec((bf, d), lambda i, f: (f, 0)),
            pl.BlockSpec((1, d), lambda i, f: (0, 0)),
        ],
        out_specs=pl.BlockSpec((bm, d), lambda i, f: (i, 0)),
        scratch_shapes=[pltpu.VMEM((bm + SUBLANES, d), BF16)],
        compiler_params=_params(("parallel", "arbitrary"), VMEM_LIMIT_BIG),
        name="ffn",
    )(x2d, x2d, g.reshape(1, d), w_up, w_up, conv_w, conv_w,
      conv_b.reshape(1, -1), conv_b.reshape(1, -1), w_down, final_g.reshape(1, d))


def kernel(x, mem, norm_mix_g, w_in, lambda_q1, lambda_k1, lambda_q2, lambda_k2, da_subln_g,
           w_proj_a, w_proj_b, w_out, norm_x_g, norm_mem_g, w_xq, w_xkv, w_xo, norm_ffn_g,
           w_up, conv_w, conv_b, w_down, final_norm_g):
    batch, seq, d = x.shape
    depth = w_in.shape[0]
    assert d == D_MODEL and mem.shape[1:] == (N_MEM, D_MODEL) and w_in.shape[1:] == (D_MODEL, N_IN)
    assert seq % TRANSPOSE_KEYS == 0, "token tiles and attention key blocks are 1024 long"
    bf = lambda a: a.astype(BF16)
    x2d = x.reshape(batch * seq, d)
    mem2d = mem.reshape(batch * mem.shape[1], d)
    for l in range(depth):
        lam_init = 0.8 - 0.6 * math.exp(-0.3 * l)
        z = _in_proj(x2d, norm_mix_g[l], bf(w_in[l]), seq)
        oa = _diff_attn_pairs(z, lambda_q1[l], lambda_k1[l], lambda_q2[l], lambda_k2[l],
                              da_subln_g[l], batch, seq, lam_init)
        ob = _sb_attn(z, batch, seq)
        x2d = _merge_out(oa, ob, z, bf(w_proj_a[l]), bf(w_proj_b[l]), bf(w_out[l]), x2d)
        kv = _mem_kv(mem2d, norm_mem_g[l], bf(w_xkv[l]))
        x2d = _xattn(x2d, norm_x_g[l], bf(w_xq[l]), kv, bf(w_xo[l]), seq)
        x2d = _ffn(x2d, norm_ffn_g[l], bf(w_up[l]), conv_w[l], conv_b[l], bf(w_down[l]),
                   final_norm_g, seq, final=(l == depth - 1))
    return x2d.reshape(batch, seq, d)
```

```python
import functools
import math

import jax
import jax.numpy as jnp
from jax import lax
from jax.experimental import pallas as pl
from jax.experimental.pallas import tpu as pltpu

F32 = jnp.float32
BF16 = jnp.bfloat16

D_MODEL = 2048
N_MEM = 256
DA_HEAD_DIM = 64
DA_WIDTH = D_MODEL // 2
DA_HEADS = DA_WIDTH // (2 * DA_HEAD_DIM)
DA_ROT_DIM = DA_HEAD_DIM // 4
SB_HEAD_DIM = 128
SB_WIDTH = D_MODEL // 2
SB_HEADS = SB_WIDTH // SB_HEAD_DIM
XA_HEADS = 4
XA_HEAD_DIM = 128
XA_WIDTH = XA_HEADS * XA_HEAD_DIM
D_FF = 256 * ((8 * D_MODEL // 3 + 255) // 256)
CONV_WIDTH = 3
ROPE_THETA = 500000.0
EPS = 1e-6
N_IN = 3 * DA_WIDTH + 3 * SB_WIDTH + 2 * D_MODEL

LANES = 128
SUBLANES = 8
BF16_ROWS = 16
NEG_BIG = -1e30
LOG2E = math.log2(math.e)
SB_ZERO_LOG2 = -160.0
TRANSPOSE_KEYS = 1024
FROZEN_MAX_RISE = 64.0
VMEM_LIMIT = 48 * 1024 * 1024
VMEM_LIMIT_BIG = 60 * 1024 * 1024

COL_QA, COL_KA, COL_VA = 0, DA_WIDTH, 2 * DA_WIDTH
COL_QB, COL_KB, COL_VB = 3 * DA_WIDTH, 3 * DA_WIDTH + SB_WIDTH, 3 * DA_WIDTH + 2 * SB_WIDTH
COL_GA = 3 * DA_WIDTH + 3 * SB_WIDTH
COL_GB = COL_GA + D_MODEL


def _params(sem, vmem_limit=VMEM_LIMIT):
    return pltpu.CompilerParams(dimension_semantics=sem, vmem_limit_bytes=vmem_limit)


def _rms(x, g):
    return x * lax.rsqrt(jnp.mean(x * x, axis=-1, keepdims=True) + EPS) * g


def _dot(a, b):
    return jnp.dot(a, b, preferred_element_type=F32)


def _dot_nt(a, b):
    return lax.dot_general(a, b, (((1,), (1,)), ((), ())), preferred_element_type=F32)


def _in_proj_kernel(x_ref, g_ref, w_ref, scale_ref, cos_ref, sp_ref, sm_ref, o_ref, h_scr, *, bn):
    j = pl.program_id(1)

    @pl.when(j == 0)
    def _():
        h_scr[...] = _rms(x_ref[...], g_ref[...]).astype(BF16)

    acc = _dot(h_scr[...], w_ref[...]) * scale_ref[...]
    col = j * bn

    @pl.when(col < COL_VA)
    def _():
        cos, sp, sm = cos_ref[...], sp_ref[...], sm_ref[...]
        for c in range(bn // LANES):
            a = acc[:, c * LANES:(c + 1) * LANES]
            r = a * cos + pltpu.roll(a, DA_ROT_DIM // 2, 1) * sp \
                + pltpu.roll(a, LANES - DA_ROT_DIM // 2, 1) * sm
            o_ref[:, c * LANES:(c + 1) * LANES] = r.astype(o_ref.dtype)

    @pl.when(col >= COL_VA)
    def _():
        o_ref[...] = acc.astype(o_ref.dtype)


def _rope_lane_tables(seq):
    half = DA_ROT_DIM // 2
    inv = ROPE_THETA ** (-jnp.arange(0, DA_ROT_DIM, 2, dtype=F32) / DA_ROT_DIM)
    ang = jnp.arange(seq, dtype=F32)[:, None] * inv[None, :]
    cos, sin = jnp.cos(ang), jnp.sin(ang)
    lane = jnp.arange(LANES) % DA_HEAD_DIM
    idx = lane % half
    lo = (lane < half)[None, :]
    hi = jnp.logical_and(lane >= half, lane < 2 * half)[None, :]
    cos_t = jnp.where(jnp.logical_or(lo, hi), cos[:, idx], 1.0)
    sp_t = jnp.where(hi, sin[:, idx], 0.0)
    sm_t = jnp.where(lo, -sin[:, idx], 0.0)
    return cos_t.astype(F32), sp_t.astype(F32), sm_t.astype(F32)


def _in_proj(x2d, g, w, seq, *, bm=1024, bn=2048):
    m, d = x2d.shape
    n = w.shape[1]
    cos_t, sp_t, sm_t = _rope_lane_tables(seq)
    col = jnp.arange(n)
    scale = jnp.where(col < COL_KA, LOG2E * DA_HEAD_DIM ** -0.5,
                      jnp.where((col >= COL_QB) & (col < COL_KB), LOG2E * SB_HEAD_DIM ** -0.5, 1.0))
    scale = scale.astype(F32).reshape(1, n)
    nseq = seq // bm
    tab_spec = pl.BlockSpec((bm, LANES), lambda i, j: (i % nseq, 0))
    return pl.pallas_call(
        functools.partial(_in_proj_kernel, bn=bn),
        out_shape=jax.ShapeDtypeStruct((m, n), BF16),
        grid=(m // bm, n // bn),
        in_specs=[
            pl.BlockSpec((bm, d), lambda i, j: (i, 0)),
            pl.BlockSpec((1, d), lambda i, j: (0, 0)),
            pl.BlockSpec((d, bn), lambda i, j: (0, j)),
            pl.BlockSpec((1, bn), lambda i, j: (0, j)),
            tab_spec, tab_spec, tab_spec,
        ],
        out_specs=pl.BlockSpec((bm, bn), lambda i, j: (i, j)),
        scratch_shapes=[pltpu.VMEM((bm, d), BF16)],
        compiler_params=_params(("parallel", "arbitrary"), VMEM_LIMIT_BIG),
        name="in_proj",
    )(x2d, g.reshape(1, d), w, scale, cos_t, sp_t, sm_t)


def _transpose_values(v_ref, vt_scr, bk):
    hd = v_ref.shape[1]
    extra = vt_scr.shape[1] - hd
    per = max(1, TRANSPOSE_KEYS // bk)
    chunk = per * bk

    def body(c, carry):
        start = pl.multiple_of(c * chunk, chunk)
        vt = v_ref[pl.ds(start, chunk), :].astype(F32).T.astype(BF16)
        for s in range(per):
            vt_scr[c * per + s, :hd, :] = vt[:, s * bk:(s + 1) * bk]
            if extra:
                vt_scr[c * per + s, hd:, :] = jnp.ones((extra, bk), BF16)
        return carry
    lax.fori_loop(0, vt_scr.shape[0] // per, body, 0)


def _diff_head_pair_kernel(q_ref, k_ref, v_ref, lq1_ref, lk1_ref, lq2_ref, lk2_ref, g_ref, o_ref,
                           vt_scr, a_scr, *, bq, bk, nh, lam_init):
    qi = pl.program_id(2)
    hd = 2 * DA_HEAD_DIM
    nmap = 2 * nh

    @pl.when(qi == 0)
    def _():
        for h in range(nh):
            _transpose_values(v_ref.at[:, h * hd:(h + 1) * hd], vt_scr.at[h], bk)

    qts = []
    for h in range(nh):
        qt = q_ref[:, h * hd:(h + 1) * hd].astype(F32).T
        feat = lax.broadcasted_iota(jnp.int32, qt.shape, 0)
        qts.append(jnp.where(feat < DA_HEAD_DIM, qt, 0.0).astype(BF16))
        qts.append(jnp.where(feat >= DA_HEAD_DIM, qt, 0.0).astype(BF16))
    def scores(start, nk, causal):
        k = k_ref[pl.ds(start, nk), :]
        ss = [_dot(k[:, (i // 2) * hd:(i // 2 + 1) * hd], qts[i]) for i in range(nmap)]
        if causal:
            key = start + lax.broadcasted_iota(jnp.int32, (nk, bq), 0)
            qry = qi * bq + lax.broadcasted_iota(jnp.int32, (nk, bq), 1)
            keep = key <= qry
            ss = [jnp.where(keep, s, NEG_BIG) for s in ss]
        return ss


    def online(kb, ms, nkeys=None):
        nk = bk if nkeys is None else nkeys
        ss = scores(pl.multiple_of(kb * bk, bk), nk, nkeys is not None)
        out = []
        for i in range(nmap):
            m_new = jnp.maximum(ms[i], jnp.max(ss[i], axis=0, keepdims=True))
            alpha = jnp.exp2(ms[i] - m_new)
            p = jnp.exp2(ss[i] - m_new).astype(BF16)
            a_scr[i] = alpha * a_scr[i] + _dot(vt_scr[i // 2, kb, :, :nk], p)
            out.append(m_new)
        return tuple(out)

    def frozen(kb, carry, nkeys=None):
        ms, gs = carry
        nk = bk if nkeys is None else nkeys
        ss = scores(pl.multiple_of(kb * bk, bk), nk, False)
        out = []
        for i in range(nmap):
            p = jnp.exp2(ss[i] - ms[i]).astype(BF16)
            a_scr[i] += _dot(vt_scr[i // 2, kb, :, :nk], p)
            out.append(jnp.maximum(gs[i], jnp.max(ss[i], axis=0, keepdims=True)))
        return ms, tuple(out)

    neg = jnp.full((1, bq), NEG_BIG, F32)
    n_full = (qi * bq) // bk
    per = bk // bq
    def run_online():
        a_scr[...] = jnp.zeros(a_scr.shape, F32)
        ms = lax.fori_loop(0, n_full, online, (neg,) * nmap)
        lax.switch(qi % per, [functools.partial(online, n_full, nkeys=(r + 1) * bq) for r in range(per)], ms)

    def diagonal_first(r):
        start = pl.multiple_of(n_full * bk + r * bq, bq)
        ms = []
        for i, s in enumerate(scores(start, bq, True)):
            m = jnp.max(s, axis=0, keepdims=True)
            a_scr[i] = _dot(vt_scr[i // 2, n_full, :, r * bq:(r + 1) * bq], jnp.exp2(s - m).astype(BF16))
            ms.append(m)
        carry = (tuple(ms), tuple(ms))
        return frozen(n_full, carry, nkeys=r * bq) if r else carry

    def run_frozen():
        carry = lax.switch(qi % per, [functools.partial(diagonal_first, r) for r in range(per)])
        n_pairs = n_full // 2
        carry = lax.fori_loop(0, n_pairs, lambda t, c: frozen(2 * t + 1, frozen(2 * t, c)), carry)
        ms, gs = lax.fori_loop(2 * n_pairs, n_full, frozen, carry)
        rise = gs[0] - ms[0]
        for i in range(1, nmap):
            rise = jnp.maximum(rise, gs[i] - ms[i])
        return jnp.max(rise) <= FROZEN_MAX_RISE

    @pl.when(jnp.logical_not(run_frozen()))
    def _():
        run_online()

    lam = (jnp.exp(jnp.sum(lq1_ref[...] * lk1_ref[...], axis=-1, keepdims=True))
           - jnp.exp(jnp.sum(lq2_ref[...] * lk2_ref[...], axis=-1, keepdims=True)) + lam_init)
    for h in range(nh):
        a1, a2 = a_scr[2 * h], a_scr[2 * h + 1]
        ot = a1[:hd, :] * (1.0 / a1[hd:hd + 1, :]) - (lam / a2[hd:hd + 1, :]) * a2[:hd, :]
        ot = ot * lax.rsqrt(jnp.mean(ot * ot, axis=0, keepdims=True) + EPS)
        o = ot.T * (g_ref[...] * (1.0 - lam_init))
        o_ref[:, h * hd:(h + 1) * hd] = o.astype(o_ref.dtype)


def _diff_attn_pairs(z, lq1, lk1, lq2, lk2, subln_g, batch, seq, lam_init, *, bq=256, bk=1024, nh=4):
    nq = seq // bq
    hd = 2 * DA_HEAD_DIM
    w = nh * hd
    vec = lambda a: a.reshape(1, -1).astype(F32)
    small = lambda n: pl.BlockSpec((1, n), lambda b, h, i: (0, 0))
    return pl.pallas_call(
        functools.partial(_diff_head_pair_kernel, bq=bq, bk=bk, nh=nh, lam_init=lam_init),
        out_shape=jax.ShapeDtypeStruct((batch * seq, DA_WIDTH), BF16),
        grid=(batch, DA_HEADS // nh, nq),
        in_specs=[
            pl.BlockSpec((bq, w), lambda b, h, i: (b * nq + i, COL_QA // w + h)),
            pl.BlockSpec((seq, w), lambda b, h, i: (b, COL_KA // w + h), pipeline_mode=pl.Buffered(1)),
            pl.BlockSpec((seq, w), lambda b, h, i: (b, COL_VA // w + h), pipeline_mode=pl.Buffered(1)),
            small(DA_HEAD_DIM), small(DA_HEAD_DIM), small(DA_HEAD_DIM), small(DA_HEAD_DIM),
            small(hd),
        ],
        out_specs=pl.BlockSpec((bq, w), lambda b, h, i: (b * nq + i, h)),
        scratch_shapes=[
            pltpu.VMEM((nh, seq // bk, hd + BF16_ROWS, bk), BF16),
            pltpu.VMEM((2 * nh, hd + BF16_ROWS, bq), F32),
        ],
        compiler_params=_params(("arbitrary", "arbitrary", "arbitrary")),
        name="diff_attn",
    )(z, z, z, vec(lq1), vec(lk1), vec(lq2), vec(lk2), vec(subln_g))


def _sb_attn_kernel(q_ref, k_ref, v_ref, u_ref, o_ref, vt_scr, a_scr, *, bq, nh):
    qi = pl.program_id(2)
    hd = SB_HEAD_DIM
    bc = bq

    @pl.when(qi == 0)
    def _():
        for h in range(nh):
            _transpose_values(v_ref.at[:, h * hd:(h + 1) * hd], vt_scr.at[h], bc)

    qts = [q_ref[:, h * hd:(h + 1) * hd].astype(F32).T.astype(BF16) for h in range(nh)]
    a_scr[...] = jnp.zeros(a_scr.shape, F32)
    u = u_ref[...]

    def pair(sb0, cs, bound):
        start = pl.multiple_of(sb0 * bc, bc)
        k = k_ref[pl.ds(start, 2 * bc), :]
        zs = [_dot(k[:, h * hd:(h + 1) * hd], qts[h]) for h in range(nh)]
        if bound is not None:
            valid = start + lax.broadcasted_iota(jnp.int32, (2 * bc, bq), 0) < bound
        out = []
        for h in range(nh):
            z = zs[h]
            sp = jnp.log2(1.0 + jnp.exp2(-jnp.abs(z)))
            log_beta = jnp.minimum(z, 0.0) - sp
            log_1m = log_beta - z
            if bound is not None:
                log_1m = jnp.where(valid, log_1m, 0.0)
            l16 = log_1m.astype(BF16)
            c = cs[h]
            tails = [None, None]
            for sb in (1, 0):
                blk = l16[sb * bc:(sb + 1) * bc, :]
                t = _dot(u, blk) + c
                tails[sb] = t
                c = t[0:1, :] + blk[0:1, :].astype(F32)
            a = jnp.exp2(log_beta + jnp.concatenate(tails, axis=0))
            if bound is not None:
                a = jnp.where(valid, a, 0.0)
            a16 = a.astype(BF16)
            a_scr[h] += _dot(vt_scr[h, sb0], a16[:bc, :]) + _dot(vt_scr[h, sb0 + 1], a16[bc:, :])
            out.append(c)
        return tuple(out)

    def live(cs):
        m = cs[0]
        for c in cs[1:]:
            m = jnp.maximum(m, c)
        return jnp.max(m) > SB_ZERO_LOG2

    first = jnp.maximum(qi - 1, 0)
    qry = qi * bq + lax.broadcasted_iota(jnp.int32, (1, bq), 1)
    cs = pair(first, (jnp.zeros((1, bq), F32),) * nh, qry)

    n_pairs = first // 2

    def more(state):
        t, cc = state
        return jnp.logical_and(t < n_pairs, live(cc))

    def walk(state):
        t, cc = state
        return t + 1, pair(first - 2 - 2 * t, cc, None)

    t_end, cs = lax.while_loop(more, walk, (jnp.int32(0), cs))

    @pl.when(jnp.logical_and(jnp.logical_and(first % 2 == 1, t_end == n_pairs), live(cs)))
    def _():
        pair(0, cs, jnp.full((1, bq), bc, jnp.int32))

    for h in range(nh):
        o_ref[:, h * hd:(h + 1) * hd] = a_scr[h].T.astype(o_ref.dtype)


def _sb_attn(z, batch, seq, *, bq=256, nh=4):
    nq = seq // bq
    hd = SB_HEAD_DIM
    w = nh * hd
    r = lax.broadcasted_iota(jnp.int32, (bq, bq), 0)
    c = lax.broadcasted_iota(jnp.int32, (bq, bq), 1)
    u = (c > r).astype(BF16)
    return pl.pallas_call(
        functools.partial(_sb_attn_kernel, bq=bq, nh=nh),
        out_shape=jax.ShapeDtypeStruct((batch * seq, SB_WIDTH), BF16),
        grid=(batch, SB_HEADS // nh, nq),
        in_specs=[
            pl.BlockSpec((bq, w), lambda b, h, i: (b * nq + i, COL_QB // w + h)),
            pl.BlockSpec((seq, w), lambda b, h, i: (b, COL_KB // w + h), pipeline_mode=pl.Buffered(1)),
            pl.BlockSpec((seq, w), lambda b, h, i: (b, COL_VB // w + h), pipeline_mode=pl.Buffered(1)),
            pl.BlockSpec((bq, bq), lambda b, h, i: (0, 0)),
        ],
        out_specs=pl.BlockSpec((bq, w), lambda b, h, i: (b * nq + i, h)),
        scratch_shapes=[pltpu.VMEM((nh, seq // bq, hd, bq), BF16), pltpu.VMEM((nh, hd, bq), F32)],
        compiler_params=_params(("arbitrary", "arbitrary", "arbitrary")),
        name="sb_attn",
    )(z, z, z, u)


def _sigmoid(x):
    return 1.0 / (1.0 + jnp.exp(-x))


def _merge_out_kernel(oa_ref, ob_ref, ga_ref, gb_ref, wa_ref, wb_ref, wo_ref, x_ref, o_ref):
    ya = _dot(oa_ref[...], wa_ref[...])
    yb = _dot(ob_ref[...], wb_ref[...])
    ga = _sigmoid(ga_ref[...].astype(F32))
    gb = _sigmoid(gb_ref[...].astype(F32))
    merged = (ga * ya + gb * yb).astype(BF16)
    o_ref[...] = x_ref[...] + _dot(merged, wo_ref[...])


def _merge_out(oa, ob, z, wa, wb, wo, x2d, *, bm=512):
    m, d = x2d.shape
    resident = lambda shape: pl.BlockSpec(shape, lambda i: (0, 0), pipeline_mode=pl.Buffered(1))
    return pl.pallas_call(
        _merge_out_kernel,
        out_shape=jax.ShapeDtypeStruct((m, d), F32),
        grid=(m // bm,),
        in_specs=[
            pl.BlockSpec((bm, DA_WIDTH), lambda i: (i, 0)),
            pl.BlockSpec((bm, SB_WIDTH), lambda i: (i, 0)),
            pl.BlockSpec((bm, d), lambda i: (i, COL_GA // d)),
            pl.BlockSpec((bm, d), lambda i: (i, COL_GB // d)),
            resident((DA_WIDTH, d)), resident((SB_WIDTH, d)), resident((d, d)),
            pl.BlockSpec((bm, d), lambda i: (i, 0)),
        ],
        out_specs=pl.BlockSpec((bm, d), lambda i: (i, 0)),
        compiler_params=_params(("parallel",), VMEM_LIMIT_BIG),
        name="merge_out",
    )(oa, ob, z, z, wa, wb, wo, x2d)


def _mem_kv_kernel(x_ref, g_ref, w_ref, o_ref):
    h = _rms(x_ref[...], g_ref[...]).astype(BF16)
    o_ref[...] = _dot(h, w_ref[...]).astype(o_ref.dtype)


def _mem_kv(mem2d, g, w, *, bm=256):
    m, d = mem2d.shape
    n = w.shape[1]
    return pl.pallas_call(
        _mem_kv_kernel,
        out_shape=jax.ShapeDtypeStruct((m, n), BF16),
        grid=(m // bm,),
        in_specs=[
            pl.BlockSpec((bm, d), lambda i: (i, 0)),
            pl.BlockSpec((1, d), lambda i: (0, 0)),
            pl.BlockSpec((d, n), lambda i: (0, 0)),
        ],
        out_specs=pl.BlockSpec((bm, n), lambda i: (i, 0)),
        compiler_params=_params(("parallel",)),
        name="mem_kv",
    )(mem2d, g.reshape(1, d), w)


def _xattn_kernel(x_ref, g_ref, wq_ref, kv_ref, wo_ref, o_ref, oh_scr):
    x = x_ref[...]
    h = _rms(x, g_ref[...]).astype(BF16)
    q = (_dot(h, wq_ref[...]) * (XA_HEAD_DIM ** -0.5)).astype(BF16)
    for hh in range(XA_HEADS):
        lo = hh * XA_HEAD_DIM
        k = kv_ref[:, lo:lo + XA_HEAD_DIM]
        v = kv_ref[:, XA_WIDTH + lo:XA_WIDTH + lo + XA_HEAD_DIM]
        s = _dot_nt(q[:, lo:lo + XA_HEAD_DIM], k)
        p = jnp.exp(s - jnp.max(s, axis=-1, keepdims=True))
        p = p / jnp.sum(p, axis=-1, keepdims=True)
        oh_scr[:, lo:lo + XA_HEAD_DIM] = _dot(p.astype(BF16), v).astype(BF16)
    o_ref[...] = x + _dot(oh_scr[...], wo_ref[...])


def _xattn(x2d, g, wq, kv, wo, seq, *, bm=1024):
    m, d = x2d.shape
    nseq = seq // bm
    return pl.pallas_call(
        _xattn_kernel,
        out_shape=jax.ShapeDtypeStruct((m, d), F32),
        grid=(m // bm,),
        in_specs=[
            pl.BlockSpec((bm, d), lambda i: (i, 0)),
            pl.BlockSpec((1, d), lambda i: (0, 0)),
            pl.BlockSpec((d, XA_WIDTH), lambda i: (0, 0)),
            pl.BlockSpec((N_MEM, 2 * XA_WIDTH), lambda i: (i // nseq, 0)),
            pl.BlockSpec((XA_WIDTH, d), lambda i: (0, 0)),
        ],
        out_specs=pl.BlockSpec((bm, d), lambda i: (i, 0)),
        scratch_shapes=[pltpu.VMEM((bm, XA_WIDTH), BF16)],
        compiler_params=_params(("parallel",)),
        name="xattn",
    )(x2d, g.reshape(1, d), wq, kv, wo)


def _ffn_kernel(x_ref, halo_ref, g_ref, wg_ref, wv_ref, cwg_ref, cwv_ref, cbg_ref, cbv_ref,
                wd_ref, fg_ref, o_ref, h_scr, *, nseq, final):
    i = pl.program_id(0)
    f = pl.program_id(1)
    pad = SUBLANES

    @pl.when(f == 0)
    def _():
        g = g_ref[...]
        h_scr[pad:, :] = _rms(x_ref[...], g).astype(BF16)
        keep = (i % nseq != 0).astype(F32)
        h_scr[:pad, :] = (_rms(halo_ref[...], g) * keep).astype(BF16)
        o_ref[...] = jnp.zeros(o_ref.shape, F32)

    h = h_scr[...]

    def conv(w_ref, cw_ref, cb_ref):
        u = _dot(h, w_ref[...])
        cw = cw_ref[...]
        y = (cw[2:3, :] * u + cw[1:2, :] * pltpu.roll(u, 1, 0) + cw[0:1, :] * pltpu.roll(u, 2, 0))
        return y[pad:, :] + cb_ref[...]

    gate = conv(wg_ref, cwg_ref, cbg_ref)
    val = conv(wv_ref, cwv_ref, cbv_ref)
    act = (gate * _sigmoid(gate) * val).astype(BF16)
    o_ref[...] += _dot(act, wd_ref[...])

    @pl.when(f == pl.num_programs(1) - 1)
    def _():
        y = x_ref[...] + o_ref[...]
        o_ref[...] = _rms(y, fg_ref[...]) if final else y


def _ffn(x2d, g, w_up, conv_w, conv_b, w_down, final_g, seq, *, final, bm=1024, bf=512):
    m, d = x2d.shape
    nf = D_FF // bf
    nseq = seq // bm
    hb = bm // SUBLANES
    return pl.pallas_call(
        functools.partial(_ffn_kernel, nseq=nseq, final=final),
        out_shape=jax.ShapeDtypeStruct((m, d), F32),
        grid=(m // bm, nf),
        in_specs=[
            pl.BlockSpec((bm, d), lambda i, f: (i, 0), pipeline_mode=pl.Buffered(1)),
            pl.BlockSpec((SUBLANES, d), lambda i, f: (jnp.maximum(i * hb - 1, 0), 0)),
            pl.BlockSpec((1, d), lambda i, f: (0, 0)),
            pl.BlockSpec((d, bf), lambda i, f: (0, f)),
            pl.BlockSpec((d, bf), lambda i, f: (0, nf + f)),
            pl.BlockSpec((CONV_WIDTH, bf), lambda i, f: (0, f)),
            pl.BlockSpec((CONV_WIDTH, bf), lambda i, f: (0, nf + f)),
            pl.BlockSpec((1, bf), lambda i, f: (0, f)),
            pl.BlockSpec((1, bf), lambda i, f: (0, nf + f)),
            pl.BlockSpec((bf, d), lambda i, f: (f, 0)),
            pl.BlockSpec((1, d), lambda i, f: (0, 0)),
        ],
        out_specs=pl.BlockSpec((bm, d), lambda i, f: (i, 0)),
        scratch_shapes=[pltpu.VMEM((bm + SUBLANES, d), BF16)],
        compiler_params=_params(("parallel", "arbitrary"), VMEM_LIMIT_BIG),
        name="ffn",
    )(x2d, x2d, g.reshape(1, d), w_up, w_up, conv_w, conv_w,
      conv_b.reshape(1, -1), conv_b.reshape(1, -1), w_down, final_g.reshape(1, d))


def kernel(x, mem, norm_mix_g, w_in, lambda_q1, lambda_k1, lambda_q2, lambda_k2, da_subln_g,
           w_proj_a, w_proj_b, w_out, norm_x_g, norm_mem_g, w_xq, w_xkv, w_xo, norm_ffn_g,
           w_up, conv_w, conv_b, w_down, final_norm_g):
    batch, seq, d = x.shape
    depth = w_in.shape[0]
    assert d == D_MODEL and mem.shape[1:] == (N_MEM, D_MODEL) and w_in.shape[1:] == (D_MODEL, N_IN)
    assert seq % TRANSPOSE_KEYS == 0, "token tiles and attention key blocks are 1024 long"
    bf = lambda a: a.astype(BF16)
    x2d = x.reshape(batch * seq, d)
    mem2d = mem.reshape(batch * mem.shape[1], d)
    for l in range(depth):
        lam_init = 0.8 - 0.6 * math.exp(-0.3 * l)
        z = _in_proj(x2d, norm_mix_g[l], bf(w_in[l]), seq)
        oa = _diff_attn_pairs(z, lambda_q1[l], lambda_k1[l], lambda_q2[l], lambda_k2[l],
                              da_subln_g[l], batch, seq, lam_init)
        ob = _sb_attn(z, batch, seq)
        x2d = _merge_out(oa, ob, z, bf(w_proj_a[l]), bf(w_proj_b[l]), bf(w_out[l]), x2d)
        kv = _mem_kv(mem2d, norm_mem_g[l], bf(w_xkv[l]))
        x2d = _xattn(x2d, norm_x_g[l], bf(w_xq[l]), kv, bf(w_xo[l]), seq)
        x2d = _ffn(x2d, norm_ffn_g[l], bf(w_up[l]), conv_w[l], conv_b[l], bf(w_down[l]),
                   final_norm_g, seq, final=(l == depth - 1))
    return x2d.reshape(batch, seq, d)
```
